```python
import math
import jax, jax.numpy as jnp
from jax import lax
import numpy as np

D_MODEL = 2048
BATCH = 8
SEQ = 8192
DEPTH = 4

HEAD_DIM = 128
MLA_HEADS = 8
MLA_Q_RANK = 512
MLA_KV_RANK = 512
MLA_NOPE_DIM = 128
MLA_ROPE_DIM = 64
MLA_V_DIM = 128
MLA_QK_DIM = MLA_NOPE_DIM + MLA_ROPE_DIM
DIL_HEADS = 8
DIL_PATTERNS = ((128, 1), (512, 4), (2048, 16))
ROPE_THETA = 500000.0
PARTIAL_ROPE_DIM = HEAD_DIM // 4
D_FF = 5632
Q_BLOCK = 128
RMS_EPS = 1e-6
NEG = -1e30
MLA_WIDTH = MLA_HEADS * MLA_V_DIM
DIL_WIDTH = DIL_HEADS * HEAD_DIM
D_MIX = MLA_WIDTH + DIL_WIDTH
IN_SIZES = (MLA_Q_RANK, MLA_KV_RANK, MLA_ROPE_DIM, DIL_WIDTH, DIL_WIDTH, DIL_WIDTH)
IN_COLS = sum(IN_SIZES)
IN_SPLITS = tuple(int(v) for v in np.cumsum(IN_SIZES)[:-1])

kernel_name = "hymba_mla_dilated_macaron_sandwich"


def rms_norm(x, g):
    xf = x.astype(jnp.float32)
    y = xf * lax.rsqrt(jnp.mean(xf * xf, axis=-1, keepdims=True) + RMS_EPS)
    return (y * g.astype(jnp.float32)).astype(x.dtype)


def swiglu(h, w_gate, w_up, w_down):
    return (jax.nn.silu(h @ w_gate) * (h @ w_up)) @ w_down


def rope_tables(positions, dim):
    inv = ROPE_THETA ** (-jnp.arange(0, dim, 2, dtype=jnp.float32) / dim)
    ang = positions.astype(jnp.float32)[..., None] * inv
    return jnp.cos(ang), jnp.sin(ang)


def apply_rope(x, cos, sin):
    xf = x.astype(jnp.float32)
    x1, x2 = jnp.split(xf, 2, axis=-1)
    out = jnp.concatenate([x1 * cos - x2 * sin, x2 * cos + x1 * sin], axis=-1)
    return out.astype(x.dtype)


def mla_attention(q_nope, q_rope, k_nope, k_rope, v):
    B, S, H, _ = q_nope.shape
    nb = S // Q_BLOCK
    scale = 1.0 / math.sqrt(MLA_QK_DIM)
    key_idx = jnp.arange(S)

    def to_blocks(t):
        return jnp.moveaxis(t.reshape(B, nb, Q_BLOCK, *t.shape[2:]), 1, 0)

    def one_block(args):
        qn_b, qr_b, i = args
        s = (jnp.einsum('bqhd,bkhd->bhqk', qn_b, k_nope).astype(jnp.float32)
             + jnp.einsum('bqhr,bkr->bhqk', qr_b, k_rope).astype(jnp.float32)) * scale
        q_idx = i * Q_BLOCK + jnp.arange(Q_BLOCK)
        mask = key_idx[None, :] <= q_idx[:, None]
        s = jnp.where(mask, s, NEG)
        p = jax.nn.softmax(s, axis=-1).astype(v.dtype)
        return jnp.einsum('bhqk,bkhd->bqhd', p, v)

    o = lax.map(one_block, (to_blocks(q_nope), to_blocks(q_rope), jnp.arange(nb)))
    return jnp.moveaxis(o, 0, 1).reshape(B, S, H, -1)


def dilated_window_attention(q, k, v, window, dilation):
    B, S, H, Dh = q.shape
    span = window // dilation
    blk = span
    seg = blk * dilation
    s_pad = -(-S // seg) * seg
    L = s_pad // dilation
    nb = L // blk
    scale = 1.0 / math.sqrt(Dh)

    def to_sub(t):
        t = jnp.pad(t, ((0, 0), (0, s_pad - S), (0, 0), (0, 0)))
        t = t.reshape(B, L, dilation, H, Dh).transpose(0, 2, 1, 3, 4)
        return t.reshape(B, dilation, nb, blk, H, Dh)

    def with_prev(t):
        prev = jnp.pad(t[:, :, :-1], ((0, 0), (0, 0), (1, 0), (0, 0), (0, 0), (0, 0)))
        return jnp.concatenate([prev, t], axis=3)

    qb = to_sub(q)
    kc = with_prev(to_sub(k))
    vc = with_prev(to_sub(v))
    s = jnp.einsum('brnqhd,brnkhd->brnhqk', qb, kc).astype(jnp.float32) * scale
    qi = jnp.arange(blk)[:, None]
    kj = jnp.arange(2 * blk)[None, :]
    dist = blk + qi - kj
    band = (dist >= 0) & (dist <= span)
    valid_prev = (jnp.arange(nb)[:, None, None] > 0) | (kj >= blk)[None]
    mask = band[None] & valid_prev
    s = jnp.where(mask[None, None, :, None], s, NEG)
    lse = jax.nn.logsumexp(s, axis=-1)
    p = jnp.exp(s - lse[..., None]).astype(v.dtype)
    o = jnp.einsum('brnhqk,brnkhd->brnqhd', p, vc)

    def from_sub(t):
        rest = t.shape[4:]
        t = t.reshape(B, dilation, L, *rest)
        t = jnp.moveaxis(t, 1, 2).reshape(B, s_pad, *rest)
        return t[:, :S]

    return from_sub(o), from_sub(jnp.moveaxis(lse, -1, -2))


def dilated_mixture(q, k, v):
    outs, lses = [], []
    for window, dilation in DIL_PATTERNS:
        o_p, lse_p = dilated_window_attention(q, k, v, window, dilation)
        outs.append(o_p)
        lses.append(lse_p)
    w = jax.nn.softmax(jnp.stack(lses, axis=0), axis=0)
    return jnp.einsum('pbsh,pbshd->bshd', w.astype(q.dtype), jnp.stack(outs, axis=0))


def _fwd_setup_inputs(seed: int = 0) -> dict:
    key = jax.random.key(seed)
    ks = jax.random.split(key, 24)

    def w(k, shape, fan_in):
        return jax.random.normal(k, shape, jnp.float32) * (fan_in ** -0.5)

    def gain(k, n):
        return 1.0 + 0.1 * jax.random.normal(k, (DEPTH, n), jnp.float32)

    x = jax.random.normal(ks[0], (BATCH, SEQ, D_MODEL), jnp.float32)
    offset = jax.random.randint(ks[1], (BATCH, 1), 0, 1024, dtype=jnp.int32)
    positions = (offset + jnp.arange(SEQ, dtype=jnp.int32)[None, :]).astype(jnp.int32)
    return {
        "x": x,
        "positions": positions,
        "ffn1_pre_g": gain(ks[2], D_MODEL),
        "ffn1_post_g": gain(ks[3], D_MODEL),
        "ffn1_w_gate": w(ks[4], (DEPTH, D_MODEL, D_FF), D_MODEL),
        "ffn1_w_up": w(ks[5], (DEPTH, D_MODEL, D_FF), D_MODEL),
        "ffn1_w_down": w(ks[6], (DEPTH, D_FF, D_MODEL), D_FF),
        "mix_pre_g": gain(ks[7], D_MODEL),
        "mix_post_g": gain(ks[8], D_MODEL),
        "w_in": w(ks[9], (DEPTH, D_MODEL, IN_COLS), D_MODEL),
        "mla_q_norm_g": gain(ks[10], MLA_Q_RANK),
        "mla_w_uq": w(ks[11], (DEPTH, MLA_Q_RANK, MLA_HEADS * MLA_QK_DIM), MLA_Q_RANK),
        "mla_kv_norm_g": gain(ks[12], MLA_KV_RANK),
        "mla_w_ukv": w(ks[13], (DEPTH, MLA_KV_RANK, MLA_HEADS * (MLA_NOPE_DIM + MLA_V_DIM)), MLA_KV_RANK),
        "w_o": w(ks[14], (DEPTH, D_MIX, D_MODEL), D_MIX),
        "ffn2_pre_g": gain(ks[15], D_MODEL),
        "ffn2_post_g": gain(ks[16], D_MODEL),
        "ffn2_w_gate": w(ks[17], (DEPTH, D_MODEL, D_FF), D_MODEL),
        "ffn2_w_up": w(ks[18], (DEPTH, D_MODEL, D_FF), D_MODEL),
        "ffn2_w_down": w(ks[19], (DEPTH, D_FF, D_MODEL), D_FF),
    }


def _fwd_reference(x, positions, ffn1_pre_g, ffn1_post_g, ffn1_w_gate, ffn1_w_up, ffn1_w_down,
              mix_pre_g, mix_post_g, w_in, mla_q_norm_g, mla_w_uq, mla_kv_norm_g, mla_w_ukv,
              w_o, ffn2_pre_g, ffn2_post_g, ffn2_w_gate, ffn2_w_up, ffn2_w_down):
    B, S, _ = x.shape
    cos_a, sin_a = rope_tables(positions, MLA_ROPE_DIM)
    cos_p, sin_p = rope_tables(positions, PARTIAL_ROPE_DIM)

    def partial_rope(t):
        return jnp.concatenate([apply_rope(t[..., :PARTIAL_ROPE_DIM], cos_p[:, :, None], sin_p[:, :, None]),
                                t[..., PARTIAL_ROPE_DIM:]], axis=-1)

    for l in range(DEPTH):
        h = rms_norm(x, ffn1_pre_g[l])
        x = x + 0.5 * rms_norm(swiglu(h, ffn1_w_gate[l], ffn1_w_up[l], ffn1_w_down[l]), ffn1_post_g[l])

        h = rms_norm(x, mix_pre_g[l])
        proj = h @ w_in[l]
        c_q, c_kv, k_rope, q_d, k_d, v_d = jnp.split(proj, IN_SPLITS, axis=-1)

        q_a = (rms_norm(c_q, mla_q_norm_g[l]) @ mla_w_uq[l]).reshape(B, S, MLA_HEADS, MLA_QK_DIM)
        q_nope, q_rope = q_a[..., :MLA_NOPE_DIM], q_a[..., MLA_NOPE_DIM:]
        q_rope = apply_rope(q_rope, cos_a[:, :, None], sin_a[:, :, None])
        k_rope = apply_rope(k_rope, cos_a, sin_a)
        kv = (rms_norm(c_kv, mla_kv_norm_g[l]) @ mla_w_ukv[l]).reshape(B, S, MLA_HEADS, MLA_NOPE_DIM + MLA_V_DIM)
        k_nope, v_a = kv[..., :MLA_NOPE_DIM], kv[..., MLA_NOPE_DIM:]
        o_a = mla_attention(q_nope, q_rope, k_nope, k_rope, v_a)

        q_b = partial_rope(q_d.reshape(B, S, DIL_HEADS, HEAD_DIM))
        k_b = partial_rope(k_d.reshape(B, S, DIL_HEADS, HEAD_DIM))
        v_b = v_d.reshape(B, S, DIL_HEADS, HEAD_DIM)
        o_b = dilated_mixture(q_b, k_b, v_b)

        o = jnp.concatenate([o_a.reshape(B, S, MLA_WIDTH), o_b.reshape(B, S, DIL_WIDTH)], axis=-1) @ w_o[l]
        x = x + rms_norm(o, mix_post_g[l])

        h = rms_norm(x, ffn2_pre_g[l])
        x = x + 0.5 * rms_norm(swiglu(h, ffn2_w_gate[l], ffn2_w_up[l], ffn2_w_down[l]), ffn2_post_g[l])
    return x


import jax as _jax
import jax.numpy as _jnp

TWIN_FORMAT = 'train_step'
FWD_PARAMS = ['x', 'positions', 'ffn1_pre_g', 'ffn1_post_g', 'ffn1_w_gate', 'ffn1_w_up', 'ffn1_w_down', 'mix_pre_g', 'mix_post_g', 'w_in', 'mla_q_norm_g', 'mla_w_uq', 'mla_kv_norm_g', 'mla_w_ukv', 'w_o', 'ffn2_pre_g', 'ffn2_post_g', 'ffn2_w_gate', 'ffn2_w_up', 'ffn2_w_down']
TWIN_WEIGHTS = ['ffn1_pre_g', 'ffn1_post_g', 'ffn1_w_gate', 'ffn1_w_up', 'ffn1_w_down', 'mix_pre_g', 'mix_post_g', 'w_in', 'mla_q_norm_g', 'mla_w_uq', 'mla_kv_norm_g', 'mla_w_ukv', 'w_o', 'ffn2_pre_g', 'ffn2_post_g', 'ffn2_w_gate', 'ffn2_w_up', 'ffn2_w_down']
TWIN_DIFF_INPUT = 'x'
TWIN_INPUTS = ['x', 'positions', 'ffn1_pre_g', 'ffn1_post_g', 'ffn1_w_gate', 'ffn1_w_up', 'ffn1_w_down', 'mix_pre_g', 'mix_post_g', 'w_in', 'mla_q_norm_g', 'mla_w_uq', 'mla_kv_norm_g', 'mla_w_ukv', 'w_o', 'ffn2_pre_g', 'ffn2_post_g', 'ffn2_w_gate', 'ffn2_w_up', 'ffn2_w_down', 'loss_target', 'm_ffn1_pre_g', 'm_ffn1_post_g', 'm_ffn1_w_gate', 'm_ffn1_w_up', 'm_ffn1_w_down', 'm_mix_pre_g', 'm_mix_post_g', 'm_w_in', 'm_mla_q_norm_g', 'm_mla_w_uq', 'm_mla_kv_norm_g', 'm_mla_w_ukv', 'm_w_o', 'm_ffn2_pre_g', 'm_ffn2_post_g', 'm_ffn2_w_gate', 'm_ffn2_w_up', 'm_ffn2_w_down', 'v_ffn1_pre_g', 'v_ffn1_post_g', 'v_ffn1_w_gate', 'v_ffn1_w_up', 'v_ffn1_w_down', 'v_mix_pre_g', 'v_mix_post_g', 'v_w_in', 'v_mla_q_norm_g', 'v_mla_w_uq', 'v_mla_kv_norm_g', 'v_mla_w_ukv', 'v_w_o', 'v_ffn2_pre_g', 'v_ffn2_post_g', 'v_ffn2_w_gate', 'v_ffn2_w_up', 'v_ffn2_w_down']
TWIN_OUTPUTS = ['loss', 'grad_x', 'grad_ffn1_pre_g', 'grad_ffn1_post_g', 'grad_ffn1_w_gate', 'grad_ffn1_w_up', 'grad_ffn1_w_down', 'grad_mix_pre_g', 'grad_mix_post_g', 'grad_w_in', 'grad_mla_q_norm_g', 'grad_mla_w_uq', 'grad_mla_kv_norm_g', 'grad_mla_w_ukv', 'grad_w_o', 'grad_ffn2_pre_g', 'grad_ffn2_post_g', 'grad_ffn2_w_gate', 'grad_ffn2_w_up', 'grad_ffn2_w_down', 'delta_ffn1_pre_g', 'delta_ffn1_post_g', 'delta_ffn1_w_gate', 'delta_ffn1_w_up', 'delta_ffn1_w_down', 'delta_mix_pre_g', 'delta_mix_post_g', 'delta_w_in', 'delta_mla_q_norm_g', 'delta_mla_w_uq', 'delta_mla_kv_norm_g', 'delta_mla_w_ukv', 'delta_w_o', 'delta_ffn2_pre_g', 'delta_ffn2_post_g', 'delta_ffn2_w_gate', 'delta_ffn2_w_up', 'delta_ffn2_w_down', 'new_m_ffn1_pre_g', 'new_m_ffn1_post_g', 'new_m_ffn1_w_gate', 'new_m_ffn1_w_up', 'new_m_ffn1_w_down', 'new_m_mix_pre_g', 'new_m_mix_post_g', 'new_m_w_in', 'new_m_mla_q_norm_g', 'new_m_mla_w_uq', 'new_m_mla_kv_norm_g', 'new_m_mla_w_ukv', 'new_m_w_o', 'new_m_ffn2_pre_g', 'new_m_ffn2_post_g', 'new_m_ffn2_w_gate', 'new_m_ffn2_w_up', 'new_m_ffn2_w_down', 'new_v_ffn1_pre_g', 'new_v_ffn1_post_g', 'new_v_ffn1_w_gate', 'new_v_ffn1_w_up', 'new_v_ffn1_w_down', 'new_v_mix_pre_g', 'new_v_mix_post_g', 'new_v_w_in', 'new_v_mla_q_norm_g', 'new_v_mla_w_uq', 'new_v_mla_kv_norm_g', 'new_v_mla_w_ukv', 'new_v_w_o', 'new_v_ffn2_pre_g', 'new_v_ffn2_post_g', 'new_v_ffn2_w_gate', 'new_v_ffn2_w_up', 'new_v_ffn2_w_down']
TWIN_LEAF_KINDS = {'loss': 'loss', 'grad_x': 'grad_x', 'grad_ffn1_pre_g': 'grad_w', 'grad_ffn1_post_g': 'grad_w', 'grad_ffn1_w_gate': 'grad_w', 'grad_ffn1_w_up': 'grad_w', 'grad_ffn1_w_down': 'grad_w', 'grad_mix_pre_g': 'grad_w', 'grad_mix_post_g': 'grad_w', 'grad_w_in': 'grad_w', 'grad_mla_q_norm_g': 'grad_w', 'grad_mla_w_uq': 'grad_w', 'grad_mla_kv_norm_g': 'grad_w', 'grad_mla_w_ukv': 'grad_w', 'grad_w_o': 'grad_w', 'grad_ffn2_pre_g': 'grad_w', 'grad_ffn2_post_g': 'grad_w', 'grad_ffn2_w_gate': 'grad_w', 'grad_ffn2_w_up': 'grad_w', 'grad_ffn2_w_down': 'grad_w', 'delta_ffn1_pre_g': 'delta_w', 'delta_ffn1_post_g': 'delta_w', 'delta_ffn1_w_gate': 'delta_w', 'delta_ffn1_w_up': 'delta_w', 'delta_ffn1_w_down': 'delta_w', 'delta_mix_pre_g': 'delta_w', 'delta_mix_post_g': 'delta_w', 'delta_w_in': 'delta_w', 'delta_mla_q_norm_g': 'delta_w', 'delta_mla_w_uq': 'delta_w', 'delta_mla_kv_norm_g': 'delta_w', 'delta_mla_w_ukv': 'delta_w', 'delta_w_o': 'delta_w', 'delta_ffn2_pre_g': 'delta_w', 'delta_ffn2_post_g': 'delta_w', 'delta_ffn2_w_gate': 'delta_w', 'delta_ffn2_w_up': 'delta_w', 'delta_ffn2_w_down': 'delta_w', 'new_m_ffn1_pre_g': 'new_m', 'new_m_ffn1_post_g': 'new_m', 'new_m_ffn1_w_gate': 'new_m', 'new_m_ffn1_w_up': 'new_m', 'new_m_ffn1_w_down': 'new_m', 'new_m_mix_pre_g': 'new_m', 'new_m_mix_post_g': 'new_m', 'new_m_w_in': 'new_m', 'new_m_mla_q_norm_g': 'new_m', 'new_m_mla_w_uq': 'new_m', 'new_m_mla_kv_norm_g': 'new_m', 'new_m_mla_w_ukv': 'new_m', 'new_m_w_o': 'new_m', 'new_m_ffn2_pre_g': 'new_m', 'new_m_ffn2_post_g': 'new_m', 'new_m_ffn2_w_gate': 'new_m', 'new_m_ffn2_w_up': 'new_m', 'new_m_ffn2_w_down': 'new_m', 'new_v_ffn1_pre_g': 'new_v', 'new_v_ffn1_post_g': 'new_v', 'new_v_ffn1_w_gate': 'new_v', 'new_v_ffn1_w_up': 'new_v', 'new_v_ffn1_w_down': 'new_v', 'new_v_mix_pre_g': 'new_v', 'new_v_mix_post_g': 'new_v', 'new_v_w_in': 'new_v', 'new_v_mla_q_norm_g': 'new_v', 'new_v_mla_w_uq': 'new_v', 'new_v_mla_kv_norm_g': 'new_v', 'new_v_mla_w_ukv': 'new_v', 'new_v_w_o': 'new_v', 'new_v_ffn2_pre_g': 'new_v', 'new_v_ffn2_post_g': 'new_v', 'new_v_ffn2_w_gate': 'new_v', 'new_v_ffn2_w_up': 'new_v', 'new_v_ffn2_w_down': 'new_v'}


def _forward(args):
    return _fwd_reference(*[args[k] for k in FWD_PARAMS])


def _output_shape():
    def fwd():
        inp = _fwd_setup_inputs(0)
        return _fwd_reference(*[inp[k] for k in FWD_PARAMS])
    out = _jax.eval_shape(fwd)
    return out.shape, out.dtype

N_MICROBATCH = 1
ADAM_LR = 0.001
ADAM_B1 = 0.9
ADAM_B2 = 0.999
ADAM_EPS = 1e-08
ADAM_WD = 0.01
ADAM_STEP = 10
PER_EXAMPLE_BATCH_AXIS = {'x': 0, 'positions': 0, 'loss_target': 0}
SHARED_INPUTS = []
_WEIGHT_DTYPES = {'ffn1_pre_g': _jnp.float32, 'ffn1_post_g': _jnp.float32, 'ffn1_w_gate': _jnp.float32, 'ffn1_w_up': _jnp.float32, 'ffn1_w_down': _jnp.float32, 'mix_pre_g': _jnp.float32, 'mix_post_g': _jnp.float32, 'w_in': _jnp.float32, 'mla_q_norm_g': _jnp.float32, 'mla_w_uq': _jnp.float32, 'mla_kv_norm_g': _jnp.float32, 'mla_w_ukv': _jnp.float32, 'w_o': _jnp.float32, 'ffn2_pre_g': _jnp.float32, 'ffn2_post_g': _jnp.float32, 'ffn2_w_gate': _jnp.float32, 'ffn2_w_up': _jnp.float32, 'ffn2_w_down': _jnp.float32}
MOMENT_SCALE = {'ffn1_pre_g': 7.819077e+00, 'ffn1_post_g': 9.766508e+00, 'ffn1_w_gate': 3.170300e+00, 'ffn1_w_up': 3.253204e+00, 'ffn1_w_down': 5.502822e+00, 'mix_pre_g': 1.696062e+01, 'mix_post_g': 3.763198e+01, 'w_in': 1.209369e+01, 'mla_q_norm_g': 1.629433e+00, 'mla_w_uq': 8.249601e-01, 'mla_kv_norm_g': 2.610733e+01, 'mla_w_ukv': 1.315232e+01, 'w_o': 1.775252e+01, 'ffn2_pre_g': 3.409027e+00, 'ffn2_post_g': 8.226404e+00, 'ffn2_w_gate': 1.216034e+00, 'ffn2_w_up': 1.659006e+00, 'ffn2_w_down': 2.756167e+00}


def _to_microbatches(a, axis):
    t = _jnp.moveaxis(a, axis, 0)
    t = t.reshape((N_MICROBATCH, t.shape[0] // N_MICROBATCH) + t.shape[1:])
    return _jnp.moveaxis(t, 1, axis + 1)


def setup_inputs(seed: int = 0) -> dict:
    inp = _fwd_setup_inputs(seed)
    key = _jax.random.fold_in(_jax.random.key(seed), 7919)
    shape, _ = _output_shape()
    out = dict(inp)
    out["loss_target"] = _jax.random.normal(_jax.random.fold_in(key, 0), shape, _jnp.float32)
    for i, name in enumerate(TWIN_WEIGHTS):
        w = inp[name].astype(_jnp.float32)
        if MOMENT_SCALE is None:
            s = _jnp.sqrt(_jnp.mean(_jnp.square(w)) + 1e-30)
        else:
            s = MOMENT_SCALE[name]
        km, kv = _jax.random.split(_jax.random.fold_in(key, i + 1))
        out[name] = w
        out["m_" + name] = s * _jax.random.normal(km, w.shape, _jnp.float32)
        out["v_" + name] = (s * s) * _jax.random.uniform(kv, w.shape, _jnp.float32, 0.5, 1.5)
    if N_MICROBATCH > 1:
        for name, axis in PER_EXAMPLE_BATCH_AXIS.items():
            out[name] = _to_microbatches(out[name], axis)
    return {'x': out['x'], 'positions': out['positions'], 'ffn1_pre_g': out['ffn1_pre_g'], 'ffn1_post_g': out['ffn1_post_g'], 'ffn1_w_gate': out['ffn1_w_gate'], 'ffn1_w_up': out['ffn1_w_up'], 'ffn1_w_down': out['ffn1_w_down'], 'mix_pre_g': out['mix_pre_g'], 'mix_post_g': out['mix_post_g'], 'w_in': out['w_in'], 'mla_q_norm_g': out['mla_q_norm_g'], 'mla_w_uq': out['mla_w_uq'], 'mla_kv_norm_g': out['mla_kv_norm_g'], 'mla_w_ukv': out['mla_w_ukv'], 'w_o': out['w_o'], 'ffn2_pre_g': out['ffn2_pre_g'], 'ffn2_post_g': out['ffn2_post_g'], 'ffn2_w_gate': out['ffn2_w_gate'], 'ffn2_w_up': out['ffn2_w_up'], 'ffn2_w_down': out['ffn2_w_down'], 'loss_target': out['loss_target'], 'm_ffn1_pre_g': out['m_ffn1_pre_g'], 'm_ffn1_post_g': out['m_ffn1_post_g'], 'm_ffn1_w_gate': out['m_ffn1_w_gate'], 'm_ffn1_w_up': out['m_ffn1_w_up'], 'm_ffn1_w_down': out['m_ffn1_w_down'], 'm_mix_pre_g': out['m_mix_pre_g'], 'm_mix_post_g': out['m_mix_post_g'], 'm_w_in': out['m_w_in'], 'm_mla_q_norm_g': out['m_mla_q_norm_g'], 'm_mla_w_uq': out['m_mla_w_uq'], 'm_mla_kv_norm_g': out['m_mla_kv_norm_g'], 'm_mla_w_ukv': out['m_mla_w_ukv'], 'm_w_o': out['m_w_o'], 'm_ffn2_pre_g': out['m_ffn2_pre_g'], 'm_ffn2_post_g': out['m_ffn2_post_g'], 'm_ffn2_w_gate': out['m_ffn2_w_gate'], 'm_ffn2_w_up': out['m_ffn2_w_up'], 'm_ffn2_w_down': out['m_ffn2_w_down'], 'v_ffn1_pre_g': out['v_ffn1_pre_g'], 'v_ffn1_post_g': out['v_ffn1_post_g'], 'v_ffn1_w_gate': out['v_ffn1_w_gate'], 'v_ffn1_w_up': out['v_ffn1_w_up'], 'v_ffn1_w_down': out['v_ffn1_w_down'], 'v_mix_pre_g': out['v_mix_pre_g'], 'v_mix_post_g': out['v_mix_post_g'], 'v_w_in': out['v_w_in'], 'v_mla_q_norm_g': out['v_mla_q_norm_g'], 'v_mla_w_uq': out['v_mla_w_uq'], 'v_mla_kv_norm_g': out['v_mla_kv_norm_g'], 'v_mla_w_ukv': out['v_mla_w_ukv'], 'v_w_o': out['v_w_o'], 'v_ffn2_pre_g': out['v_ffn2_pre_g'], 'v_ffn2_post_g': out['v_ffn2_post_g'], 'v_ffn2_w_gate': out['v_ffn2_w_gate'], 'v_ffn2_w_up': out['v_ffn2_w_up'], 'v_ffn2_w_down': out['v_ffn2_w_down']}


def _loss(weights, diff, rest, loss_target):
    with _jax.named_scope("forward"):
        args = {**rest, TWIN_DIFF_INPUT: diff, **{k: w.astype(_WEIGHT_DTYPES[k]) for k, w in weights.items()}}
        y = _forward(args)
    with _jax.named_scope("loss_head"):
        err = _jnp.square(y.astype(_jnp.float32) - loss_target)
        return 0.5 * _jnp.sum(_jnp.mean(err, axis=-1)) if err.ndim else 0.5 * err


def _adamw(w, g, m, v):
    m = ADAM_B1 * m + (1.0 - ADAM_B1) * g
    v = ADAM_B2 * v + (1.0 - ADAM_B2) * _jnp.square(g)
    m_hat = m / (1.0 - ADAM_B1 ** ADAM_STEP)
    v_hat = v / (1.0 - ADAM_B2 ** ADAM_STEP)
    delta = -ADAM_LR * (m_hat / (_jnp.sqrt(v_hat) + ADAM_EPS) + ADAM_WD * w)
    return delta, m, v


def reference(x, positions, ffn1_pre_g, ffn1_post_g, ffn1_w_gate, ffn1_w_up, ffn1_w_down, mix_pre_g, mix_post_g, w_in, mla_q_norm_g, mla_w_uq, mla_kv_norm_g, mla_w_ukv, w_o, ffn2_pre_g, ffn2_post_g, ffn2_w_gate, ffn2_w_up, ffn2_w_down, loss_target, m_ffn1_pre_g, m_ffn1_post_g, m_ffn1_w_gate, m_ffn1_w_up, m_ffn1_w_down, m_mix_pre_g, m_mix_post_g, m_w_in, m_mla_q_norm_g, m_mla_w_uq, m_mla_kv_norm_g, m_mla_w_ukv, m_w_o, m_ffn2_pre_g, m_ffn2_post_g, m_ffn2_w_gate, m_ffn2_w_up, m_ffn2_w_down, v_ffn1_pre_g, v_ffn1_post_g, v_ffn1_w_gate, v_ffn1_w_up, v_ffn1_w_down, v_mix_pre_g, v_mix_post_g, v_w_in, v_mla_q_norm_g, v_mla_w_uq, v_mla_kv_norm_g, v_mla_w_ukv, v_w_o, v_ffn2_pre_g, v_ffn2_post_g, v_ffn2_w_gate, v_ffn2_w_up, v_ffn2_w_down):
    given = dict(x=x, positions=positions, ffn1_pre_g=ffn1_pre_g, ffn1_post_g=ffn1_post_g, ffn1_w_gate=ffn1_w_gate, ffn1_w_up=ffn1_w_up, ffn1_w_down=ffn1_w_down, mix_pre_g=mix_pre_g, mix_post_g=mix_post_g, w_in=w_in, mla_q_norm_g=mla_q_norm_g, mla_w_uq=mla_w_uq, mla_kv_norm_g=mla_kv_norm_g, mla_w_ukv=mla_w_ukv, w_o=w_o, ffn2_pre_g=ffn2_pre_g, ffn2_post_g=ffn2_post_g, ffn2_w_gate=ffn2_w_gate, ffn2_w_up=ffn2_w_up, ffn2_w_down=ffn2_w_down, loss_target=loss_target, m_ffn1_pre_g=m_ffn1_pre_g, m_ffn1_post_g=m_ffn1_post_g, m_ffn1_w_gate=m_ffn1_w_gate, m_ffn1_w_up=m_ffn1_w_up, m_ffn1_w_down=m_ffn1_w_down, m_mix_pre_g=m_mix_pre_g, m_mix_post_g=m_mix_post_g, m_w_in=m_w_in, m_mla_q_norm_g=m_mla_q_norm_g, m_mla_w_uq=m_mla_w_uq, m_mla_kv_norm_g=m_mla_kv_norm_g, m_mla_w_ukv=m_mla_w_ukv, m_w_o=m_w_o, m_ffn2_pre_g=m_ffn2_pre_g, m_ffn2_post_g=m_ffn2_post_g, m_ffn2_w_gate=m_ffn2_w_gate, m_ffn2_w_up=m_ffn2_w_up, m_ffn2_w_down=m_ffn2_w_down, v_ffn1_pre_g=v_ffn1_pre_g, v_ffn1_post_g=v_ffn1_post_g, v_ffn1_w_gate=v_ffn1_w_gate, v_ffn1_w_up=v_ffn1_w_up, v_ffn1_w_down=v_ffn1_w_down, v_mix_pre_g=v_mix_pre_g, v_mix_post_g=v_mix_post_g, v_w_in=v_w_in, v_mla_q_norm_g=v_mla_q_norm_g, v_mla_w_uq=v_mla_w_uq, v_mla_kv_norm_g=v_mla_kv_norm_g, v_mla_w_ukv=v_mla_w_ukv, v_w_o=v_w_o, v_ffn2_pre_g=v_ffn2_pre_g, v_ffn2_post_g=v_ffn2_post_g, v_ffn2_w_gate=v_ffn2_w_gate, v_ffn2_w_up=v_ffn2_w_up, v_ffn2_w_down=v_ffn2_w_down)
    weights = {n: given[n] for n in TWIN_WEIGHTS}
    shared = {n: given[n] for n in SHARED_INPUTS}
    per_example = {n: given[n] for n in ['x', 'positions']}
    grad_fn = _jax.value_and_grad(_loss, argnums=(0, 1))

    def one_microbatch(ex, loss_target):
        ex = dict(ex)
        diff = ex.pop(TWIN_DIFF_INPUT)
        return grad_fn(weights, diff, {**shared, **ex}, loss_target)

    if N_MICROBATCH == 1:
        loss, (grad_w, grad_x) = one_microbatch(per_example, given["loss_target"])
    else:
        def body(carry, xs):
            loss_sum, grad_sum = carry
            l_k, (gw_k, gx_k) = one_microbatch(xs[0], xs[1])
            with _jax.named_scope("update"):
                return (loss_sum + l_k, _jax.tree.map(_jnp.add, grad_sum, gw_k)), gx_k

        init = (_jnp.zeros((), _jnp.float32), _jax.tree.map(_jnp.zeros_like, weights))
        (loss, grad_w), grad_x = _jax.lax.scan(body, init, (per_example, given["loss_target"]))
    with _jax.named_scope("update"):
        delta_w, new_m, new_v = {}, {}, {}
        for n in TWIN_WEIGHTS:
            delta_w[n], new_m[n], new_v[n] = _adamw(weights[n], grad_w[n], given["m_" + n], given["v_" + n])
    return (loss, grad_x, *[grad_w[n] for n in TWIN_WEIGHTS], *[delta_w[n] for n in TWIN_WEIGHTS],
            *[new_m[n] for n in TWIN_WEIGHTS], *[new_v[n] for n in TWIN_WEIGHTS])
```

```python
import functools
import math

import jax
import jax.numpy as jnp
from jax import lax
from jax.experimental import pallas as pl
from jax.experimental.pallas import tpu as pltpu

F32 = jnp.float32
BF16 = jnp.bfloat16
N_DEV = 8
MESH = pl.DeviceIdType.MESH

HEADS = 8
HEAD_DIM = 128
Q_RANK = 512
KV_RANK = 512
ROPE_DIM = 64
QK_PAD = 256
PART_ROPE = 32
DIL_PATTERNS = ((128, 1), (512, 4), (2048, 16))
ROPE_THETA = 500000.0
RMS_EPS = 1e-6
NEG = -1e30
LANE = 128
IN_COLS = 4160
IN_PAD = 4224
DIL_W = HEADS * HEAD_DIM

ADAM_LR, ADAM_B1, ADAM_B2, ADAM_EPS, ADAM_WD, ADAM_STEP = 0.001, 0.9, 0.999, 1e-08, 0.01, 10

VMEM_LIMIT = 56 * 1024 * 1024

WNAMES = ['ffn1_pre_g', 'ffn1_post_g', 'ffn1_w_gate', 'ffn1_w_up', 'ffn1_w_down', 'mix_pre_g', 'mix_post_g', 'w_in',
          'mla_q_norm_g', 'mla_w_uq', 'mla_kv_norm_g', 'mla_w_ukv', 'w_o', 'ffn2_pre_g', 'ffn2_post_g',
          'ffn2_w_gate', 'ffn2_w_up', 'ffn2_w_down']
BIG = ['ffn1_w_gate', 'ffn1_w_up', 'ffn1_w_down', 'w_in', 'mla_w_uq', 'mla_w_ukv', 'w_o',
       'ffn2_w_gate', 'ffn2_w_up', 'ffn2_w_down']
GAINS = [n for n in WNAMES if n not in BIG]

NT = (((1,), (1,)), ((), ()))
NN = (((1,), (0,)), ((), ()))
TN = (((0,), (0,)), ((), ()))


def _tile(n, target, mult):
    best = None
    t = mult
    while t <= min(n, target):
        if n % t == 0:
            best = t
        t += mult
    return n if best is None else best


def _params(sem=None):
    kw = dict(vmem_limit_bytes=VMEM_LIMIT)
    if sem is not None:
        kw['dimension_semantics'] = sem
    return pltpu.CompilerParams(**kw)


def _dot(a, b, dn):
    return lax.dot_general(a, b, dn, preferred_element_type=F32)


def _mm(name, pairs, pair_specs, dn, grid, k_axis, acc_shape, out_shapes, out_specs, epilogue,
        extras=(), extra_specs=()):
    n_pair, n_ex, n_out = len(pairs), len(extras), len(out_shapes)
    nk = 1 if k_axis is None else grid[k_axis]

    def body(*refs):
        ab = refs[:2 * n_pair]
        ex = refs[2 * n_pair:2 * n_pair + n_ex]
        outs = refs[2 * n_pair + n_ex:2 * n_pair + n_ex + n_out]
        part = _dot(ab[0][...], ab[1][...], dn)
        for p in range(1, n_pair):
            part = part + _dot(ab[2 * p][...], ab[2 * p + 1][...], dn)
        if nk == 1:
            epilogue(part, ex, outs)
            return
        acc = refs[-1]
        k = pl.program_id(k_axis)

        @pl.when(k == 0)
        def _():
            acc[...] = part

        @pl.when(k > 0)
        def _():
            acc[...] += part

        @pl.when(k == nk - 1)
        def _():
            epilogue(acc[...], ex, outs)

    flat, flat_specs = [], []
    for (a, b), (sa, sb) in zip(pairs, pair_specs):
        flat += [a, b]
        flat_specs += [sa, sb]
    return pl.pallas_call(
        body, name=name, grid=grid, in_specs=flat_specs + list(extra_specs), out_specs=out_specs,
        out_shape=out_shapes, scratch_shapes=[pltpu.VMEM(acc_shape, F32)] if nk > 1 else [],
        compiler_params=_params(("arbitrary",) * len(grid)),
    )(*flat, *extras)


def _store(dtype):
    def epi(acc, ex, outs):
        outs[0][...] = acc.astype(dtype)
    return epi


def _mm_nn(name, a, b, out_dtype, tm=1024, tn=1408):
    M, K = a.shape
    N = b.shape[1]
    tm, tn = _tile(M, tm, 8), _tile(N, tn, LANE)
    return _mm(name, [(a, b)],
               [(pl.BlockSpec((tm, K), lambda j, i: (i, 0)), pl.BlockSpec((K, tn), lambda j, i: (0, j)))],
               NN, (N // tn, M // tm), None, None,
               [jax.ShapeDtypeStruct((M, N), out_dtype)], [pl.BlockSpec((tm, tn), lambda j, i: (i, j))],
               _store(out_dtype))[0]


def _mm_nt(name, a, b, out_dtype, tm=512, tk=1408):
    M, K = a.shape
    N = b.shape[0]
    tm, tk = _tile(M, tm, 8), _tile(K, tk, LANE)
    return _mm(name, [(a, b)],
               [(pl.BlockSpec((tm, tk), lambda i, k: (i, k)), pl.BlockSpec((N, tk), lambda i, k: (0, k)))],
               NT, (M // tm, K // tk), 1, (tm, N),
               [jax.ShapeDtypeStruct((M, N), out_dtype)], [pl.BlockSpec((tm, N), lambda i, k: (i, 0))],
               _store(out_dtype))[0]


def _mm_tn(name, a, b, out_dtype, ts=512, tn=1408):
    M, K = a.shape
    N = b.shape[1]
    ts, tn = _tile(M, ts, 16), _tile(N, tn, LANE)
    return _mm(name, [(a, b)],
               [(pl.BlockSpec((ts, K), lambda j, m: (m, 0)), pl.BlockSpec((ts, tn), lambda j, m: (m, j)))],
               TN, (N // tn, M // ts), 1, (K, tn),
               [jax.ShapeDtypeStruct((K, N), out_dtype)], [pl.BlockSpec((K, tn), lambda j, m: (0, j))],
               _store(out_dtype))[0]


def _mm_tn_chunks_a(name, a3, b, out_dtype, ts=512):
    C, M, Kc = a3.shape
    N = b.shape[1]
    ts = _tile(M, ts, 16)
    return _mm(name, [(a3, b)],
               [(pl.BlockSpec((None, ts, Kc), lambda c, m: (c, m, 0)), pl.BlockSpec((ts, N), lambda c, m: (m, 0)))],
               TN, (C, M // ts), 1, (Kc, N),
               [jax.ShapeDtypeStruct((C, Kc, N), out_dtype)], [pl.BlockSpec((None, Kc, N), lambda c, m: (c, 0, 0))],
               _store(out_dtype))[0]


def _mm_tn_chunks_b(name, a, b3, out_dtype, ts=512):
    M, K = a.shape
    C, _, Nc = b3.shape
    ts = _tile(M, ts, 16)
    return _mm(name, [(a, b3)],
               [(pl.BlockSpec((ts, K), lambda c, m: (m, 0)), pl.BlockSpec((None, ts, Nc), lambda c, m: (c, m, 0)))],
               TN, (C, M // ts), 1, (K, Nc),
               [jax.ShapeDtypeStruct((C, K, Nc), out_dtype)], [pl.BlockSpec((None, K, Nc), lambda c, m: (c, 0, 0))],
               _store(out_dtype))[0]


def _rows(name, body, n_rows, tm, ins, outs, accs=()):
    in_specs, arrays = [], []
    for spec in ins:
        if spec[0] == 'row':
            _, arr, width, cb = spec
            in_specs.append(pl.BlockSpec((tm, width), functools.partial(lambda i, cb: (i, cb), cb=cb)))
        else:
            arr = spec[1]
            in_specs.append(pl.BlockSpec(arr.shape, functools.partial(lambda i, nd: (0,) * nd, nd=arr.ndim)))
        arrays.append(arr)
    out_shapes = [jax.ShapeDtypeStruct((n_rows, w), dt) for w, dt in outs]
    out_specs = [pl.BlockSpec((tm, w), lambda i: (i, 0)) for w, _ in outs]
    out_shapes += [jax.ShapeDtypeStruct((1, w), F32) for w in accs]
    out_specs += [pl.BlockSpec((1, w), lambda i: (0, 0)) for w in accs]
    return pl.pallas_call(body, name=name, grid=(n_rows // tm,), in_specs=in_specs, out_specs=out_specs,
                          out_shape=out_shapes, compiler_params=_params(("arbitrary",)))(*arrays)


def _acc_add(ref, val):
    @pl.when(pl.program_id(0) == 0)
    def _():
        ref[...] = val

    @pl.when(pl.program_id(0) > 0)
    def _():
        ref[...] += val


def _rms_scale(x):
    return lax.rsqrt(jnp.mean(x * x, axis=-1, keepdims=True) + RMS_EPS)


def _rms_bwd(x, g, dy):
    r = _rms_scale(x)
    t = dy * g
    dx = r * t - x * (r * r * r) * jnp.mean(t * x, axis=-1, keepdims=True)
    return dx, dy * x * r


def _rot_half(x, hw):
    lane = lax.broadcasted_iota(jnp.int32, x.shape, 1)
    left = pltpu.roll(x, LANE - hw, 1)
    right = pltpu.roll(x, hw, 1)
    return jnp.where(lane < hw, -left, right)


def _rope(x, cos, sin, hw):
    return x * cos + _rot_half(x, hw) * sin


def _rope_t(dy, cos, sin, hw):
    return dy * cos - _rot_half(dy, hw) * sin


def _rms_cast(name, x, g):
    S, D = x.shape
    tm = _tile(S, 512, 8)

    def body(x_ref, g_ref, o_ref):
        xv = x_ref[...]
        o_ref[...] = (xv * _rms_scale(xv) * g_ref[...]).astype(BF16)

    return _rows(name, body, S, tm, [('row', x, D, 0), ('full', g)], [(D, BF16)])[0]


def _postnorm_bwd(name, dxo, y, g, coef):
    S, D = y.shape
    tm = _tile(S, 512, 8)

    def body(d_ref, y_ref, g_ref, dy_ref, dg_ref):
        dx, dg = _rms_bwd(y_ref[...], g_ref[...], coef * d_ref[...])
        dy_ref[...] = dx.astype(BF16)
        _acc_add(dg_ref, jnp.sum(dg, axis=0, keepdims=True))

    return _rows(name, body, S, tm, [('row', dxo, D, 0), ('row', y, D, 0), ('full', g)], [(D, BF16)], [D])


def _prenorm_bwd(name, dh, x, g, dxo):
    S, D = x.shape
    tm = _tile(S, 512, 8)

    def body(dh_ref, x_ref, g_ref, d_ref, dx_ref, dg_ref):
        dx, dg = _rms_bwd(x_ref[...], g_ref[...], dh_ref[...])
        dx_ref[...] = d_ref[...] + dx
        _acc_add(dg_ref, jnp.sum(dg, axis=0, keepdims=True))

    return _rows(name, body, S, tm, [('row', dh, D, 0), ('row', x, D, 0), ('full', g), ('row', dxo, D, 0)],
                 [(D, F32)], [D])


def _loss_head(y, target):
    S, D = y.shape
    tm = _tile(S, 512, 8)

    def body(y_ref, t_ref, dy_ref, l_ref):
        e = y_ref[...] - t_ref[...]
        dy_ref[...] = e * (1.0 / D)
        row = 0.5 * jnp.mean(e * e, axis=-1, keepdims=True)
        _acc_add(l_ref, jnp.broadcast_to(jnp.sum(row, axis=0, keepdims=True), (1, LANE)))

    dy, l = _rows("loss_head", body, S, tm, [('row', y, D, 0), ('row', target, D, 0)], [(D, F32)], [LANE])
    return dy, l[0, 0]


def _ffn_up(h, wg, wu):
    S, D = h.shape
    C, _, Fc = wg.shape
    tm = _tile(S, 1024, 8)

    def body(h_ref, wg_ref, wu_ref, g_ref, u_ref, a_ref):
        hv = h_ref[...]
        g = _dot(hv, wg_ref[...], NN)
        u = _dot(hv, wu_ref[...], NN)
        g_ref[...] = g.astype(BF16)
        u_ref[...] = u.astype(BF16)
        a_ref[...] = (g * jax.nn.sigmoid(g) * u).astype(BF16)

    w_spec = pl.BlockSpec((None, D, Fc), lambda j, i: (j, 0, 0))
    o_spec = pl.BlockSpec((None, tm, Fc), lambda j, i: (j, i, 0))
    shp = jax.ShapeDtypeStruct((C, S, Fc), BF16)
    return pl.pallas_call(body, name="ffn_up", grid=(C, S // tm),
                          in_specs=[pl.BlockSpec((tm, D), lambda j, i: (i, 0)), w_spec, w_spec],
                          out_specs=[o_spec, o_spec, o_spec], out_shape=[shp, shp, shp],
                          compiler_params=_params(("arbitrary", "arbitrary")))(h, wg, wu)


def _chunk_post(name, a3, w3, x, g, coef):
    C, S, Kc = a3.shape
    D = w3.shape[2]
    tm = _tile(S, 512, 8)

    def epi(acc, ex, outs):
        x_ref, g_ref = ex
        outs[0][...] = x_ref[...] + coef * (acc * _rms_scale(acc) * g_ref[...])
        outs[1][...] = acc

    row = pl.BlockSpec((tm, D), lambda i, c: (i, 0))
    shp = jax.ShapeDtypeStruct((S, D), F32)
    return _mm(name, [(a3, w3)],
               [(pl.BlockSpec((None, tm, Kc), lambda i, c: (c, i, 0)), pl.BlockSpec((None, Kc, D), lambda i, c: (c, 0, 0)))],
               NN, (S // tm, C), 1, (tm, D), [shp, shp], [row, row], epi,
               extras=[x, g], extra_specs=[row, pl.BlockSpec((1, D), lambda i, c: (0, 0))])


def _ffn_da(dy, wd, gate, up):
    S, D = dy.shape
    C, Fc, _ = wd.shape
    tm = _tile(S, 1024, 8)

    def epi(acc, ex, outs):
        g = ex[0][...].astype(F32)
        u = ex[1][...].astype(F32)
        sig = jax.nn.sigmoid(g)
        outs[0][...] = (acc * u * (sig * (1.0 + g * (1.0 - sig)))).astype(BF16)
        outs[1][...] = (acc * (g * sig)).astype(BF16)

    blk = pl.BlockSpec((None, tm, Fc), lambda c, i: (c, i, 0))
    shp = jax.ShapeDtypeStruct((C, S, Fc), BF16)
    return _mm("ffn_da", [(dy, wd)],
               [(pl.BlockSpec((tm, D), lambda c, i: (i, 0)), pl.BlockSpec((None, Fc, D), lambda c, i: (c, 0, 0)))],
               NT, (C, S // tm), None, None, [shp, shp], [blk, blk], epi, extras=[gate, up], extra_specs=[blk, blk])


def _ffn_dh(dg, du, wg, wu):
    C, S, Fc = dg.shape
    D = wg.shape[1]
    tm = _tile(S, 512, 8)
    a_spec = pl.BlockSpec((None, tm, Fc), lambda i, c: (c, i, 0))
    w_spec = pl.BlockSpec((None, D, Fc), lambda i, c: (c, 0, 0))
    return _mm("ffn_dh", [(dg, wg), (du, wu)], [(a_spec, w_spec), (a_spec, w_spec)], NT, (S // tm, C), 1, (tm, D),
               [jax.ShapeDtypeStruct((S, D), F32)], [pl.BlockSpec((tm, D), lambda i, c: (i, 0))], _store(F32))[0]


def _ffn_fwd(x, p):
    h = _rms_cast("ffn_prenorm", x, p['pre_g'])
    gate, up, act = _ffn_up(h, p['w_gate'], p['w_up'])
    x_out, y = _chunk_post("ffn_down", act, p['w_down'], x, p['post_g'], 0.5)
    return x_out, (x, h, gate, up, act, y)


def _ffn_bwd(dxo, p, saved):
    x, h, gate, up, act, y = saved
    dy, d_post = _postnorm_bwd("ffn_postnorm_bwd", dxo, y, p['post_g'], 0.5)
    dgate, dup = _ffn_da(dy, p['w_down'], gate, up)
    d_wd = _mm_tn_chunks_a("ffn_dwd", act, dy, BF16)
    dh = _ffn_dh(dgate, dup, p['w_gate'], p['w_up'])
    d_wg = _mm_tn_chunks_b("ffn_dwg", h, dgate, BF16)
    d_wu = _mm_tn_chunks_b("ffn_dwu", h, dup, BF16)
    dx, d_pre = _prenorm_bwd("ffn_prenorm_bwd", dh, x, p['pre_g'], dxo)
    return dx, dict(pre_g=d_pre, post_g=d_post, w_gate=d_wg, w_up=d_wu, w_down=d_wd)


def _causal_mask(t):
    r = lax.broadcasted_iota(jnp.int32, (t, t), 0)
    c = lax.broadcasted_iota(jnp.int32, (t, t), 1)
    return r >= c


def _mla_fwd(q, k, kv):
    S = q.shape[0]
    t = _tile(S, 512, LANE)
    nq = S // t
    scale = 1.0 / math.sqrt(HEAD_DIM + ROPE_DIM)

    def body(q_ref, k_ref, v_ref, o_ref, lse_ref):
        i = pl.program_id(1)
        qb = q_ref[...]

        def step(j, carry, masked):
            m, l, acc = carry
            rows = pl.ds(pl.multiple_of(j * t, t), t)
            s = _dot(qb, k_ref[rows, :], NT) * scale
            if masked:
                s = jnp.where(_causal_mask(t), s, NEG)
            m_new = jnp.maximum(m, jnp.max(s, axis=-1, keepdims=True))
            alpha = jnp.exp(m - m_new)
            pr = jnp.exp(s - m_new)
            l = alpha * l + jnp.sum(pr, axis=-1, keepdims=True)
            acc = alpha * acc + _dot(pr.astype(BF16), v_ref[rows, :], NN)
            return m_new, l, acc

        init = (jnp.full((t, 1), NEG, F32), jnp.zeros((t, 1), F32), jnp.zeros((t, HEAD_DIM), F32))
        carry = lax.fori_loop(0, i, lambda j, c: step(j, c, False), init)
        m, l, acc = step(i, carry, True)
        o_ref[...] = (acc / l).astype(BF16)
        lse_ref[...] = jnp.broadcast_to(m + jnp.log(l), (t, LANE))

    return pl.pallas_call(
        body, name="mla_fwd", grid=(HEADS, nq),
        in_specs=[pl.BlockSpec((t, QK_PAD), lambda h, i: (i, h)), pl.BlockSpec((S, QK_PAD), lambda h, i: (0, h)),
                  pl.BlockSpec((S, HEAD_DIM), lambda h, i: (0, HEADS + h))],
        out_specs=[pl.BlockSpec((t, HEAD_DIM), lambda h, i: (i, h)), pl.BlockSpec((None, t, LANE), lambda h, i: (h, i, 0))],
        out_shape=[jax.ShapeDtypeStruct((S, HEADS * HEAD_DIM), BF16), jax.ShapeDtypeStruct((HEADS, S, LANE), F32)],
        compiler_params=_params(("arbitrary", "arbitrary")))(q, k, kv)


def _mla_stats(do, o, lse):
    S = o.shape[0]
    t = _tile(S, 512, 8)

    def body(do_ref, o_ref, lse_ref, st_ref):
        delta = jnp.sum(do_ref[...].astype(F32) * o_ref[...].astype(F32), axis=-1, keepdims=True)
        lane = lax.broadcasted_iota(jnp.int32, (t, LANE), 1)
        st_ref[...] = jnp.where(lane < LANE // 2, lse_ref[...], jnp.broadcast_to(delta, (t, LANE)))

    blk = pl.BlockSpec((t, HEAD_DIM), lambda h, i: (i, h))
    st = pl.BlockSpec((None, t, LANE), lambda h, i: (h, i, 0))
    return pl.pallas_call(body, name="mla_stats", grid=(HEADS, S // t), in_specs=[blk, blk, st], out_specs=st,
                          out_shape=jax.ShapeDtypeStruct((HEADS, S, LANE), F32),
                          compiler_params=_params(("arbitrary", "arbitrary")))(do, o, lse)


def _mla_bwd(q, k, kv, do, stats):
    S = q.shape[0]
    t = _tile(S, 512, LANE)
    nq = S // t
    scale = 1.0 / math.sqrt(HEAD_DIM + ROPE_DIM)

    def body(q_ref, do_ref, st_ref, k_ref, v_ref, dq_ref, dk_ref, dv_ref):
        j = pl.program_id(1)

        @pl.when(j == 0)
        def _():
            dq_ref[...] = jnp.zeros_like(dq_ref)

        kb = k_ref[...]
        vb = v_ref[...]

        def step(i, carry, masked):
            dk, dv = carry
            rows = pl.ds(pl.multiple_of(i * t, t), t)
            qb = q_ref[rows, :]
            dob = do_ref[rows, :]
            pr = jnp.exp(_dot(qb, kb, NT) * scale - st_ref[rows, 0:1])
            if masked:
                pr = jnp.where(_causal_mask(t), pr, 0.0)
            dv = dv + _dot(pr.astype(BF16), dob, TN)
            dp = _dot(dob, vb, NT)
            ds = (pr * (dp - st_ref[rows, LANE // 2:LANE // 2 + 1]) * scale).astype(BF16)
            dk = dk + _dot(ds, qb, TN)
            dq_ref[rows, :] += _dot(ds, kb, NN)
            return dk, dv

        carry = step(j, (jnp.zeros((t, QK_PAD), F32), jnp.zeros((t, HEAD_DIM), F32)), True)
        dk, dv = lax.fori_loop(j + 1, nq, lambda i, c: step(i, c, False), carry)
        dk_ref[...] = dk
        dv_ref[...] = dv.astype(BF16)

    return pl.pallas_call(
        body, name="mla_bwd", grid=(HEADS, nq),
        in_specs=[pl.BlockSpec((S, QK_PAD), lambda h, j: (0, h)), pl.BlockSpec((S, HEAD_DIM), lambda h, j: (0, h)),
                  pl.BlockSpec((None, S, LANE), lambda h, j: (h, 0, 0)),
                  pl.BlockSpec((t, QK_PAD), lambda h, j: (j, h)), pl.BlockSpec((t, HEAD_DIM), lambda h, j: (j, HEADS + h))],
        out_specs=[pl.BlockSpec((S, QK_PAD), lambda h, j: (0, h)), pl.BlockSpec((t, QK_PAD), lambda h, j: (j, h)),
                   pl.BlockSpec((t, HEAD_DIM), lambda h, j: (j, h))],
        out_shape=[jax.ShapeDtypeStruct((S, HEADS * QK_PAD), F32), jax.ShapeDtypeStruct((S, HEADS * QK_PAD), F32),
                   jax.ShapeDtypeStruct((S, HEADS * HEAD_DIM), BF16)],
        compiler_params=_params(("arbitrary", "arbitrary")))(q, do, stats, k, kv)


DIL_BLK = 128
DIL_HB = 2


def _dil_tiles(S, d):
    L = S // d
    tq = _tile(L, 4 * DIL_BLK, DIL_BLK)
    return L, tq, tq // DIL_BLK


def _dil_masks(absent):
    qi = lax.broadcasted_iota(jnp.int32, (DIL_BLK, DIL_BLK), 0)
    kj = lax.broadcasted_iota(jnp.int32, (DIL_BLK, DIL_BLK), 1)
    edge = kj >= qi + jnp.where(absent, DIL_BLK, 0)
    return edge, kj >= qi, kj <= qi


def _dil_fwd(q, k, v, d):
    S = q.shape[0]
    L, tq, nb = _dil_tiles(S, d)
    W = DIL_HB * HEAD_DIM
    scale = 1.0 / math.sqrt(HEAD_DIM)
    view = lambda a: a.reshape(L, d * DIL_W)

    def body(q_ref, k_ref, v_ref, kp_ref, vp_ref, o_ref, lse_ref):
        m_edge, m_prev, m_cur = _dil_masks(pl.program_id(0) == 0)
        for hb in range(DIL_HB):
            cols = slice(hb * HEAD_DIM, (hb + 1) * HEAD_DIM)
            for b in range(nb):
                rows = slice(b * DIL_BLK, (b + 1) * DIL_BLK)
                qb = q_ref[rows, cols]
                if b == 0:
                    kp, vp = kp_ref[:, cols], vp_ref[:, cols]
                    ok_prev = m_edge
                else:
                    prev = slice((b - 1) * DIL_BLK, b * DIL_BLK)
                    kp, vp = k_ref[prev, cols], v_ref[prev, cols]
                    ok_prev = m_prev
                s_p = jnp.where(ok_prev, _dot(qb, kp, NT) * scale, NEG)
                s_c = jnp.where(m_cur, _dot(qb, k_ref[rows, cols], NT) * scale, NEG)
                m = jnp.maximum(jnp.max(s_p, axis=-1, keepdims=True), jnp.max(s_c, axis=-1, keepdims=True))
                e_p, e_c = jnp.exp(s_p - m), jnp.exp(s_c - m)
                l = jnp.sum(e_p, axis=-1, keepdims=True) + jnp.sum(e_c, axis=-1, keepdims=True)
                lse = m + jnp.log(l)
                p_p, p_c = jnp.exp(s_p - lse).astype(BF16), jnp.exp(s_c - lse).astype(BF16)
                o_ref[rows, cols] = _dot(p_p, vp, NN) + _dot(p_c, v_ref[rows, cols], NN)
                lse_ref[rows, cols] = jnp.broadcast_to(lse, (DIL_BLK, HEAD_DIM))

    cur = pl.BlockSpec((tq, W), lambda n, cb: (n, cb))
    prv = pl.BlockSpec((DIL_BLK, W), lambda n, cb: (jnp.maximum(n * nb - 1, 0), cb))
    shp = jax.ShapeDtypeStruct((L, d * DIL_W), F32)
    o, lse = pl.pallas_call(body, name=f"dil_fwd_{d}", grid=(L // tq, d * HEADS // DIL_HB),
                            in_specs=[cur, cur, cur, prv, prv], out_specs=[cur, cur], out_shape=[shp, shp],
                            compiler_params=_params(("arbitrary", "arbitrary")))(view(q), view(k), view(v), view(k), view(v))
    return o.reshape(S, DIL_W), lse.reshape(S, DIL_W)


def _dil_bwd(q, k, v, do, lse, dd, d):
    S = q.shape[0]
    L, tq, nb = _dil_tiles(S, d)
    nt = L // tq
    W = DIL_HB * HEAD_DIM
    scale = 1.0 / math.sqrt(HEAD_DIM)
    view = lambda a: a.reshape(L, d * DIL_W)

    def body(q_ref, k_ref, v_ref, do_ref, lse_ref, dd_ref, kp_ref, vp_ref, qn_ref, don_ref, lsen_ref, ddn_ref,
             dq_ref, dk_ref, dv_ref):
        n = pl.program_id(0)
        no_prev, m_prev, m_cur = _dil_masks(n == 0)
        has_next = _dil_masks(n == nt - 1)[0]

        def pair(qb, kb, vb, dob, lse_b, dd_b, mask):
            pr = jnp.where(mask, jnp.exp(_dot(qb, kb, NT) * scale - lse_b), 0.0)
            ds = (pr * (_dot(dob, vb, NT) - dd_b) * scale).astype(BF16)
            return _dot(ds, kb, NN), _dot(ds, qb, TN), _dot(pr.astype(BF16), dob, TN)

        for hb in range(DIL_HB):
            cols = slice(hb * HEAD_DIM, (hb + 1) * HEAD_DIM)
            stat = slice(hb * HEAD_DIM, hb * HEAD_DIM + 1)
            dks, dvs = [], []
            for b in range(nb):
                rows = slice(b * DIL_BLK, (b + 1) * DIL_BLK)
                qb, dob = q_ref[rows, cols], do_ref[rows, cols]
                lse_b, dd_b = lse_ref[rows, stat], dd_ref[rows, stat]
                dq, dk, dv = pair(qb, k_ref[rows, cols], v_ref[rows, cols], dob, lse_b, dd_b, m_cur)
                if b == 0:
                    dq_p, _, _ = pair(qb, kp_ref[:, cols], vp_ref[:, cols], dob, lse_b, dd_b, no_prev)
                else:
                    prev = slice((b - 1) * DIL_BLK, b * DIL_BLK)
                    dq_p, dk_p, dv_p = pair(qb, k_ref[prev, cols], v_ref[prev, cols], dob, lse_b, dd_b, m_prev)
                    dks[b - 1] = dks[b - 1] + dk_p
                    dvs[b - 1] = dvs[b - 1] + dv_p
                dq_ref[rows, cols] = dq + dq_p
                dks.append(dk)
                dvs.append(dv)
            last = slice((nb - 1) * DIL_BLK, nb * DIL_BLK)
            _, dk_n, dv_n = pair(qn_ref[:, cols], k_ref[last, cols], v_ref[last, cols], don_ref[:, cols],
                                 lsen_ref[:, stat], ddn_ref[:, stat], has_next)
            dks[nb - 1] = dks[nb - 1] + dk_n
            dvs[nb - 1] = dvs[nb - 1] + dv_n
            for b in range(nb):
                rows = slice(b * DIL_BLK, (b + 1) * DIL_BLK)
                dk_ref[rows, cols] = dks[b]
                dv_ref[rows, cols] = dvs[b]

    cur = pl.BlockSpec((tq, W), lambda n, cb: (n, cb))
    prv = pl.BlockSpec((DIL_BLK, W), lambda n, cb: (jnp.maximum(n * nb - 1, 0), cb))
    nxt = pl.BlockSpec((DIL_BLK, W), lambda n, cb: (jnp.minimum((n + 1) * nb, L // DIL_BLK - 1), cb))
    shp = jax.ShapeDtypeStruct((L, d * DIL_W), F32)
    qv, kv_, vv, dov, lv, ddv = (view(a) for a in (q, k, v, do, lse, dd))
    outs = pl.pallas_call(body, name=f"dil_bwd_{d}", grid=(nt, d * HEADS // DIL_HB),
                          in_specs=[cur] * 6 + [prv, prv] + [nxt] * 4, out_specs=[cur] * 3, out_shape=[shp] * 3,
                          compiler_params=_params(("arbitrary", "arbitrary")))(
                              qv, kv_, vv, dov, lv, ddv, kv_, vv, qv, dov, lv, ddv)
    return [a.reshape(S, DIL_W) for a in outs]


def _dil_merge(os_, lses):
    S = os_[0].shape[0]
    tm = _tile(S, 256, 8)

    def body(o0, o1, o2, l0, l1, l2, out_ref):
        ls = [l0[...], l1[...], l2[...]]
        m = jnp.maximum(jnp.maximum(ls[0], ls[1]), ls[2])
        es = [jnp.exp(l - m) for l in ls]
        den = es[0] + es[1] + es[2]
        out_ref[...] = ((es[0] * o0[...] + es[1] * o1[...] + es[2] * o2[...]) / den).astype(BF16)

    ins = [('row', a, DIL_W, 0) for a in list(os_) + list(lses)]
    return _rows("dil_merge", body, S, tm, ins, [(DIL_W, BF16)])[0]


def _dil_merge_bwd(dout, os_, lses):
    S = dout.shape[0]
    tm = _tile(S, 256, 8)

    def body(d_ref, o0, o1, o2, l0, l1, l2, do0, do1, do2, dd0, dd1, dd2):
        dv = d_ref[...]
        os3 = [o0[...], o1[...], o2[...]]
        ls = [l0[...], l1[...], l2[...]]
        m = jnp.maximum(jnp.maximum(ls[0], ls[1]), ls[2])
        es = [jnp.exp(l - m) for l in ls]
        den = es[0] + es[1] + es[2]
        ws = [e / den for e in es]
        for h in range(HEADS):
            cols = slice(h * HEAD_DIM, (h + 1) * HEAD_DIM)
            dw = [jnp.sum(dv[:, cols] * o[:, cols], axis=-1, keepdims=True) for o in os3]
            wh = [w[:, cols] for w in ws]
            mean = wh[0] * dw[0] + wh[1] * dw[1] + wh[2] * dw[2]
            for p_, (do_ref, dd_ref) in enumerate(((do0, dd0), (do1, dd1), (do2, dd2))):
                do_p = wh[p_] * dv[:, cols]
                dlse = wh[p_] * (dw[p_] - mean)
                do_ref[:, cols] = do_p.astype(BF16)
                dd_ref[:, cols] = wh[p_] * dw[p_] - dlse

    ins = [('row', a, DIL_W, 0) for a in [dout] + list(os_) + list(lses)]
    outs = _rows("dil_merge_bwd", body, S, tm, ins, [(DIL_W, BF16)] * 3 + [(DIL_W, F32)] * 3)
    return outs[:3], outs[3:]


def _qkv_prep(proj, gq, gkv, tabs):
    S = proj.shape[0]
    tm = _tile(S, 256, 8)
    cos_a, sin_a, cos_p, sin_p = tabs

    def body(cq, ckv, qd, kd, vd, kr, gq_ref, gkv_ref, ca, sa, cp, sp, o_cq, o_ckv, o_qd, o_kd, o_vd, o_kr):
        x = cq[...]
        o_cq[...] = (x * _rms_scale(x) * gq_ref[...]).astype(BF16)
        x = ckv[...]
        o_ckv[...] = (x * _rms_scale(x) * gkv_ref[...]).astype(BF16)
        c, s = cp[...], sp[...]
        for h in range(HEADS):
            cols = slice(h * HEAD_DIM, (h + 1) * HEAD_DIM)
            o_qd[:, cols] = _rope(qd[:, cols], c, s, PART_ROPE // 2).astype(BF16)
            o_kd[:, cols] = _rope(kd[:, cols], c, s, PART_ROPE // 2).astype(BF16)
        o_vd[...] = vd[...].astype(BF16)
        o_kr[...] = _rope(kr[...], ca[...], sa[...], ROPE_DIM // 2).astype(BF16)

    ins = [('row', proj, Q_RANK, 0), ('row', proj, KV_RANK, 1), ('row', proj, DIL_W, 1), ('row', proj, DIL_W, 2),
           ('row', proj, DIL_W, 3), ('row', proj, LANE, 4 * DIL_W // LANE), ('full', gq), ('full', gkv),
           ('row', cos_a, LANE, 0), ('row', sin_a, LANE, 0), ('row', cos_p, LANE, 0), ('row', sin_p, LANE, 0)]
    return _rows("qkv_prep", body, S, tm, ins,
                 [(Q_RANK, BF16), (KV_RANK, BF16), (DIL_W, BF16), (DIL_W, BF16), (DIL_W, BF16), (LANE, BF16)])


def _qk_finish(qa, kv, kr, tabs):
    S = qa.shape[0]
    tm = _tile(S, 256, 8)
    cos_a, sin_a = tabs[0], tabs[1]

    def body(qa_ref, kn_ref, kr_ref, ca, sa, q_ref, k_ref):
        c, s = ca[...], sa[...]
        krv = kr_ref[...]
        for h in range(HEADS):
            nope = slice(h * QK_PAD, h * QK_PAD + HEAD_DIM)
            rope = slice(h * QK_PAD + HEAD_DIM, (h + 1) * QK_PAD)
            q_ref[:, nope] = qa_ref[:, nope].astype(BF16)
            q_ref[:, rope] = _rope(qa_ref[:, rope], c, s, ROPE_DIM // 2).astype(BF16)
            k_ref[:, nope] = kn_ref[:, h * HEAD_DIM:(h + 1) * HEAD_DIM]
            k_ref[:, rope] = krv

    W = HEADS * QK_PAD
    ins = [('row', qa, W, 0), ('row', kv, DIL_W, 0), ('row', kr, LANE, 0), ('row', cos_a, LANE, 0), ('row', sin_a, LANE, 0)]
    return _rows("qk_finish", body, S, tm, ins, [(W, BF16), (W, BF16)])


def _qk_finish_bwd(dq, dk, dv, tabs):
    S = dq.shape[0]
    tm = _tile(S, 256, 8)
    cos_a, sin_a = tabs[0], tabs[1]

    def body(dq_ref, dk_ref, dv_ref, ca, sa, dqa_ref, dkv_ref, dkr_ref):
        c, s = ca[...], sa[...]
        krsum = jnp.zeros((tm, LANE), F32)
        for h in range(HEADS):
            nope = slice(h * QK_PAD, h * QK_PAD + HEAD_DIM)
            rope = slice(h * QK_PAD + HEAD_DIM, (h + 1) * QK_PAD)
            dqa_ref[:, nope] = dq_ref[:, nope].astype(BF16)
            dqa_ref[:, rope] = _rope_t(dq_ref[:, rope], c, s, ROPE_DIM // 2).astype(BF16)
            dkv_ref[:, h * HEAD_DIM:(h + 1) * HEAD_DIM] = dk_ref[:, nope].astype(BF16)
            krsum = krsum + dk_ref[:, rope]
        dkv_ref[:, DIL_W:] = dv_ref[...]
        dkr_ref[...] = _rope_t(krsum, c, s, ROPE_DIM // 2)

    W = HEADS * QK_PAD
    ins = [('row', dq, W, 0), ('row', dk, W, 0), ('row', dv, DIL_W, 0), ('row', cos_a, LANE, 0), ('row', sin_a, LANE, 0)]
    return _rows("qk_finish_bwd", body, S, tm, ins, [(W, BF16), (2 * DIL_W, BF16), (LANE, F32)])


def _qkv_prep_bwd(proj, gq, gkv, d_cqn, d_ckvn, dqs, dks, dvs, dkr, tabs):
    S = proj.shape[0]
    tm = _tile(S, 256, 8)
    cos_p, sin_p = tabs[2], tabs[3]

    def body(cq, ckv, gq_ref, gkv_ref, dcq, dckv, dq0, dq1, dq2, dk0, dk1, dk2, dv0, dv1, dv2, dkr_ref, cp, sp,
             out_ref, dgq_ref, dgkv_ref):
        dx, dg = _rms_bwd(cq[...], gq_ref[...], dcq[...])
        out_ref[:, 0:Q_RANK] = dx.astype(BF16)
        _acc_add(dgq_ref, jnp.sum(dg, axis=0, keepdims=True))
        dx, dg = _rms_bwd(ckv[...], gkv_ref[...], dckv[...])
        out_ref[:, Q_RANK:Q_RANK + KV_RANK] = dx.astype(BF16)
        _acc_add(dgkv_ref, jnp.sum(dg, axis=0, keepdims=True))
        c, s = cp[...], sp[...]
        base = Q_RANK + KV_RANK
        for h in range(HEADS):
            cols = slice(h * HEAD_DIM, (h + 1) * HEAD_DIM)
            dst = lambda part: slice(base + part * DIL_W + h * HEAD_DIM, base + part * DIL_W + (h + 1) * HEAD_DIM)
            out_ref[:, dst(0)] = _rope_t(dq0[:, cols] + dq1[:, cols] + dq2[:, cols], c, s, PART_ROPE // 2).astype(BF16)
            out_ref[:, dst(1)] = _rope_t(dk0[:, cols] + dk1[:, cols] + dk2[:, cols], c, s, PART_ROPE // 2).astype(BF16)
            out_ref[:, dst(2)] = (dv0[:, cols] + dv1[:, cols] + dv2[:, cols]).astype(BF16)
        out_ref[:, base + 3 * DIL_W:] = dkr_ref[...].astype(BF16)

    ins = [('row', proj, Q_RANK, 0), ('row', proj, KV_RANK, 1), ('full', gq), ('full', gkv),
           ('row', d_cqn, Q_RANK, 0), ('row', d_ckvn, KV_RANK, 0)]
    ins += [('row', a, DIL_W, 0) for a in list(dqs) + list(dks) + list(dvs)]
    ins += [('row', dkr, LANE, 0), ('row', cos_p, LANE, 0), ('row', sin_p, LANE, 0)]
    return _rows("qkv_prep_bwd", body, S, tm, ins, [(IN_PAD, BF16)], [Q_RANK, KV_RANK])


def _mix_fwd(x, p, tabs):
    S = x.shape[0]
    h = _rms_cast("mix_prenorm", x, p['pre_g'])
    proj = _mm_nn("mix_proj", h, p['w_in'], F32)
    cqn, ckvn, qd, kd, vd, kr = _qkv_prep(proj, p['q_norm_g'], p['kv_norm_g'], tabs)
    qa = _mm_nn("mla_q_up", cqn, p['w_uq'], F32, tn=1024)
    kv = _mm_nn("mla_kv_up", ckvn, p['w_ukv'], BF16, tn=1024)
    q_cat, k_cat = _qk_finish(qa, kv, kr, tabs)
    o_a, lse_a = _mla_fwd(q_cat, k_cat, kv)
    o_ps, lse_ps = [], []
    for _, d in DIL_PATTERNS:
        o_p, lse_p = _dil_fwd(qd, kd, vd, d)
        o_ps.append(o_p)
        lse_ps.append(lse_p)
    o_b = _dil_merge(o_ps, lse_ps)
    o_cat = jnp.stack([o_a, o_b])
    x_out, y = _chunk_post("mix_out", o_cat, p['w_o'].reshape(2, DIL_W, -1), x, p['post_g'], 1.0)
    return x_out, (x, h, proj, cqn, ckvn, qd, kd, vd, q_cat, k_cat, kv, o_a, lse_a, o_ps, lse_ps, o_cat, y)


def _mix_bwd(dxo, p, tabs, saved):
    x, h, proj, cqn, ckvn, qd, kd, vd, q_cat, k_cat, kv, o_a, lse_a, o_ps, lse_ps, o_cat, y = saved
    dy, d_post = _postnorm_bwd("mix_postnorm_bwd", dxo, y, p['post_g'], 1.0)
    w_o = p['w_o']
    d_oa = _mm_nt("mix_do_a", dy, w_o[:DIL_W], BF16, tk=1024)
    d_ob = _mm_nt("mix_do_b", dy, w_o[DIL_W:], F32, tk=1024)
    d_wo = _mm_tn_chunks_a("mix_dwo", o_cat, dy, BF16).reshape(w_o.shape)
    stats = _mla_stats(d_oa, o_a, lse_a)
    dq, dk, dv = _mla_bwd(q_cat, k_cat, kv, d_oa, stats)
    dqa, dkv, dkr = _qk_finish_bwd(dq, dk, dv, tabs)
    d_cqn = _mm_nt("mla_dcq", dqa, p['w_uq'], F32, tk=1024)
    d_ckvn = _mm_nt("mla_dckv", dkv, p['w_ukv'], F32, tk=1024)
    d_wuq = _mm_tn("mla_dwuq", cqn, dqa, BF16, tn=1024)
    d_wukv = _mm_tn("mla_dwukv", ckvn, dkv, BF16, tn=1024)
    do_ps, dd_ps = _dil_merge_bwd(d_ob, o_ps, lse_ps)
    dqs, dks, dvs = [], [], []
    for (_, d), do_p, lse_p, dd_p in zip(DIL_PATTERNS, do_ps, lse_ps, dd_ps):
        dq_p, dk_p, dv_p = _dil_bwd(qd, kd, vd, do_p, lse_p, dd_p, d)
        dqs.append(dq_p)
        dks.append(dk_p)
        dvs.append(dv_p)
    d_proj, d_gq, d_gkv = _qkv_prep_bwd(proj, p['q_norm_g'], p['kv_norm_g'], d_cqn, d_ckvn, dqs, dks, dvs, dkr, tabs)
    dh = _mm_nt("mix_dh", d_proj, p['w_in'], F32)
    d_win = _mm_tn("mix_dwin", h, d_proj, BF16)
    dx, d_pre = _prenorm_bwd("mix_prenorm_bwd", dh, x, p['pre_g'], dxo)
    return dx, dict(pre_g=d_pre, post_g=d_post, w_in=d_win, q_norm_g=d_gq, w_uq=d_wuq, kv_norm_g=d_gkv,
                    w_ukv=d_wukv, w_o=d_wo)


def _cols_from_shards(g):
    return jnp.transpose(g, (1, 0, 2)).reshape(g.shape[1], -1)


def _shards_from_cols(w):
    K = w.shape[0]
    return jnp.transpose(w.reshape(K, N_DEV, -1), (1, 0, 2))


def _win_layout(g):
    w = _cols_from_shards(g)
    a = Q_RANK + KV_RANK
    return jnp.concatenate([w[:, :a], w[:, a + ROPE_DIM:], w[:, a:a + ROPE_DIM],
                            jnp.zeros((w.shape[0], IN_PAD - IN_COLS), w.dtype)], axis=1)


def _win_unlayout(dw):
    a = Q_RANK + KV_RANK
    w = jnp.concatenate([dw[:, :a], dw[:, a + 3 * DIL_W:a + 3 * DIL_W + ROPE_DIM], dw[:, a:a + 3 * DIL_W]], axis=1)
    return _shards_from_cols(w)


def _wuq_layout(g):
    return _cols_from_shards(jnp.pad(g, ((0, 0), (0, 0), (0, QK_PAD - HEAD_DIM - ROPE_DIM))))


def _wuq_unlayout(dw):
    return _shards_from_cols(dw)[:, :, :HEAD_DIM + ROPE_DIM]


def _wukv_layout(g):
    return jnp.concatenate([_cols_from_shards(g[:, :, :HEAD_DIM]), _cols_from_shards(g[:, :, HEAD_DIM:])], axis=1)


def _wukv_unlayout(dw):
    return jnp.concatenate([_shards_from_cols(dw[:, :DIL_W]), _shards_from_cols(dw[:, DIL_W:])], axis=2)


def _cast_bf16(x):
    shp = x.shape
    x2 = x.reshape(-1, shp[-1])
    R, C = x2.shape
    tr = _tile(R, 512, 8)

    def body(x_ref, o_ref):
        o_ref[...] = x_ref[...].astype(BF16)

    spec = pl.BlockSpec((tr, C), lambda i: (i, 0))
    out = pl.pallas_call(body, name="cast_bf16", grid=(R // tr,), in_specs=[spec], out_specs=spec,
                         out_shape=jax.ShapeDtypeStruct((R, C), BF16), compiler_params=_params(("arbitrary",)))(x2)
    return out.reshape(shp)


def _mesh_pos():
    x, y, c = lax.axis_index("x"), lax.axis_index("y"), lax.axis_index("c")
    return x, y, c


def _all_gather(xs):
    n = len(xs)

    def body(*refs):
        x_refs, o_refs = refs[:n], refs[n:2 * n]
        send, recv, loc = refs[2 * n:]
        x, y, c = _mesh_pos()
        me, sib = (x, y, c), (x, y, 1 - c)
        chips = [(1 - x, y), (x, 1 - y), (1 - x, 1 - y)]

        def slot(k, dev):
            return o_refs[k].at[4 * dev[0] + 2 * dev[1] + dev[2]]

        def copy(k, s, block, to, src=None):
            return pltpu.make_async_remote_copy(src_ref=slot(k, block) if src is None else src, dst_ref=slot(k, block),
                                                send_sem=send.at[k, s], recv_sem=recv.at[k, s],
                                                device_id=to, device_id_type=MESH)

        mine = [pltpu.make_async_copy(x_refs[k], slot(k, me), loc.at[k]) for k in range(n)]
        for cp in mine:
            cp.start()
        first = []
        for k in range(n):
            first.append(copy(k, 0, me, sib, src=x_refs[k]))
            first += [copy(k, 1 + j, me, (*chip, c), src=x_refs[k]) for j, chip in enumerate(chips)]
        for cp in first:
            cp.start()
        passed = []
        for k in range(n):
            for j, chip in enumerate(chips):
                copy(k, 1 + j, (*chip, c), me).wait_recv()
                fwd = copy(k, 4 + j, (*chip, c), sib)
                fwd.start()
                passed.append(fwd)
        for k in range(n):
            copy(k, 0, sib, me).wait_recv()
            for j, chip in enumerate(chips):
                copy(k, 4 + j, (*chip, 1 - c), me).wait_recv()
        for cp in first + passed:
            cp.wait_send()
        for cp in mine:
            cp.wait()

    any_spec = pl.BlockSpec(memory_space=pl.ANY)
    return pl.pallas_call(
        body, name="all_gather", in_specs=[any_spec] * n, out_specs=[any_spec] * n,
        out_shape=[jax.ShapeDtypeStruct((N_DEV,) + a.shape, a.dtype) for a in xs],
        scratch_shapes=[pltpu.SemaphoreType.DMA((n, 7)), pltpu.SemaphoreType.DMA((n, 7)), pltpu.SemaphoreType.DMA((n,))],
    )(*xs)


def _exchange_slots(gs):
    n = len(gs)

    def body(*refs):
        g_refs, r_refs = refs[:n], refs[n:2 * n]
        send, recv, loc = refs[2 * n:]
        x, y, c = _mesh_pos()
        me = 4 * x + 2 * y + c
        mine = [pltpu.make_async_copy(g_refs[k].at[me], r_refs[k].at[me], loc.at[k]) for k in range(n)]
        for cp in mine:
            cp.start()
        copies = []
        for k in range(n):
            for m in range(1, N_DEV):
                px = 1 - x if m & 4 else x
                py = 1 - y if m & 2 else y
                pc = 1 - c if m & 1 else c
                peer = 4 * px + 2 * py + pc
                copies.append((pltpu.make_async_remote_copy(
                    src_ref=g_refs[k].at[peer], dst_ref=r_refs[k].at[me], send_sem=send.at[k, m - 1],
                    recv_sem=recv.at[k, m - 1], device_id=(px, py, pc), device_id_type=MESH),
                    pltpu.make_async_remote_copy(
                    src_ref=g_refs[k].at[peer], dst_ref=r_refs[k].at[peer], send_sem=send.at[k, m - 1],
                    recv_sem=recv.at[k, m - 1], device_id=(px, py, pc), device_id_type=MESH)))
        for out_cp, _ in copies:
            out_cp.start()
        for _, in_cp in copies:
            in_cp.wait_recv()
        for out_cp, _ in copies:
            out_cp.wait_send()
        for cp in mine:
            cp.wait()

    any_spec = pl.BlockSpec(memory_space=pl.ANY)
    return pl.pallas_call(
        body, name="grad_exchange", in_specs=[any_spec] * n, out_specs=[any_spec] * n,
        out_shape=[jax.ShapeDtypeStruct(a.shape, a.dtype) for a in gs],
        scratch_shapes=[pltpu.SemaphoreType.DMA((n, 7)), pltpu.SemaphoreType.DMA((n, 7)), pltpu.SemaphoreType.DMA((n,))],
    )(*gs)


def _all_reduce_small(v):
    R, C = v.shape

    def body(v_ref, o_ref, buf, send, recv):
        x, y, c = _mesh_pos()
        me = 4 * x + 2 * y + c
        buf[me] = v_ref[...]
        copies = []
        for m in range(1, N_DEV):
            px = 1 - x if m & 4 else x
            py = 1 - y if m & 2 else y
            pc = 1 - c if m & 1 else c
            peer = 4 * px + 2 * py + pc
            copies.append((pltpu.make_async_remote_copy(
                src_ref=v_ref, dst_ref=buf.at[me], send_sem=send.at[m - 1], recv_sem=recv.at[m - 1],
                device_id=(px, py, pc), device_id_type=MESH),
                pltpu.make_async_remote_copy(
                src_ref=v_ref, dst_ref=buf.at[peer], send_sem=send.at[m - 1], recv_sem=recv.at[m - 1],
                device_id=(px, py, pc), device_id_type=MESH)))
        for out_cp, _ in copies:
            out_cp.start()
        for _, in_cp in copies:
            in_cp.wait_recv()
        for out_cp, _ in copies:
            out_cp.wait_send()
        total = buf[0]
        for s in range(1, N_DEV):
            total = total + buf[s]
        o_ref[...] = total

    vmem = pl.BlockSpec(memory_space=pltpu.VMEM)
    return pl.pallas_call(
        body, name="all_reduce_small", in_specs=[vmem], out_specs=vmem, out_shape=jax.ShapeDtypeStruct((R, C), F32),
        scratch_shapes=[pltpu.VMEM((N_DEV, R, C), F32), pltpu.SemaphoreType.DMA((7,)), pltpu.SemaphoreType.DMA((7,))],
    )(v)


def _adamw(name, parts, w, m, v):
    P, R, C = parts.shape
    tr = _tile(R, 256, 16)

    def body(p_ref, w_ref, m_ref, v_ref, g_out, d_out, m_out, v_out):
        g = p_ref[0].astype(F32)
        for s in range(1, P):
            g = g + p_ref[s].astype(F32)
        m_new = ADAM_B1 * m_ref[...] + (1.0 - ADAM_B1) * g
        v_new = ADAM_B2 * v_ref[...] + (1.0 - ADAM_B2) * (g * g)
        m_hat = m_new / (1.0 - ADAM_B1 ** ADAM_STEP)
        v_hat = v_new / (1.0 - ADAM_B2 ** ADAM_STEP)
        g_out[...] = g
        d_out[...] = -ADAM_LR * (m_hat / (jnp.sqrt(v_hat) + ADAM_EPS) + ADAM_WD * w_ref[...])
        m_out[...] = m_new
        v_out[...] = v_new

    spec = pl.BlockSpec((tr, C), lambda i: (i, 0))
    shp = jax.ShapeDtypeStruct((R, C), F32)
    return pl.pallas_call(body, name=name, grid=(R // tr,),
                          in_specs=[pl.BlockSpec((P, tr, C), lambda i: (0, i, 0)), spec, spec, spec],
                          out_specs=[spec] * 4, out_shape=[shp] * 4, compiler_params=_params(("arbitrary",)))(parts, w, m, v)


def _rope_tables(positions):
    pos = positions.reshape(-1).astype(F32)[:, None]
    S = pos.shape[0]

    def cs(dim):
        inv = ROPE_THETA ** (-jnp.arange(0, dim, 2, dtype=F32) / dim)
        ang = pos * inv
        return jnp.cos(ang), jnp.sin(ang)

    ca, sa = cs(ROPE_DIM)
    cp, sp = cs(PART_ROPE)
    z = lambda w: jnp.zeros((S, w), F32)
    return (jnp.concatenate([ca, ca, z(LANE - ROPE_DIM)], axis=1), jnp.concatenate([sa, sa, z(LANE - ROPE_DIM)], axis=1),
            jnp.concatenate([cp, cp, jnp.ones((S, LANE - PART_ROPE), F32)], axis=1),
            jnp.concatenate([sp, sp, z(LANE - PART_ROPE)], axis=1))


def _layer_params(gathered, gains, l):
    g = gathered
    row = lambda n: gains[n][l][None, :]
    ffn = lambda t: dict(pre_g=row(t + '_pre_g'), post_g=row(t + '_post_g'), w_gate=g[t + '_w_gate'],
                         w_up=g[t + '_w_up'], w_down=g[t + '_w_down'])
    mix = dict(pre_g=row('mix_pre_g'), post_g=row('mix_post_g'), q_norm_g=row('mla_q_norm_g'),
               kv_norm_g=row('mla_kv_norm_g'), w_in=_win_layout(g['w_in']), w_uq=_wuq_layout(g['mla_w_uq']),
               w_ukv=_wukv_layout(g['mla_w_ukv']), w_o=g['w_o'].reshape(-1, g['w_o'].shape[-1]))
    return ffn('ffn1'), mix, ffn('ffn2')


def _grad_slots(d1, dm, d2):
    return [d1['w_gate'], d1['w_up'], d1['w_down'], _win_unlayout(dm['w_in']), _wuq_unlayout(dm['w_uq']),
            _wukv_unlayout(dm['w_ukv']), dm['w_o'].reshape(N_DEV, -1, dm['w_o'].shape[-1]),
            d2['w_gate'], d2['w_up'], d2['w_down']]


def _gain_grads(d1, dm, d2):
    return dict(ffn1_pre_g=d1['pre_g'], ffn1_post_g=d1['post_g'], mix_pre_g=dm['pre_g'], mix_post_g=dm['post_g'],
                mla_q_norm_g=dm['q_norm_g'], mla_kv_norm_g=dm['kv_norm_g'], ffn2_pre_g=d2['pre_g'], ffn2_post_g=d2['post_g'])


def _pack(vecs, width):
    flat = jnp.concatenate([v.reshape(-1) for v in vecs])
    per = 8 * width
    flat = jnp.pad(flat, (0, (-flat.shape[0]) % per))
    return flat.reshape(-1, width)


def _unpack(packed, shapes):
    flat = packed.reshape(-1)
    out, off = [], 0
    for shp in shapes:
        size = math.prod(shp)
        out.append(flat[off:off + size].reshape(shp))
        off += size
    return out


def kernel(x, positions, ffn1_pre_g, ffn1_post_g, ffn1_w_gate, ffn1_w_up, ffn1_w_down, mix_pre_g, mix_post_g, w_in, mla_q_norm_g, mla_w_uq, mla_kv_norm_g, mla_w_ukv, w_o, ffn2_pre_g, ffn2_post_g, ffn2_w_gate, ffn2_w_up, ffn2_w_down, loss_target, m_ffn1_pre_g, m_ffn1_post_g, m_ffn1_w_gate, m_ffn1_w_up, m_ffn1_w_down, m_mix_pre_g, m_mix_post_g, m_w_in, m_mla_q_norm_g, m_mla_w_uq, m_mla_kv_norm_g, m_mla_w_ukv, m_w_o, m_ffn2_pre_g, m_ffn2_post_g, m_ffn2_w_gate, m_ffn2_w_up, m_ffn2_w_down, v_ffn1_pre_g, v_ffn1_post_g, v_ffn1_w_gate, v_ffn1_w_up, v_ffn1_w_down, v_mix_pre_g, v_mix_post_g, v_w_in, v_mla_q_norm_g, v_mla_w_uq, v_mla_kv_norm_g, v_mla_w_ukv, v_w_o, v_ffn2_pre_g, v_ffn2_post_g, v_ffn2_w_gate, v_ffn2_w_up, v_ffn2_w_down):
    w = dict(zip(WNAMES, (ffn1_pre_g, ffn1_post_g, ffn1_w_gate, ffn1_w_up, ffn1_w_down, mix_pre_g, mix_post_g, w_in,
                          mla_q_norm_g, mla_w_uq, mla_kv_norm_g, mla_w_ukv, w_o, ffn2_pre_g, ffn2_post_g,
                          ffn2_w_gate, ffn2_w_up, ffn2_w_down)))
    mom = dict(zip(WNAMES, (m_ffn1_pre_g, m_ffn1_post_g, m_ffn1_w_gate, m_ffn1_w_up, m_ffn1_w_down, m_mix_pre_g,
                            m_mix_post_g, m_w_in, m_mla_q_norm_g, m_mla_w_uq, m_mla_kv_norm_g, m_mla_w_ukv, m_w_o,
                            m_ffn2_pre_g, m_ffn2_post_g, m_ffn2_w_gate, m_ffn2_w_up, m_ffn2_w_down)))
    var = dict(zip(WNAMES, (v_ffn1_pre_g, v_ffn1_post_g, v_ffn1_w_gate, v_ffn1_w_up, v_ffn1_w_down, v_mix_pre_g,
                            v_mix_post_g, v_w_in, v_mla_q_norm_g, v_mla_w_uq, v_mla_kv_norm_g, v_mla_w_ukv, v_w_o,
                            v_ffn2_pre_g, v_ffn2_post_g, v_ffn2_w_gate, v_ffn2_w_up, v_ffn2_w_down)))
    depth = w_in.shape[0]
    xs = x[0]
    D = xs.shape[1]
    tabs = _rope_tables(positions)

    shards = {n: _cast_bf16(w[n]) for n in BIG}
    params = []
    for l in range(depth):
        gathered = dict(zip(BIG, _all_gather([shards[n][l] for n in BIG])))
        params.append(_layer_params(gathered, w, l))

    saved = []
    act = xs
    for l in range(depth):
        p1, pm, p2 = params[l]
        act, s1 = _ffn_fwd(act, p1)
        act, sm = _mix_fwd(act, pm, tabs)
        act, s2 = _ffn_fwd(act, p2)
        saved.append((s1, sm, s2))
    dact, loss_part = _loss_head(act, loss_target[0])

    received, gain_parts = [None] * depth, [None] * depth
    for l in reversed(range(depth)):
        p1, pm, p2 = params[l]
        s1, sm, s2 = saved[l]
        dact, d2 = _ffn_bwd(dact, p2, s2)
        dact, dm = _mix_bwd(dact, pm, tabs, sm)
        dact, d1 = _ffn_bwd(dact, p1, s1)
        received[l] = _exchange_slots(_grad_slots(d1, dm, d2))
        gain_parts[l] = _gain_grads(d1, dm, d2)

    gain_local = [jnp.stack([gain_parts[l][n].reshape(-1) for l in range(depth)]) for n in GAINS]
    packed = _all_reduce_small(_pack(gain_local + [loss_part.reshape(1)], D))
    summed = _unpack(packed, [w[n].shape for n in GAINS] + [(1,)])
    loss = summed[-1][0]

    out = {}
    for i, n in enumerate(BIG):
        shp = w[n].shape
        res = []
        for l in range(depth):
            parts = received[l][i]
            res.append(_adamw("adamw_" + n, parts.reshape(N_DEV, -1, shp[-1]), w[n][l].reshape(-1, shp[-1]),
                              mom[n][l].reshape(-1, shp[-1]), var[n][l].reshape(-1, shp[-1])))
        out[n] = [jnp.stack([res[l][t] for l in range(depth)]).reshape(shp) for t in range(4)]
    pk = lambda d: _pack([d[n] for n in GAINS], D)
    g_pack = _pack(summed[:-1], D)
    res = _adamw("adamw_gains", g_pack[None], pk(w), pk(mom), pk(var))
    for t in range(4):
        for n, a in zip(GAINS, _unpack(res[t], [w[n].shape for n in GAINS])):
            out.setdefault(n, [None] * 4)[t] = a

    grads = [out[n][0] for n in WNAMES]
    deltas = [out[n][1] for n in WNAMES]
    new_m = [out[n][2] for n in WNAMES]
    new_v = [out[n][3] for n in WNAMES]
    return (loss, dact[None], *grads, *deltas, *new_m, *new_v)
```

```python
import functools
import math

import jax
import jax.numpy as jnp
from jax import lax
from jax.experimental import pallas as pl
from jax.experimental.pallas import tpu as pltpu

F32 = jnp.float32
BF16 = jnp.bfloat16
N_DEV = 8
MESH = pl.DeviceIdType.MESH

HEADS = 8
HEAD_DIM = 128
Q_RANK = 512
KV_RANK = 512
ROPE_DIM = 64
QK_PAD = 256
PART_ROPE = 32
DIL_PATTERNS = ((128, 1), (512, 4), (2048, 16))
ROPE_THETA = 500000.0
RMS_EPS = 1e-6
NEG = -1e30
LANE = 128
IN_COLS = 4160
IN_PAD = 4224
DIL_W = HEADS * HEAD_DIM

ADAM_LR, ADAM_B1, ADAM_B2, ADAM_EPS, ADAM_WD, ADAM_STEP = 0.001, 0.9, 0.999, 1e-08, 0.01, 10

VMEM_LIMIT = 56 * 1024 * 1024
SUB_ROWS = 256

WNAMES = ['ffn1_pre_g', 'ffn1_post_g', 'ffn1_w_gate', 'ffn1_w_up', 'ffn1_w_down', 'mix_pre_g', 'mix_post_g', 'w_in',
          'mla_q_norm_g', 'mla_w_uq', 'mla_kv_norm_g', 'mla_w_ukv', 'w_o', 'ffn2_pre_g', 'ffn2_post_g',
          'ffn2_w_gate', 'ffn2_w_up', 'ffn2_w_down']
BIG = ['ffn1_w_gate', 'ffn1_w_up', 'ffn1_w_down', 'w_in', 'mla_w_uq', 'mla_w_ukv', 'w_o',
       'ffn2_w_gate', 'ffn2_w_up', 'ffn2_w_down']
GAINS = [n for n in WNAMES if n not in BIG]

NT = (((1,), (1,)), ((), ()))
NN = (((1,), (0,)), ((), ()))
TN = (((0,), (0,)), ((), ()))


def _tile(n, target, mult):
    best = None
    t = mult
    while t <= min(n, target):
        if n % t == 0:
            best = t
        t += mult
    return n if best is None else best


def _params(sem=None):
    kw = dict(vmem_limit_bytes=VMEM_LIMIT)
    if sem is not None:
        kw['dimension_semantics'] = sem
    return pltpu.CompilerParams(**kw)


def _dot(a, b, dn):
    return lax.dot_general(a, b, dn, preferred_element_type=F32)


def _mm(name, pairs, pair_specs, dn, grid, k_axis, acc_shape, out_shapes, out_specs, epilogue,
        extras=(), extra_specs=()):
    n_pair, n_ex, n_out = len(pairs), len(extras), len(out_shapes)
    nk = 1 if k_axis is None else grid[k_axis]

    def body(*refs):
        ab = refs[:2 * n_pair]
        ex = refs[2 * n_pair:2 * n_pair + n_ex]
        outs = refs[2 * n_pair + n_ex:2 * n_pair + n_ex + n_out]
        part = _dot(ab[0][...], ab[1][...], dn)
        for p in range(1, n_pair):
            part = part + _dot(ab[2 * p][...], ab[2 * p + 1][...], dn)
        if nk == 1:
            epilogue(part, ex, outs)
            return
        acc = refs[-1]
        k = pl.program_id(k_axis)

        @pl.when(k == 0)
        def _():
            acc[...] = part

        @pl.when(k > 0)
        def _():
            acc[...] += part

        @pl.when(k == nk - 1)
        def _():
            epilogue(acc[...], ex, outs)

    flat, flat_specs = [], []
    for (a, b), (sa, sb) in zip(pairs, pair_specs):
        flat += [a, b]
        flat_specs += [sa, sb]
    return pl.pallas_call(
        body, name=name, grid=grid, in_specs=flat_specs + list(extra_specs), out_specs=out_specs,
        out_shape=out_shapes, scratch_shapes=[pltpu.VMEM(acc_shape, F32)] if nk > 1 else [],
        compiler_params=_params(("arbitrary",) * len(grid)),
    )(*flat, *extras)


def _store(dtype):
    def epi(acc, ex, outs):
        outs[0][...] = acc.astype(dtype)
    return epi


def _mm_nn(name, a, b, out_dtype, tm=1024, tn=1408):
    M, K = a.shape
    N = b.shape[1]
    tm, tn = _tile(M, tm, 8), _tile(N, tn, LANE)
    return _mm(name, [(a, b)],
               [(pl.BlockSpec((tm, K), lambda j, i: (i, 0)), pl.BlockSpec((K, tn), lambda j, i: (0, j)))],
               NN, (N // tn, M // tm), None, None,
               [jax.ShapeDtypeStruct((M, N), out_dtype)], [pl.BlockSpec((tm, tn), lambda j, i: (i, j))],
               _store(out_dtype))[0]


def _mm_nt(name, a, b, out_dtype, tm=1024, tk=1408):
    M, K = a.shape
    N = b.shape[0]
    tm, tk = _tile(M, tm, 8), _tile(K, tk, LANE)
    return _mm(name, [(a, b)],
               [(pl.BlockSpec((tm, tk), lambda i, k: (i, k)), pl.BlockSpec((N, tk), lambda i, k: (0, k)))],
               NT, (M // tm, K // tk), 1, (tm, N),
               [jax.ShapeDtypeStruct((M, N), out_dtype)], [pl.BlockSpec((tm, N), lambda i, k: (i, 0))],
               _store(out_dtype))[0]


def _mm_tn(name, a, b, out_dtype, ts=1024, tn=1408):
    M, K = a.shape
    N = b.shape[1]
    ts, tn = _tile(M, ts, 16), _tile(N, tn, LANE)
    return _mm(name, [(a, b)],
               [(pl.BlockSpec((ts, K), lambda j, m: (m, 0)), pl.BlockSpec((ts, tn), lambda j, m: (m, j)))],
               TN, (N // tn, M // ts), 1, (K, tn),
               [jax.ShapeDtypeStruct((K, N), out_dtype)], [pl.BlockSpec((K, tn), lambda j, m: (0, j))],
               _store(out_dtype))[0]


def _mm_tn_chunks_a(name, a3, b, out_dtype, ts=2048):
    C, M, Kc = a3.shape
    N = b.shape[1]
    ts = _tile(M, ts, 16)
    return _mm(name, [(a3, b)],
               [(pl.BlockSpec((None, ts, Kc), lambda c, m: (c, m, 0)), pl.BlockSpec((ts, N), lambda c, m: (m, 0)))],
               TN, (C, M // ts), 1, (Kc, N),
               [jax.ShapeDtypeStruct((C, Kc, N), out_dtype)], [pl.BlockSpec((None, Kc, N), lambda c, m: (c, 0, 0))],
               _store(out_dtype))[0]


def _mm_tn_chunks_b(name, a, b3, out_dtype, ts=2048):
    M, K = a.shape
    C, _, Nc = b3.shape
    ts = _tile(M, ts, 16)
    return _mm(name, [(a, b3)],
               [(pl.BlockSpec((ts, K), lambda c, m: (m, 0)), pl.BlockSpec((None, ts, Nc), lambda c, m: (c, m, 0)))],
               TN, (C, M // ts), 1, (K, Nc),
               [jax.ShapeDtypeStruct((C, K, Nc), out_dtype)], [pl.BlockSpec((None, K, Nc), lambda c, m: (c, 0, 0))],
               _store(out_dtype))[0]


def _rows(name, body, n_rows, tm, ins, outs, accs=()):
    in_specs, arrays = [], []
    for spec in ins:
        if spec[0] == 'row':
            _, arr, width, cb = spec
            in_specs.append(pl.BlockSpec((tm, width), functools.partial(lambda i, cb: (i, cb), cb=cb)))
        else:
            arr = spec[1]
            in_specs.append(pl.BlockSpec(arr.shape, functools.partial(lambda i, nd: (0,) * nd, nd=arr.ndim)))
        arrays.append(arr)
    out_shapes = [jax.ShapeDtypeStruct((n_rows, w), dt) for w, dt in outs]
    out_specs = [pl.BlockSpec((tm, w), lambda i: (i, 0)) for w, _ in outs]
    out_shapes += [jax.ShapeDtypeStruct((1, w), F32) for w in accs]
    out_specs += [pl.BlockSpec((1, w), lambda i: (0, 0)) for w in accs]
    return pl.pallas_call(body, name=name, grid=(n_rows // tm,), in_specs=in_specs, out_specs=out_specs,
                          out_shape=out_shapes, compiler_params=_params(("arbitrary",)))(*arrays)


def _acc_add(ref, val):
    @pl.when(pl.program_id(0) == 0)
    def _():
        ref[...] = val

    @pl.when(pl.program_id(0) > 0)
    def _():
        ref[...] += val


def _rms_scale(x):
    return lax.rsqrt(jnp.mean(x * x, axis=-1, keepdims=True) + RMS_EPS)


def _rms_bwd(x, g, dy):
    r = _rms_scale(x)
    t = dy * g
    dx = r * t - x * (r * r * r) * jnp.mean(t * x, axis=-1, keepdims=True)
    return dx, dy * x * r


def _rot_half(x, hw):
    lane = lax.broadcasted_iota(jnp.int32, x.shape, 1)
    left = pltpu.roll(x, LANE - hw, 1)
    right = pltpu.roll(x, hw, 1)
    return jnp.where(lane < hw, -left, right)


def _rope(x, cos, sin, hw):
    return x * cos + _rot_half(x, hw) * sin


def _rope_t(dy, cos, sin, hw):
    return dy * cos - _rot_half(dy, hw) * sin


def _rms_cast(name, x, g):
    S, D = x.shape
    tm = _tile(S, 512, 8)

    def body(x_ref, g_ref, o_ref):
        xv = x_ref[...]
        o_ref[...] = (xv * _rms_scale(xv) * g_ref[...]).astype(BF16)

    return _rows(name, body, S, tm, [('row', x, D, 0), ('full', g)], [(D, BF16)])[0]


def _postnorm_bwd(name, dxo, y, g, coef):
    S, D = y.shape
    tm = _tile(S, 512, 8)

    def body(d_ref, y_ref, g_ref, dy_ref, dg_ref):
        dx, dg = _rms_bwd(y_ref[...], g_ref[...], coef * d_ref[...])
        dy_ref[...] = dx.astype(BF16)
        _acc_add(dg_ref, jnp.sum(dg, axis=0, keepdims=True))

    return _rows(name, body, S, tm, [('row', dxo, D, 0), ('row', y, D, 0), ('full', g)], [(D, BF16)], [D])


def _prenorm_bwd(name, dh, x, g, dxo):
    S, D = x.shape
    tm = _tile(S, 512, 8)

    def body(dh_ref, x_ref, g_ref, d_ref, dx_ref, dg_ref):
        dx, dg = _rms_bwd(x_ref[...], g_ref[...], dh_ref[...])
        dx_ref[...] = d_ref[...] + dx
        _acc_add(dg_ref, jnp.sum(dg, axis=0, keepdims=True))

    return _rows(name, body, S, tm, [('row', dh, D, 0), ('row', x, D, 0), ('full', g), ('row', dxo, D, 0)],
                 [(D, F32)], [D])


def _loss_head(y, target):
    S, D = y.shape
    tm = _tile(S, 512, 8)

    def body(y_ref, t_ref, dy_ref, l_ref):
        e = y_ref[...] - t_ref[...]
        dy_ref[...] = e * (1.0 / D)
        row = 0.5 * jnp.mean(e * e, axis=-1, keepdims=True)
        _acc_add(l_ref, jnp.broadcast_to(jnp.sum(row, axis=0, keepdims=True), (1, LANE)))

    dy, l = _rows("loss_head", body, S, tm, [('row', y, D, 0), ('row', target, D, 0)], [(D, F32)], [LANE])
    return dy, l[0, 0]


def _ffn_up(h, wg, wu):
    S, D = h.shape
    C, _, Fc = wg.shape
    tm = _tile(S, 1024, 8)

    sub = _tile(tm, SUB_ROWS, 8)

    def body(h_ref, wg_ref, wu_ref, g_ref, u_ref, a_ref):
        for r in range(tm // sub):
            rows = slice(r * sub, (r + 1) * sub)
            hv = h_ref[rows, :]
            g = _dot(hv, wg_ref[...], NN)
            u = _dot(hv, wu_ref[...], NN)
            g_ref[rows, :] = g.astype(BF16)
            u_ref[rows, :] = u.astype(BF16)
            a_ref[rows, :] = (g * jax.nn.sigmoid(g) * u).astype(BF16)

    w_spec = pl.BlockSpec((None, D, Fc), lambda j, i: (j, 0, 0))
    o_spec = pl.BlockSpec((None, tm, Fc), lambda j, i: (j, i, 0))
    shp = jax.ShapeDtypeStruct((C, S, Fc), BF16)
    return pl.pallas_call(body, name="ffn_up", grid=(C, S // tm),
                          in_specs=[pl.BlockSpec((tm, D), lambda j, i: (i, 0)), w_spec, w_spec],
                          out_specs=[o_spec, o_spec, o_spec], out_shape=[shp, shp, shp],
                          compiler_params=_params(("arbitrary", "arbitrary")))(h, wg, wu)


def _chunk_post(name, a3, w3, x, g, coef):
    C, S, Kc = a3.shape
    D = w3.shape[2]
    tm = _tile(S, 512, 8)

    def epi(acc, ex, outs):
        x_ref, g_ref = ex
        outs[0][...] = x_ref[...] + coef * (acc * _rms_scale(acc) * g_ref[...])
        outs[1][...] = acc

    row = pl.BlockSpec((tm, D), lambda i, c: (i, 0))
    shp = jax.ShapeDtypeStruct((S, D), F32)
    per = 2 if C % 2 == 0 else 1
    specs = [(pl.BlockSpec((None, tm, Kc), functools.partial(lambda i, c, o: (per * c + o, i, 0), o=o)),
              pl.BlockSpec((None, Kc, D), functools.partial(lambda i, c, o: (per * c + o, 0, 0), o=o))) for o in range(per)]
    return _mm(name, [(a3, w3)] * per, specs, NN, (S // tm, C // per), 1, (tm, D), [shp, shp], [row, row], epi,
               extras=[x, g], extra_specs=[row, pl.BlockSpec((1, D), lambda i, c: (0, 0))])


def _ffn_da(dy, wd, gate, up):
    S, D = dy.shape
    C, Fc, _ = wd.shape
    tm = _tile(S, 1024, 8)

    sub = _tile(tm, SUB_ROWS, 8)

    def body(dy_ref, wd_ref, g_ref, u_ref, dg_ref, du_ref):
        for r in range(tm // sub):
            rows = slice(r * sub, (r + 1) * sub)
            da = _dot(dy_ref[rows, :], wd_ref[...], NT)
            g = g_ref[rows, :].astype(F32)
            u = u_ref[rows, :].astype(F32)
            sig = jax.nn.sigmoid(g)
            dg_ref[rows, :] = (da * u * (sig * (1.0 + g * (1.0 - sig)))).astype(BF16)
            du_ref[rows, :] = (da * (g * sig)).astype(BF16)

    blk = pl.BlockSpec((None, tm, Fc), lambda c, i: (c, i, 0))
    shp = jax.ShapeDtypeStruct((C, S, Fc), BF16)
    return pl.pallas_call(body, name="ffn_da", grid=(C, S // tm),
                          in_specs=[pl.BlockSpec((tm, D), lambda c, i: (i, 0)),
                                    pl.BlockSpec((None, Fc, D), lambda c, i: (c, 0, 0)), blk, blk],
                          out_specs=[blk, blk], out_shape=[shp, shp],
                          compiler_params=_params(("arbitrary", "arbitrary")))(dy, wd, gate, up)


def _ffn_dh(dg, du, wg, wu):
    C, S, Fc = dg.shape
    D = wg.shape[1]
    tm = _tile(S, 1024, 8)
    a_spec = pl.BlockSpec((None, tm, Fc), lambda i, c: (c, i, 0))
    w_spec = pl.BlockSpec((None, D, Fc), lambda i, c: (c, 0, 0))
    return _mm("ffn_dh", [(dg, wg), (du, wu)], [(a_spec, w_spec), (a_spec, w_spec)], NT, (S // tm, C), 1, (tm, D),
               [jax.ShapeDtypeStruct((S, D), F32)], [pl.BlockSpec((tm, D), lambda i, c: (i, 0))], _store(F32))[0]


def _ffn_fwd(x, p):
    h = _rms_cast("ffn_prenorm", x, p['pre_g'])
    gate, up, act = _ffn_up(h, p['w_gate'], p['w_up'])
    x_out, y = _chunk_post("ffn_down", act, p['w_down'], x, p['post_g'], 0.5)
    return x_out, (x, h, gate, up, act, y)


def _ffn_bwd(dxo, p, saved):
    x, h, gate, up, act, y = saved
    dy, d_post = _postnorm_bwd("ffn_postnorm_bwd", dxo, y, p['post_g'], 0.5)
    dgate, dup = _ffn_da(dy, p['w_down'], gate, up)
    d_wd = _mm_tn_chunks_a("ffn_dwd", act, dy, BF16)
    dh = _ffn_dh(dgate, dup, p['w_gate'], p['w_up'])
    d_wg = _mm_tn_chunks_b("ffn_dwg", h, dgate, BF16)
    d_wu = _mm_tn_chunks_b("ffn_dwu", h, dup, BF16)
    dx, d_pre = _prenorm_bwd("ffn_prenorm_bwd", dh, x, p['pre_g'], dxo)
    return dx, dict(pre_g=d_pre, post_g=d_post, w_gate=d_wg, w_up=d_wu, w_down=d_wd)


def _causal_mask(t):
    r = lax.broadcasted_iota(jnp.int32, (t, t), 0)
    c = lax.broadcasted_iota(jnp.int32, (t, t), 1)
    return r >= c


def _grid_ends(grid):
    first = lambda: jnp.logical_and(pl.program_id(0) == 0, pl.program_id(1) == 0)
    last = lambda: jnp.logical_and(pl.program_id(0) == grid[0] - 1, pl.program_id(1) == grid[1] - 1)
    return first, last


def _with_rider(rider, name, grid, compute, n_in, in_specs, out_specs, out_shape, arrays):
    any_spec = pl.BlockSpec(memory_space=pl.ANY)
    r = 0 if rider is None else rider.n
    outs = pl.pallas_call(
        _ride(rider, n_in, len(out_shape), *_grid_ends(grid), compute), name=name, grid=grid,
        in_specs=list(in_specs) + [any_spec] * r, out_specs=list(out_specs) + [any_spec] * r,
        out_shape=list(out_shape) + ([] if rider is None else rider.out_shapes),
        scratch_shapes=[] if rider is None else rider.scratch,
        compiler_params=_params(("arbitrary",) * len(grid)))(*arrays, *([] if rider is None else rider.arrays))
    return outs[:len(out_shape)], outs[len(out_shape):]


def _mla_fwd(q, k, kv, rider=None):
    S = q.shape[0]
    t = _tile(S, 512, LANE)
    nq = S // t
    scale = 1.0 / math.sqrt(HEAD_DIM + ROPE_DIM)

    def compute(ins, outs):
        q_ref, k_ref, v_ref = ins
        o_ref, lse_ref = outs
        i = pl.program_id(1)
        qb = q_ref[...]

        def step(j, carry, masked):
            m, l, acc = carry
            rows = pl.ds(pl.multiple_of(j * t, t), t)
            s = _dot(qb, k_ref[rows, :], NT) * scale
            if masked:
                s = jnp.where(_causal_mask(t), s, NEG)
            m_new = jnp.maximum(m, jnp.max(s, axis=-1, keepdims=True))
            alpha = jnp.exp(m - m_new)
            pr = jnp.exp(s - m_new)
            l = alpha * l + jnp.sum(pr, axis=-1, keepdims=True)
            acc = alpha * acc + _dot(pr.astype(BF16), v_ref[rows, :], NN)
            return m_new, l, acc

        init = (jnp.full((t, 1), NEG, F32), jnp.zeros((t, 1), F32), jnp.zeros((t, HEAD_DIM), F32))
        carry = lax.fori_loop(0, i, lambda j, c: step(j, c, False), init)
        m, l, acc = step(i, carry, True)
        o_ref[...] = (acc / l).astype(BF16)
        lse_ref[...] = jnp.broadcast_to(m + jnp.log(l), (t, LANE))

    return _with_rider(
        rider, "mla_fwd", (HEADS, nq), compute, 3,
        [pl.BlockSpec((t, QK_PAD), lambda h, i: (i, h)), pl.BlockSpec((S, QK_PAD), lambda h, i: (0, h)),
         pl.BlockSpec((S, HEAD_DIM), lambda h, i: (0, HEADS + h))],
        [pl.BlockSpec((t, HEAD_DIM), lambda h, i: (i, h)), pl.BlockSpec((None, t, LANE), lambda h, i: (h, i, 0))],
        [jax.ShapeDtypeStruct((S, HEADS * HEAD_DIM), BF16), jax.ShapeDtypeStruct((HEADS, S, LANE), F32)], (q, k, kv))


def _mla_stats(do, o, lse):
    S = o.shape[0]
    t = _tile(S, 512, 8)

    def body(do_ref, o_ref, lse_ref, st_ref):
        delta = jnp.sum(do_ref[...].astype(F32) * o_ref[...].astype(F32), axis=-1, keepdims=True)
        lane = lax.broadcasted_iota(jnp.int32, (t, LANE), 1)
        st_ref[...] = jnp.where(lane < LANE // 2, lse_ref[...], jnp.broadcast_to(delta, (t, LANE)))

    blk = pl.BlockSpec((t, HEAD_DIM), lambda h, i: (i, h))
    st = pl.BlockSpec((None, t, LANE), lambda h, i: (h, i, 0))
    return pl.pallas_call(body, name="mla_stats", grid=(HEADS, S // t), in_specs=[blk, blk, st], out_specs=st,
                          out_shape=jax.ShapeDtypeStruct((HEADS, S, LANE), F32),
                          compiler_params=_params(("arbitrary", "arbitrary")))(do, o, lse)


def _mla_bwd(q, k, kv, do, stats, rider=None):
    S = q.shape[0]
    t = _tile(S, 512, LANE)
    nq = S // t
    scale = 1.0 / math.sqrt(HEAD_DIM + ROPE_DIM)

    def compute(ins, outs):
        q_ref, do_ref, st_ref, k_ref, v_ref = ins
        dq_ref, dk_ref, dv_ref = outs
        j = pl.program_id(1)

        @pl.when(j == 0)
        def _():
            dq_ref[...] = jnp.zeros_like(dq_ref)

        kb = k_ref[...]
        vb = v_ref[...]

        def step(i, carry, masked):
            dk, dv = carry
            rows = pl.ds(pl.multiple_of(i * t, t), t)
            qb = q_ref[rows, :]
            dob = do_ref[rows, :]
            pr = jnp.exp(_dot(qb, kb, NT) * scale - st_ref[rows, 0:1])
            if masked:
                pr = jnp.where(_causal_mask(t), pr, 0.0)
            dv = dv + _dot(pr.astype(BF16), dob, TN)
            dp = _dot(dob, vb, NT)
            ds = (pr * (dp - st_ref[rows, LANE // 2:LANE // 2 + 1]) * scale).astype(BF16)
            dk = dk + _dot(ds, qb, TN)
            dq_ref[rows, :] += _dot(ds, kb, NN)
            return dk, dv

        carry = step(j, (jnp.zeros((t, QK_PAD), F32), jnp.zeros((t, HEAD_DIM), F32)), True)
        dk, dv = lax.fori_loop(j + 1, nq, lambda i, c: step(i, c, False), carry)
        dk_ref[...] = dk
        dv_ref[...] = dv.astype(BF16)

    return _with_rider(
        rider, "mla_bwd", (HEADS, nq), compute, 5,
        [pl.BlockSpec((S, QK_PAD), lambda h, j: (0, h)), pl.BlockSpec((S, HEAD_DIM), lambda h, j: (0, h)),
         pl.BlockSpec((None, S, LANE), lambda h, j: (h, 0, 0)),
         pl.BlockSpec((t, QK_PAD), lambda h, j: (j, h)), pl.BlockSpec((t, HEAD_DIM), lambda h, j: (j, HEADS + h))],
        [pl.BlockSpec((S, QK_PAD), lambda h, j: (0, h)), pl.BlockSpec((t, QK_PAD), lambda h, j: (j, h)),
         pl.BlockSpec((t, HEAD_DIM), lambda h, j: (j, h))],
        [jax.ShapeDtypeStruct((S, HEADS * QK_PAD), F32), jax.ShapeDtypeStruct((S, HEADS * QK_PAD), F32),
         jax.ShapeDtypeStruct((S, HEADS * HEAD_DIM), BF16)], (q, do, stats, k, kv))


DIL_BLK = 128
DIL_HB = 2


def _dil_tiles(S, d):
    L = S // d
    tq = _tile(L, 4 * DIL_BLK, DIL_BLK)
    return L, tq, tq // DIL_BLK


def _dil_masks(absent):
    qi = lax.broadcasted_iota(jnp.int32, (DIL_BLK, DIL_BLK), 0)
    kj = lax.broadcasted_iota(jnp.int32, (DIL_BLK, DIL_BLK), 1)
    edge = kj >= qi + jnp.where(absent, DIL_BLK, 0)
    return edge, kj >= qi, kj <= qi


def _dil_fwd(q, k, v, d):
    S = q.shape[0]
    L, tq, nb = _dil_tiles(S, d)
    W = DIL_HB * HEAD_DIM
    scale = 1.0 / math.sqrt(HEAD_DIM)
    view = lambda a: a.reshape(L, d * DIL_W)

    def body(q_ref, k_ref, v_ref, kp_ref, vp_ref, o_ref, lse_ref):
        m_edge, m_prev, m_cur = _dil_masks(pl.program_id(0) == 0)
        for hb in range(DIL_HB):
            cols = slice(hb * HEAD_DIM, (hb + 1) * HEAD_DIM)
            for b in range(nb):
                rows = slice(b * DIL_BLK, (b + 1) * DIL_BLK)
                qb = q_ref[rows, cols]
                if b == 0:
                    kp, vp = kp_ref[:, cols], vp_ref[:, cols]
                    ok_prev = m_edge
                else:
                    prev = slice((b - 1) * DIL_BLK, b * DIL_BLK)
                    kp, vp = k_ref[prev, cols], v_ref[prev, cols]
                    ok_prev = m_prev
                s_p = jnp.where(ok_prev, _dot(qb, kp, NT) * scale, NEG)
                s_c = jnp.where(m_cur, _dot(qb, k_ref[rows, cols], NT) * scale, NEG)
                m = jnp.maximum(jnp.max(s_p, axis=-1, keepdims=True), jnp.max(s_c, axis=-1, keepdims=True))
                e_p, e_c = jnp.exp(s_p - m), jnp.exp(s_c - m)
                l = jnp.sum(e_p, axis=-1, keepdims=True) + jnp.sum(e_c, axis=-1, keepdims=True)
                lse = m + jnp.log(l)
                p_p, p_c = jnp.exp(s_p - lse).astype(BF16), jnp.exp(s_c - lse).astype(BF16)
                o_ref[rows, cols] = _dot(p_p, vp, NN) + _dot(p_c, v_ref[rows, cols], NN)
                lse_ref[rows, cols] = jnp.broadcast_to(lse, (DIL_BLK, HEAD_DIM))

    cur = pl.BlockSpec((tq, W), lambda n, cb: (n, cb))
    prv = pl.BlockSpec((DIL_BLK, W), lambda n, cb: (jnp.maximum(n * nb - 1, 0), cb))
    shp = jax.ShapeDtypeStruct((L, d * DIL_W), F32)
    o, lse = pl.pallas_call(body, name=f"dil_fwd_{d}", grid=(L // tq, d * HEADS // DIL_HB),
                            in_specs=[cur, cur, cur, prv, prv], out_specs=[cur, cur], out_shape=[shp, shp],
                            compiler_params=_params(("arbitrary", "arbitrary")))(view(q), view(k), view(v), view(k), view(v))
    return o.reshape(S, DIL_W), lse.reshape(S, DIL_W)


def _dil_bwd(q, k, v, do, lse, dd, d):
    S = q.shape[0]
    L, tq, nb = _dil_tiles(S, d)
    nt = L // tq
    W = DIL_HB * HEAD_DIM
    scale = 1.0 / math.sqrt(HEAD_DIM)
    view = lambda a: a.reshape(L, d * DIL_W)

    def body(q_ref, k_ref, v_ref, do_ref, lse_ref, dd_ref, kp_ref, vp_ref, qn_ref, don_ref, lsen_ref, ddn_ref,
             dq_ref, dk_ref, dv_ref):
        n = pl.program_id(0)
        no_prev, m_prev, m_cur = _dil_masks(n == 0)
        has_next = _dil_masks(n == nt - 1)[0]

        def pair(qb, kb, vb, dob, lse_b, dd_b, mask):
            pr = jnp.where(mask, jnp.exp(_dot(qb, kb, NT) * scale - lse_b), 0.0)
            ds = (pr * (_dot(dob, vb, NT) - dd_b) * scale).astype(BF16)
            return _dot(ds, kb, NN), _dot(ds, qb, TN), _dot(pr.astype(BF16), dob, TN)

        for hb in range(DIL_HB):
            cols = slice(hb * HEAD_DIM, (hb + 1) * HEAD_DIM)
            stat = slice(hb * HEAD_DIM, hb * HEAD_DIM + 1)
            dks, dvs = [], []
            for b in range(nb):
                rows = slice(b * DIL_BLK, (b + 1) * DIL_BLK)
                qb, dob = q_ref[rows, cols], do_ref[rows, cols]
                lse_b, dd_b = lse_ref[rows, stat], dd_ref[rows, stat]
                dq, dk, dv = pair(qb, k_ref[rows, cols], v_ref[rows, cols], dob, lse_b, dd_b, m_cur)
                if b == 0:
                    dq_p, _, _ = pair(qb, kp_ref[:, cols], vp_ref[:, cols], dob, lse_b, dd_b, no_prev)
                else:
                    prev = slice((b - 1) * DIL_BLK, b * DIL_BLK)
                    dq_p, dk_p, dv_p = pair(qb, k_ref[prev, cols], v_ref[prev, cols], dob, lse_b, dd_b, m_prev)
                    dks[b - 1] = dks[b - 1] + dk_p
                    dvs[b - 1] = dvs[b - 1] + dv_p
                dq_ref[rows, cols] = dq + dq_p
                dks.append(dk)
                dvs.append(dv)
            last = slice((nb - 1) * DIL_BLK, nb * DIL_BLK)
            _, dk_n, dv_n = pair(qn_ref[:, cols], k_ref[last, cols], v_ref[last, cols], don_ref[:, cols],
                                 lsen_ref[:, stat], ddn_ref[:, stat], has_next)
            dks[nb - 1] = dks[nb - 1] + dk_n
            dvs[nb - 1] = dvs[nb - 1] + dv_n
            for b in range(nb):
                rows = slice(b * DIL_BLK, (b + 1) * DIL_BLK)
                dk_ref[rows, cols] = dks[b]
                dv_ref[rows, cols] = dvs[b]

    cur = pl.BlockSpec((tq, W), lambda n, cb: (n, cb))
    prv = pl.BlockSpec((DIL_BLK, W), lambda n, cb: (jnp.maximum(n * nb - 1, 0), cb))
    nxt = pl.BlockSpec((DIL_BLK, W), lambda n, cb: (jnp.minimum((n + 1) * nb, L // DIL_BLK - 1), cb))
    shp = jax.ShapeDtypeStruct((L, d * DIL_W), F32)
    qv, kv_, vv, dov, lv, ddv = (view(a) for a in (q, k, v, do, lse, dd))
    outs = pl.pallas_call(body, name=f"dil_bwd_{d}", grid=(nt, d * HEADS // DIL_HB),
                          in_specs=[cur] * 6 + [prv, prv] + [nxt] * 4, out_specs=[cur] * 3, out_shape=[shp] * 3,
                          compiler_params=_params(("arbitrary", "arbitrary")))(
                              qv, kv_, vv, dov, lv, ddv, kv_, vv, qv, dov, lv, ddv)
    return [a.reshape(S, DIL_W) for a in outs]


def _dil_merge(os_, lses):
    S = os_[0].shape[0]
    tm = _tile(S, 256, 8)

    def body(o0, o1, o2, l0, l1, l2, out_ref):
        ls = [l0[...], l1[...], l2[...]]
        m = jnp.maximum(jnp.maximum(ls[0], ls[1]), ls[2])
        es = [jnp.exp(l - m) for l in ls]
        den = es[0] + es[1] + es[2]
        out_ref[...] = ((es[0] * o0[...] + es[1] * o1[...] + es[2] * o2[...]) / den).astype(BF16)

    ins = [('row', a, DIL_W, 0) for a in list(os_) + list(lses)]
    return _rows("dil_merge", body, S, tm, ins, [(DIL_W, BF16)])[0]


def _dil_merge_bwd(dout, os_, lses):
    S = dout.shape[0]
    tm = _tile(S, 256, 8)

    def body(d_ref, o0, o1, o2, l0, l1, l2, do0, do1, do2, dd0, dd1, dd2):
        dv = d_ref[...]
        os3 = [o0[...], o1[...], o2[...]]
        ls = [l0[...], l1[...], l2[...]]
        m = jnp.maximum(jnp.maximum(ls[0], ls[1]), ls[2])
        es = [jnp.exp(l - m) for l in ls]
        den = es[0] + es[1] + es[2]
        ws = [e / den for e in es]
        for h in range(HEADS):
            cols = slice(h * HEAD_DIM, (h + 1) * HEAD_DIM)
            dw = [jnp.sum(dv[:, cols] * o[:, cols], axis=-1, keepdims=True) for o in os3]
            wh = [w[:, cols] for w in ws]
            mean = wh[0] * dw[0] + wh[1] * dw[1] + wh[2] * dw[2]
            for p_, (do_ref, dd_ref) in enumerate(((do0, dd0), (do1, dd1), (do2, dd2))):
                do_p = wh[p_] * dv[:, cols]
                dlse = wh[p_] * (dw[p_] - mean)
                do_ref[:, cols] = do_p.astype(BF16)
                dd_ref[:, cols] = wh[p_] * dw[p_] - dlse

    ins = [('row', a, DIL_W, 0) for a in [dout] + list(os_) + list(lses)]
    outs = _rows("dil_merge_bwd", body, S, tm, ins, [(DIL_W, BF16)] * 3 + [(DIL_W, F32)] * 3)
    return outs[:3], outs[3:]


def _qkv_prep(proj, gq, gkv, tabs):
    S = proj.shape[0]
    tm = _tile(S, 256, 8)
    cos_a, sin_a, cos_p, sin_p = tabs

    def body(cq, ckv, qd, kd, vd, kr, gq_ref, gkv_ref, ca, sa, cp, sp, o_cq, o_ckv, o_qd, o_kd, o_vd, o_kr):
        x = cq[...]
        o_cq[...] = (x * _rms_scale(x) * gq_ref[...]).astype(BF16)
        x = ckv[...]
        o_ckv[...] = (x * _rms_scale(x) * gkv_ref[...]).astype(BF16)
        c, s = cp[...], sp[...]
        for h in range(HEADS):
            cols = slice(h * HEAD_DIM, (h + 1) * HEAD_DIM)
            o_qd[:, cols] = _rope(qd[:, cols], c, s, PART_ROPE // 2).astype(BF16)
            o_kd[:, cols] = _rope(kd[:, cols], c, s, PART_ROPE // 2).astype(BF16)
        o_vd[...] = vd[...].astype(BF16)
        o_kr[...] = _rope(kr[...], ca[...], sa[...], ROPE_DIM // 2).astype(BF16)

    ins = [('row', proj, Q_RANK, 0), ('row', proj, KV_RANK, 1), ('row', proj, DIL_W, 1), ('row', proj, DIL_W, 2),
           ('row', proj, DIL_W, 3), ('row', proj, LANE, 4 * DIL_W // LANE), ('full', gq), ('full', gkv),
           ('row', cos_a, LANE, 0), ('row', sin_a, LANE, 0), ('row', cos_p, LANE, 0), ('row', sin_p, LANE, 0)]
    return _rows("qkv_prep", body, S, tm, ins,
                 [(Q_RANK, BF16), (KV_RANK, BF16), (DIL_W, BF16), (DIL_W, BF16), (DIL_W, BF16), (LANE, BF16)])


def _qk_finish(qa, kv, kr, tabs):
    S = qa.shape[0]
    tm = _tile(S, 256, 8)
    cos_a, sin_a = tabs[0], tabs[1]

    def body(qa_ref, kn_ref, kr_ref, ca, sa, q_ref, k_ref):
        c, s = ca[...], sa[...]
        krv = kr_ref[...]
        for h in range(HEADS):
            nope = slice(h * QK_PAD, h * QK_PAD + HEAD_DIM)
            rope = slice(h * QK_PAD + HEAD_DIM, (h + 1) * QK_PAD)
            q_ref[:, nope] = qa_ref[:, nope].astype(BF16)
            q_ref[:, rope] = _rope(qa_ref[:, rope], c, s, ROPE_DIM // 2).astype(BF16)
            k_ref[:, nope] = kn_ref[:, h * HEAD_DIM:(h + 1) * HEAD_DIM]
            k_ref[:, rope] = krv

    W = HEADS * QK_PAD
    ins = [('row', qa, W, 0), ('row', kv, DIL_W, 0), ('row', kr, LANE, 0), ('row', cos_a, LANE, 0), ('row', sin_a, LANE, 0)]
    return _rows("qk_finish", body, S, tm, ins, [(W, BF16), (W, BF16)])


def _qk_finish_bwd(dq, dk, dv, tabs):
    S = dq.shape[0]
    tm = _tile(S, 256, 8)
    cos_a, sin_a = tabs[0], tabs[1]

    def body(dq_ref, dk_ref, dv_ref, ca, sa, dqa_ref, dkv_ref, dkr_ref):
        c, s = ca[...], sa[...]
        krsum = jnp.zeros((tm, LANE), F32)
        for h in range(HEADS):
            nope = slice(h * QK_PAD, h * QK_PAD + HEAD_DIM)
            rope = slice(h * QK_PAD + HEAD_DIM, (h + 1) * QK_PAD)
            dqa_ref[:, nope] = dq_ref[:, nope].astype(BF16)
            dqa_ref[:, rope] = _rope_t(dq_ref[:, rope], c, s, ROPE_DIM // 2).astype(BF16)
            dkv_ref[:, h * HEAD_DIM:(h + 1) * HEAD_DIM] = dk_ref[:, nope].astype(BF16)
            krsum = krsum + dk_ref[:, rope]
        dkv_ref[:, DIL_W:] = dv_ref[...]
        dkr_ref[...] = _rope_t(krsum, c, s, ROPE_DIM // 2)

    W = HEADS * QK_PAD
    ins = [('row', dq, W, 0), ('row', dk, W, 0), ('row', dv, DIL_W, 0), ('row', cos_a, LANE, 0), ('row', sin_a, LANE, 0)]
    return _rows("qk_finish_bwd", body, S, tm, ins, [(W, BF16), (2 * DIL_W, BF16), (LANE, F32)])


def _qkv_prep_bwd(proj, gq, gkv, d_cqn, d_ckvn, dqs, dks, dvs, dkr, tabs):
    S = proj.shape[0]
    tm = _tile(S, 256, 8)
    cos_p, sin_p = tabs[2], tabs[3]

    def body(cq, ckv, gq_ref, gkv_ref, dcq, dckv, dq0, dq1, dq2, dk0, dk1, dk2, dv0, dv1, dv2, dkr_ref, cp, sp,
             out_ref, dgq_ref, dgkv_ref):
        dx, dg = _rms_bwd(cq[...], gq_ref[...], dcq[...])
        out_ref[:, 0:Q_RANK] = dx.astype(BF16)
        _acc_add(dgq_ref, jnp.sum(dg, axis=0, keepdims=True))
        dx, dg = _rms_bwd(ckv[...], gkv_ref[...], dckv[...])
        out_ref[:, Q_RANK:Q_RANK + KV_RANK] = dx.astype(BF16)
        _acc_add(dgkv_ref, jnp.sum(dg, axis=0, keepdims=True))
        c, s = cp[...], sp[...]
        base = Q_RANK + KV_RANK
        for h in range(HEADS):
            cols = slice(h * HEAD_DIM, (h + 1) * HEAD_DIM)
            dst = lambda part: slice(base + part * DIL_W + h * HEAD_DIM, base + part * DIL_W + (h + 1) * HEAD_DIM)
            out_ref[:, dst(0)] = _rope_t(dq0[:, cols] + dq1[:, cols] + dq2[:, cols], c, s, PART_ROPE // 2).astype(BF16)
            out_ref[:, dst(1)] = _rope_t(dk0[:, cols] + dk1[:, cols] + dk2[:, cols], c, s, PART_ROPE // 2).astype(BF16)
            out_ref[:, dst(2)] = (dv0[:, cols] + dv1[:, cols] + dv2[:, cols]).astype(BF16)
        out_ref[:, base + 3 * DIL_W:] = dkr_ref[...].astype(BF16)

    ins = [('row', proj, Q_RANK, 0), ('row', proj, KV_RANK, 1), ('full', gq), ('full', gkv),
           ('row', d_cqn, Q_RANK, 0), ('row', d_ckvn, KV_RANK, 0)]
    ins += [('row', a, DIL_W, 0) for a in list(dqs) + list(dks) + list(dvs)]
    ins += [('row', dkr, LANE, 0), ('row', cos_p, LANE, 0), ('row', sin_p, LANE, 0)]
    return _rows("qkv_prep_bwd", body, S, tm, ins, [(IN_PAD, BF16)], [Q_RANK, KV_RANK])


def _mix_fwd(x, p, tabs, rider=None):
    h = _rms_cast("mix_prenorm", x, p['pre_g'])
    proj = _mm_nn("mix_proj", h, p['w_in'], F32)
    cqn, ckvn, qd, kd, vd, kr = _qkv_prep(proj, p['q_norm_g'], p['kv_norm_g'], tabs)
    qa = _mm_nn("mla_q_up", cqn, p['w_uq'], F32, tn=1024)
    kv = _mm_nn("mla_kv_up", ckvn, p['w_ukv'], BF16, tn=1024)
    q_cat, k_cat = _qk_finish(qa, kv, kr, tabs)
    (o_a, lse_a), rode = _mla_fwd(q_cat, k_cat, kv, rider)
    o_ps, lse_ps = [], []
    for _, d in DIL_PATTERNS:
        o_p, lse_p = _dil_fwd(qd, kd, vd, d)
        o_ps.append(o_p)
        lse_ps.append(lse_p)
    o_b = _dil_merge(o_ps, lse_ps)
    o_cat = jnp.stack([o_a, o_b])
    x_out, y = _chunk_post("mix_out", o_cat, p['w_o'].reshape(2, DIL_W, -1), x, p['post_g'], 1.0)
    return x_out, (x, h, proj, cqn, ckvn, qd, kd, vd, q_cat, k_cat, kv, o_a, lse_a, o_ps, lse_ps, o_cat, y), rode


def _mix_bwd(dxo, p, tabs, saved, rider=None):
    x, h, proj, cqn, ckvn, qd, kd, vd, q_cat, k_cat, kv, o_a, lse_a, o_ps, lse_ps, o_cat, y = saved
    dy, d_post = _postnorm_bwd("mix_postnorm_bwd", dxo, y, p['post_g'], 1.0)
    w_o = p['w_o']
    d_oa = _mm_nt("mix_do_a", dy, w_o[:DIL_W], BF16, tk=1024)
    d_ob = _mm_nt("mix_do_b", dy, w_o[DIL_W:], F32, tk=1024)
    d_wo = _mm_tn_chunks_a("mix_dwo", o_cat, dy, BF16).reshape(w_o.shape)
    stats = _mla_stats(d_oa, o_a, lse_a)
    (dq, dk, dv), rode = _mla_bwd(q_cat, k_cat, kv, d_oa, stats, rider)
    dqa, dkv, dkr = _qk_finish_bwd(dq, dk, dv, tabs)
    d_cqn = _mm_nt("mla_dcq", dqa, p['w_uq'], F32, tk=1024)
    d_ckvn = _mm_nt("mla_dckv", dkv, p['w_ukv'], F32, tk=1024)
    d_wuq = _mm_tn("mla_dwuq", cqn, dqa, BF16, tn=1024)
    d_wukv = _mm_tn("mla_dwukv", ckvn, dkv, BF16, tn=1024)
    do_ps, dd_ps = _dil_merge_bwd(d_ob, o_ps, lse_ps)
    dqs, dks, dvs = [], [], []
    for (_, d), do_p, lse_p, dd_p in zip(DIL_PATTERNS, do_ps, lse_ps, dd_ps):
        dq_p, dk_p, dv_p = _dil_bwd(qd, kd, vd, do_p, lse_p, dd_p, d)
        dqs.append(dq_p)
        dks.append(dk_p)
        dvs.append(dv_p)
    d_proj, d_gq, d_gkv = _qkv_prep_bwd(proj, p['q_norm_g'], p['kv_norm_g'], d_cqn, d_ckvn, dqs, dks, dvs, dkr, tabs)
    dh = _mm_nt("mix_dh", d_proj, p['w_in'], F32)
    d_win = _mm_tn("mix_dwin", h, d_proj, BF16)
    dx, d_pre = _prenorm_bwd("mix_prenorm_bwd", dh, x, p['pre_g'], dxo)
    return dx, dict(pre_g=d_pre, post_g=d_post, w_in=d_win, q_norm_g=d_gq, w_uq=d_wuq, kv_norm_g=d_gkv,
                    w_ukv=d_wukv, w_o=d_wo), rode


def _cols_from_shards(g):
    return jnp.transpose(g, (1, 0, 2)).reshape(g.shape[1], -1)


def _shards_from_cols(w):
    K = w.shape[0]
    return jnp.transpose(w.reshape(K, N_DEV, -1), (1, 0, 2))


def _win_layout(g):
    w = _cols_from_shards(g)
    a = Q_RANK + KV_RANK
    return jnp.concatenate([w[:, :a], w[:, a + ROPE_DIM:], w[:, a:a + ROPE_DIM],
                            jnp.zeros((w.shape[0], IN_PAD - IN_COLS), w.dtype)], axis=1)


def _win_unlayout(dw):
    a = Q_RANK + KV_RANK
    w = jnp.concatenate([dw[:, :a], dw[:, a + 3 * DIL_W:a + 3 * DIL_W + ROPE_DIM], dw[:, a:a + 3 * DIL_W]], axis=1)
    return _shards_from_cols(w)


def _wuq_layout(g):
    return _cols_from_shards(jnp.pad(g, ((0, 0), (0, 0), (0, QK_PAD - HEAD_DIM - ROPE_DIM))))


def _wuq_unlayout(dw):
    return _shards_from_cols(dw)[:, :, :HEAD_DIM + ROPE_DIM]


def _wukv_layout(g):
    return jnp.concatenate([_cols_from_shards(g[:, :, :HEAD_DIM]), _cols_from_shards(g[:, :, HEAD_DIM:])], axis=1)


def _wukv_unlayout(dw):
    return jnp.concatenate([_shards_from_cols(dw[:, :DIL_W]), _shards_from_cols(dw[:, DIL_W:])], axis=2)


def _cast_bf16(x):
    shp = x.shape
    x2 = x.reshape(-1, shp[-1])
    R, C = x2.shape
    tr = _tile(R, 512, 8)

    def body(x_ref, o_ref):
        o_ref[...] = x_ref[...].astype(BF16)

    spec = pl.BlockSpec((tr, C), lambda i: (i, 0))
    out = pl.pallas_call(body, name="cast_bf16", grid=(R // tr,), in_specs=[spec], out_specs=spec,
                         out_shape=jax.ShapeDtypeStruct((R, C), BF16), compiler_params=_params(("arbitrary",)))(x2)
    return out.reshape(shp)


def _mesh_pos():
    x, y, c = lax.axis_index("x"), lax.axis_index("y"), lax.axis_index("c")
    return x, y, c


class _Rider:
    def __init__(self, name, arrays, out_shapes, scratch, start, finish):
        self.name, self.arrays, self.out_shapes, self.scratch = name, arrays, out_shapes, scratch
        self.start, self.finish = start, finish
        self.n = len(arrays)

    def split(self, refs):
        return refs[:self.n], refs[self.n:2 * self.n], refs[2 * self.n:]


def _run_rider(rider):
    def body(*refs):
        rider.start(*rider.split(refs))
        rider.finish(*rider.split(refs))

    any_spec = pl.BlockSpec(memory_space=pl.ANY)
    return pl.pallas_call(body, name=rider.name, in_specs=[any_spec] * rider.n, out_specs=[any_spec] * rider.n,
                          out_shape=rider.out_shapes, scratch_shapes=rider.scratch)(*rider.arrays)


def _ride(rider, n_in, n_out, first, last, compute):
    def body(*refs):
        if rider is None:
            compute(refs[:n_in], refs[n_in:n_in + n_out])
            return
        r = rider.n
        mine = (refs[n_in:n_in + r], refs[n_in + r + n_out:n_in + 2 * r + n_out], refs[n_in + 2 * r + n_out:])

        @pl.when(first())
        def _():
            rider.start(*mine)

        compute(refs[:n_in], refs[n_in + r:n_in + r + n_out])

        @pl.when(last())
        def _():
            rider.finish(*mine)

    return body


def _gather_rider(xs):
    n = len(xs)

    def program(x_refs, o_refs, sems):
        send, recv, loc = sems
        x, y, c = _mesh_pos()
        me, sib = (x, y, c), (x, y, 1 - c)
        chips = [(1 - x, y), (x, 1 - y), (1 - x, 1 - y)]

        def slot(k, dev):
            return o_refs[k].at[4 * dev[0] + 2 * dev[1] + dev[2]]

        def copy(k, s, block, to, src=None):
            return pltpu.make_async_remote_copy(src_ref=slot(k, block) if src is None else src, dst_ref=slot(k, block),
                                                send_sem=send.at[k, s], recv_sem=recv.at[k, s],
                                                device_id=to, device_id_type=MESH)

        mine = [pltpu.make_async_copy(x_refs[k], slot(k, me), loc.at[k]) for k in range(n)]
        first = []
        for k in range(n):
            first.append(copy(k, 0, me, sib, src=x_refs[k]))
            first += [copy(k, 1 + j, me, (*chip, c), src=x_refs[k]) for j, chip in enumerate(chips)]

        def start():
            for cp in mine + first:
                cp.start()

        def finish():
            passed = []
            for k in range(n):
                for j, chip in enumerate(chips):
                    copy(k, 1 + j, (*chip, c), me).wait_recv()
                    fwd = copy(k, 4 + j, (*chip, c), sib)
                    fwd.start()
                    passed.append(fwd)
            for k in range(n):
                copy(k, 0, sib, me).wait_recv()
                for j, chip in enumerate(chips):
                    copy(k, 4 + j, (*chip, 1 - c), me).wait_recv()
            for cp in first + passed:
                cp.wait_send()
            for cp in mine:
                cp.wait()

        return start, finish

    return _Rider("all_gather", list(xs), [jax.ShapeDtypeStruct((N_DEV,) + a.shape, a.dtype) for a in xs],
                  [pltpu.SemaphoreType.DMA((n, 7)), pltpu.SemaphoreType.DMA((n, 7)), pltpu.SemaphoreType.DMA((n,))],
                  lambda *refs: program(*refs)[0](), lambda *refs: program(*refs)[1]())


def _exchange_rider(gs):
    n = len(gs)

    def program(g_refs, r_refs, sems):
        send, recv, loc = sems
        x, y, c = _mesh_pos()
        me = 4 * x + 2 * y + c
        mine = [pltpu.make_async_copy(g_refs[k].at[me], r_refs[k].at[me], loc.at[k]) for k in range(n)]
        out_cps, in_cps = [], []
        for k in range(n):
            for m in range(1, N_DEV):
                px = 1 - x if m & 4 else x
                py = 1 - y if m & 2 else y
                pc = 1 - c if m & 1 else c
                peer = 4 * px + 2 * py + pc
                kw = dict(src_ref=g_refs[k].at[peer], send_sem=send.at[k, m - 1], recv_sem=recv.at[k, m - 1],
                          device_id=(px, py, pc), device_id_type=MESH)
                out_cps.append(pltpu.make_async_remote_copy(dst_ref=r_refs[k].at[me], **kw))
                in_cps.append(pltpu.make_async_remote_copy(dst_ref=r_refs[k].at[peer], **kw))

        def start():
            for cp in mine + out_cps:
                cp.start()

        def finish():
            for cp in in_cps:
                cp.wait_recv()
            for cp in out_cps:
                cp.wait_send()
            for cp in mine:
                cp.wait()

        return start, finish

    return _Rider("grad_exchange", list(gs), [jax.ShapeDtypeStruct(a.shape, a.dtype) for a in gs],
                  [pltpu.SemaphoreType.DMA((n, 7)), pltpu.SemaphoreType.DMA((n, 7)), pltpu.SemaphoreType.DMA((n,))],
                  lambda *refs: program(*refs)[0](), lambda *refs: program(*refs)[1]())


def _all_reduce_small(v):
    R, C = v.shape

    def body(v_ref, o_ref, buf, send, recv):
        x, y, c = _mesh_pos()
        me = 4 * x + 2 * y + c
        buf[me] = v_ref[...]
        copies = []
        for m in range(1, N_DEV):
            px = 1 - x if m & 4 else x
            py = 1 - y if m & 2 else y
            pc = 1 - c if m & 1 else c
            peer = 4 * px + 2 * py + pc
            copies.append((pltpu.make_async_remote_copy(
                src_ref=v_ref, dst_ref=buf.at[me], send_sem=send.at[m - 1], recv_sem=recv.at[m - 1],
                device_id=(px, py, pc), device_id_type=MESH),
                pltpu.make_async_remote_copy(
                src_ref=v_ref, dst_ref=buf.at[peer], send_sem=send.at[m - 1], recv_sem=recv.at[m - 1],
                device_id=(px, py, pc), device_id_type=MESH)))
        for out_cp, _ in copies:
            out_cp.start()
        for _, in_cp in copies:
            in_cp.wait_recv()
        for out_cp, _ in copies:
            out_cp.wait_send()
        total = buf[0]
        for s in range(1, N_DEV):
            total = total + buf[s]
        o_ref[...] = total

    vmem = pl.BlockSpec(memory_space=pltpu.VMEM)
    return pl.pallas_call(
        body, name="all_reduce_small", in_specs=[vmem], out_specs=vmem, out_shape=jax.ShapeDtypeStruct((R, C), F32),
        scratch_shapes=[pltpu.VMEM((N_DEV, R, C), F32), pltpu.SemaphoreType.DMA((7,)), pltpu.SemaphoreType.DMA((7,))],
    )(v)


def _adamw(name, parts, w, m, v):
    P, R, C = parts.shape
    tr = _tile(R, 256, 16)

    def body(p_ref, w_ref, m_ref, v_ref, g_out, d_out, m_out, v_out):
        g = p_ref[0].astype(F32)
        for s in range(1, P):
            g = g + p_ref[s].astype(F32)
        m_new = ADAM_B1 * m_ref[...] + (1.0 - ADAM_B1) * g
        v_new = ADAM_B2 * v_ref[...] + (1.0 - ADAM_B2) * (g * g)
        m_hat = m_new / (1.0 - ADAM_B1 ** ADAM_STEP)
        v_hat = v_new / (1.0 - ADAM_B2 ** ADAM_STEP)
        g_out[...] = g
        d_out[...] = -ADAM_LR * (m_hat / (jnp.sqrt(v_hat) + ADAM_EPS) + ADAM_WD * w_ref[...])
        m_out[...] = m_new
        v_out[...] = v_new

    spec = pl.BlockSpec((tr, C), lambda i: (i, 0))
    shp = jax.ShapeDtypeStruct((R, C), F32)
    return pl.pallas_call(body, name=name, grid=(R // tr,),
                          in_specs=[pl.BlockSpec((P, tr, C), lambda i: (0, i, 0)), spec, spec, spec],
                          out_specs=[spec] * 4, out_shape=[shp] * 4, compiler_params=_params(("arbitrary",)))(parts, w, m, v)


def _rope_tables(positions):
    pos = positions.reshape(-1).astype(F32)[:, None]
    S = pos.shape[0]

    def cs(dim):
        inv = ROPE_THETA ** (-jnp.arange(0, dim, 2, dtype=F32) / dim)
        ang = pos * inv
        return jnp.cos(ang), jnp.sin(ang)

    ca, sa = cs(ROPE_DIM)
    cp, sp = cs(PART_ROPE)
    z = lambda w: jnp.zeros((S, w), F32)
    return (jnp.concatenate([ca, ca, z(LANE - ROPE_DIM)], axis=1), jnp.concatenate([sa, sa, z(LANE - ROPE_DIM)], axis=1),
            jnp.concatenate([cp, cp, jnp.ones((S, LANE - PART_ROPE), F32)], axis=1),
            jnp.concatenate([sp, sp, z(LANE - PART_ROPE)], axis=1))


def _layer_params(gathered, gains, l):
    g = gathered
    row = lambda n: gains[n][l][None, :]
    ffn = lambda t: dict(pre_g=row(t + '_pre_g'), post_g=row(t + '_post_g'), w_gate=g[t + '_w_gate'],
                         w_up=g[t + '_w_up'], w_down=g[t + '_w_down'])
    mix = dict(pre_g=row('mix_pre_g'), post_g=row('mix_post_g'), q_norm_g=row('mla_q_norm_g'),
               kv_norm_g=row('mla_kv_norm_g'), w_in=_win_layout(g['w_in']), w_uq=_wuq_layout(g['mla_w_uq']),
               w_ukv=_wukv_layout(g['mla_w_ukv']), w_o=g['w_o'].reshape(-1, g['w_o'].shape[-1]))
    return ffn('ffn1'), mix, ffn('ffn2')


def _grad_slots(d1, dm, d2):
    return [d1['w_gate'], d1['w_up'], d1['w_down'], _win_unlayout(dm['w_in']), _wuq_unlayout(dm['w_uq']),
            _wukv_unlayout(dm['w_ukv']), dm['w_o'].reshape(N_DEV, -1, dm['w_o'].shape[-1]),
            d2['w_gate'], d2['w_up'], d2['w_down']]


def _gain_grads(d1, dm, d2):
    return dict(ffn1_pre_g=d1['pre_g'], ffn1_post_g=d1['post_g'], mix_pre_g=dm['pre_g'], mix_post_g=dm['post_g'],
                mla_q_norm_g=dm['q_norm_g'], mla_kv_norm_g=dm['kv_norm_g'], ffn2_pre_g=d2['pre_g'], ffn2_post_g=d2['post_g'])


def _pack(vecs, width):
    flat = jnp.concatenate([v.reshape(-1) for v in vecs])
    per = 8 * width
    flat = jnp.pad(flat, (0, (-flat.shape[0]) % per))
    return flat.reshape(-1, width)


def _unpack(packed, shapes):
    flat = packed.reshape(-1)
    out, off = [], 0
    for shp in shapes:
        size = math.prod(shp)
        out.append(flat[off:off + size].reshape(shp))
        off += size
    return out


def kernel(x, positions, ffn1_pre_g, ffn1_post_g, ffn1_w_gate, ffn1_w_up, ffn1_w_down, mix_pre_g, mix_post_g, w_in, mla_q_norm_g, mla_w_uq, mla_kv_norm_g, mla_w_ukv, w_o, ffn2_pre_g, ffn2_post_g, ffn2_w_gate, ffn2_w_up, ffn2_w_down, loss_target, m_ffn1_pre_g, m_ffn1_post_g, m_ffn1_w_gate, m_ffn1_w_up, m_ffn1_w_down, m_mix_pre_g, m_mix_post_g, m_w_in, m_mla_q_norm_g, m_mla_w_uq, m_mla_kv_norm_g, m_mla_w_ukv, m_w_o, m_ffn2_pre_g, m_ffn2_post_g, m_ffn2_w_gate, m_ffn2_w_up, m_ffn2_w_down, v_ffn1_pre_g, v_ffn1_post_g, v_ffn1_w_gate, v_ffn1_w_up, v_ffn1_w_down, v_mix_pre_g, v_mix_post_g, v_w_in, v_mla_q_norm_g, v_mla_w_uq, v_mla_kv_norm_g, v_mla_w_ukv, v_w_o, v_ffn2_pre_g, v_ffn2_post_g, v_ffn2_w_gate, v_ffn2_w_up, v_ffn2_w_down):
    w = dict(zip(WNAMES, (ffn1_pre_g, ffn1_post_g, ffn1_w_gate, ffn1_w_up, ffn1_w_down, mix_pre_g, mix_post_g, w_in,
                          mla_q_norm_g, mla_w_uq, mla_kv_norm_g, mla_w_ukv, w_o, ffn2_pre_g, ffn2_post_g,
                          ffn2_w_gate, ffn2_w_up, ffn2_w_down)))
    mom = dict(zip(WNAMES, (m_ffn1_pre_g, m_ffn1_post_g, m_ffn1_w_gate, m_ffn1_w_up, m_ffn1_w_down, m_mix_pre_g,
                            m_mix_post_g, m_w_in, m_mla_q_norm_g, m_mla_w_uq, m_mla_kv_norm_g, m_mla_w_ukv, m_w_o,
                            m_ffn2_pre_g, m_ffn2_post_g, m_ffn2_w_gate, m_ffn2_w_up, m_ffn2_w_down)))
    var = dict(zip(WNAMES, (v_ffn1_pre_g, v_ffn1_post_g, v_ffn1_w_gate, v_ffn1_w_up, v_ffn1_w_down, v_mix_pre_g,
                            v_mix_post_g, v_w_in, v_mla_q_norm_g, v_mla_w_uq, v_mla_kv_norm_g, v_mla_w_ukv, v_w_o,
                            v_ffn2_pre_g, v_ffn2_post_g, v_ffn2_w_gate, v_ffn2_w_up, v_ffn2_w_down)))
    depth = w_in.shape[0]
    xs = x[0]
    D = xs.shape[1]
    tabs = _rope_tables(positions)

    shards = {n: _cast_bf16(w[n]) for n in BIG}
    gather = lambda l: _gather_rider([shards[n][l] for n in BIG])
    params = [None] * depth
    params[0] = _layer_params(dict(zip(BIG, _run_rider(gather(0)))), w, 0)

    saved = []
    act = xs
    for l in range(depth):
        p1, pm, p2 = params[l]
        act, s1 = _ffn_fwd(act, p1)
        act, sm, gathered = _mix_fwd(act, pm, tabs, gather(l + 1) if l + 1 < depth else None)
        if l + 1 < depth:
            params[l + 1] = _layer_params(dict(zip(BIG, gathered)), w, l + 1)
        act, s2 = _ffn_fwd(act, p2)
        saved.append((s1, sm, s2))
    dact, loss_part = _loss_head(act, loss_target[0])

    received, gain_parts = [None] * depth, [None] * depth
    pending = None
    for l in reversed(range(depth)):
        p1, pm, p2 = params[l]
        s1, sm, s2 = saved[l]
        dact, d2 = _ffn_bwd(dact, p2, s2)
        dact, dm, got = _mix_bwd(dact, pm, tabs, sm, None if pending is None else _exchange_rider(pending))
        if pending is not None:
            received[l + 1] = got
        dact, d1 = _ffn_bwd(dact, p1, s1)
        pending = _grad_slots(d1, dm, d2)
        gain_parts[l] = _gain_grads(d1, dm, d2)
    received[0] = _run_rider(_exchange_rider(pending))

    gain_local = [jnp.stack([gain_parts[l][n].reshape(-1) for l in range(depth)]) for n in GAINS]
    packed = _all_reduce_small(_pack(gain_local + [loss_part.reshape(1)], D))
    summed = _unpack(packed, [w[n].shape for n in GAINS] + [(1,)])
    loss = summed[-1][0]

    out = {}
    for i, n in enumerate(BIG):
        shp = w[n].shape
        res = []
        for l in range(depth):
            parts = received[l][i]
            res.append(_adamw("adamw_" + n, parts.reshape(N_DEV, -1, shp[-1]), w[n][l].reshape(-1, shp[-1]),
                              mom[n][l].reshape(-1, shp[-1]), var[n][l].reshape(-1, shp[-1])))
        out[n] = [jnp.stack([res[l][t] for l in range(depth)]).reshape(shp) for t in range(4)]
    pk = lambda d: _pack([d[n] for n in GAINS], D)
    g_pack = _pack(summed[:-1], D)
    res = _adamw("adamw_gains", g_pack[None], pk(w), pk(mom), pk(var))
    for t in range(4):
        for n, a in zip(GAINS, _unpack(res[t], [w[n].shape for n in GAINS])):
            out.setdefault(n, [None] * 4)[t] = a

    grads = [out[n][0] for n in WNAMES]
    deltas = [out[n][1] for n in WNAMES]
    new_m = [out[n][2] for n in WNAMES]
    new_v = [out[n][3] for n in WNAMES]
    return (loss, dact[None], *grads, *deltas, *new_m, *new_v)
```

```python
import functools
import math

import jax
import jax.numpy as jnp
from jax import lax
from jax.experimental import pallas as pl
from jax.experimental.pallas import tpu as pltpu

F32 = jnp.float32
BF16 = jnp.bfloat16
N_DEV = 8
MESH = pl.DeviceIdType.MESH

HEADS = 8
HEAD_DIM = 128
Q_RANK = 512
KV_RANK = 512
ROPE_DIM = 64
QK_PAD = 256
PART_ROPE = 32
DIL_PATTERNS = ((128, 1), (512, 4), (2048, 16))
ROPE_THETA = 500000.0
RMS_EPS = 1e-6
NEG = -1e30
LANE = 128
IN_COLS = 4160
IN_PAD = 4224
DIL_W = HEADS * HEAD_DIM

ADAM_LR, ADAM_B1, ADAM_B2, ADAM_EPS, ADAM_WD, ADAM_STEP = 0.001, 0.9, 0.999, 1e-08, 0.01, 10

VMEM_LIMIT = 56 * 1024 * 1024
SUB_ROWS = 256
ADAMW_TILE_ELEMS = 128 * 1024

WNAMES = ['ffn1_pre_g', 'ffn1_post_g', 'ffn1_w_gate', 'ffn1_w_up', 'ffn1_w_down', 'mix_pre_g', 'mix_post_g', 'w_in',
          'mla_q_norm_g', 'mla_w_uq', 'mla_kv_norm_g', 'mla_w_ukv', 'w_o', 'ffn2_pre_g', 'ffn2_post_g',
          'ffn2_w_gate', 'ffn2_w_up', 'ffn2_w_down']
BIG = ['ffn1_w_gate', 'ffn1_w_up', 'ffn1_w_down', 'w_in', 'mla_w_uq', 'mla_w_ukv', 'w_o',
       'ffn2_w_gate', 'ffn2_w_up', 'ffn2_w_down']
GAINS = [n for n in WNAMES if n not in BIG]

NT = (((1,), (1,)), ((), ()))
NN = (((1,), (0,)), ((), ()))
TN = (((0,), (0,)), ((), ()))


def _tile(n, target, mult):
    best = None
    t = mult
    while t <= min(n, target):
        if n % t == 0:
            best = t
        t += mult
    return n if best is None else best


def _params(sem=None):
    kw = dict(vmem_limit_bytes=VMEM_LIMIT)
    if sem is not None:
        kw['dimension_semantics'] = sem
    return pltpu.CompilerParams(**kw)


def _dot(a, b, dn):
    return lax.dot_general(a, b, dn, preferred_element_type=F32)


def _mm(name, pairs, pair_specs, dn, grid, k_axis, acc_shape, out_shapes, out_specs, epilogue,
        extras=(), extra_specs=(), rider=None):
    n_pair = len(pairs)
    nk = 1 if k_axis is None else grid[k_axis]

    def compute(ins, outs, scr):
        ab, ex = ins[:2 * n_pair], ins[2 * n_pair:]
        part = _dot(ab[0][...], ab[1][...], dn)
        for p in range(1, n_pair):
            part = part + _dot(ab[2 * p][...], ab[2 * p + 1][...], dn)
        if nk == 1:
            epilogue(part, ex, outs)
            return
        acc = scr[0]
        k = pl.program_id(k_axis)

        @pl.when(k == 0)
        def _():
            acc[...] = part

        @pl.when(k > 0)
        def _():
            acc[...] += part

        @pl.when(k == nk - 1)
        def _():
            epilogue(acc[...], ex, outs)

    flat, flat_specs = [], []
    for (a, b), (sa, sb) in zip(pairs, pair_specs):
        flat += [a, b]
        flat_specs += [sa, sb]
    outs, rode = _with_rider(rider, name, grid, compute, len(flat) + len(extras), flat_specs + list(extra_specs),
                             out_specs, out_shapes, flat + list(extras),
                             scratch=[pltpu.VMEM(acc_shape, F32)] if nk > 1 else [])
    return outs if rider is None else (outs, rode)


def _store(dtype):
    def epi(acc, ex, outs):
        outs[0][...] = acc.astype(dtype)
    return epi


def _mm_nn(name, a, b, out_dtype, tm=1024, tn=1408):
    M, K = a.shape
    N = b.shape[1]
    tm, tn = _tile(M, tm, 8), _tile(N, tn, LANE)
    return _mm(name, [(a, b)],
               [(pl.BlockSpec((tm, K), lambda j, i: (i, 0)), pl.BlockSpec((K, tn), lambda j, i: (0, j)))],
               NN, (N // tn, M // tm), None, None,
               [jax.ShapeDtypeStruct((M, N), out_dtype)], [pl.BlockSpec((tm, tn), lambda j, i: (i, j))],
               _store(out_dtype))[0]


def _mm_nt(name, a, b, out_dtype, tm=1024, tk=1408):
    M, K = a.shape
    N = b.shape[0]
    tm, tk = _tile(M, tm, 8), _tile(K, tk, LANE)
    return _mm(name, [(a, b)],
               [(pl.BlockSpec((tm, tk), lambda i, k: (i, k)), pl.BlockSpec((N, tk), lambda i, k: (0, k)))],
               NT, (M // tm, K // tk), 1, (tm, N),
               [jax.ShapeDtypeStruct((M, N), out_dtype)], [pl.BlockSpec((tm, N), lambda i, k: (i, 0))],
               _store(out_dtype))[0]


def _mm_tn(name, a, b, out_dtype, ts=1024, tn=1408):
    M, K = a.shape
    N = b.shape[1]
    ts, tn = _tile(M, ts, 16), _tile(N, tn, LANE)
    return _mm(name, [(a, b)],
               [(pl.BlockSpec((ts, K), lambda j, m: (m, 0)), pl.BlockSpec((ts, tn), lambda j, m: (m, j)))],
               TN, (N // tn, M // ts), 1, (K, tn),
               [jax.ShapeDtypeStruct((K, N), out_dtype)], [pl.BlockSpec((K, tn), lambda j, m: (0, j))],
               _store(out_dtype))[0]


def _mm_tn_chunks_a(name, a3, b, out_dtype, ts=2048):
    C, M, Kc = a3.shape
    N = b.shape[1]
    ts = _tile(M, ts, 16)
    return _mm(name, [(a3, b)],
               [(pl.BlockSpec((None, ts, Kc), lambda c, m: (c, m, 0)), pl.BlockSpec((ts, N), lambda c, m: (m, 0)))],
               TN, (C, M // ts), 1, (Kc, N),
               [jax.ShapeDtypeStruct((C, Kc, N), out_dtype)], [pl.BlockSpec((None, Kc, N), lambda c, m: (c, 0, 0))],
               _store(out_dtype))[0]


def _mm_tn_chunks_b(name, a, b3, out_dtype, ts=2048):
    M, K = a.shape
    C, _, Nc = b3.shape
    ts = _tile(M, ts, 16)
    return _mm(name, [(a, b3)],
               [(pl.BlockSpec((ts, K), lambda c, m: (m, 0)), pl.BlockSpec((None, ts, Nc), lambda c, m: (c, m, 0)))],
               TN, (C, M // ts), 1, (K, Nc),
               [jax.ShapeDtypeStruct((C, K, Nc), out_dtype)], [pl.BlockSpec((None, K, Nc), lambda c, m: (c, 0, 0))],
               _store(out_dtype))[0]


def _rows(name, body, n_rows, tm, ins, outs, accs=()):
    in_specs, arrays = [], []
    for spec in ins:
        if spec[0] == 'row':
            _, arr, width, cb = spec
            in_specs.append(pl.BlockSpec((tm, width), functools.partial(lambda i, cb: (i, cb), cb=cb)))
        else:
            arr = spec[1]
            in_specs.append(pl.BlockSpec(arr.shape, functools.partial(lambda i, nd: (0,) * nd, nd=arr.ndim)))
        arrays.append(arr)
    out_shapes = [jax.ShapeDtypeStruct((n_rows, w), dt) for w, dt in outs]
    out_specs = [pl.BlockSpec((tm, w), lambda i: (i, 0)) for w, _ in outs]
    out_shapes += [jax.ShapeDtypeStruct((1, w), F32) for w in accs]
    out_specs += [pl.BlockSpec((1, w), lambda i: (0, 0)) for w in accs]
    return pl.pallas_call(body, name=name, grid=(n_rows // tm,), in_specs=in_specs, out_specs=out_specs,
                          out_shape=out_shapes, compiler_params=_params(("arbitrary",)))(*arrays)


def _acc_add(ref, val):
    @pl.when(pl.program_id(0) == 0)
    def _():
        ref[...] = val

    @pl.when(pl.program_id(0) > 0)
    def _():
        ref[...] += val


def _rms_scale(x):
    return lax.rsqrt(jnp.mean(x * x, axis=-1, keepdims=True) + RMS_EPS)


def _rms_bwd(x, g, dy):
    r = _rms_scale(x)
    t = dy * g
    dx = r * t - x * (r * r * r) * jnp.mean(t * x, axis=-1, keepdims=True)
    return dx, dy * x * r


def _rot_half(x, hw):
    lane = lax.broadcasted_iota(jnp.int32, x.shape, 1)
    left = pltpu.roll(x, LANE - hw, 1)
    right = pltpu.roll(x, hw, 1)
    return jnp.where(lane < hw, -left, right)


def _rope(x, cos, sin, hw):
    return x * cos + _rot_half(x, hw) * sin


def _rope_t(dy, cos, sin, hw):
    return dy * cos - _rot_half(dy, hw) * sin


def _rms_cast(name, x, g):
    S, D = x.shape
    tm = _tile(S, 512, 8)

    def body(x_ref, g_ref, o_ref):
        xv = x_ref[...]
        o_ref[...] = (xv * _rms_scale(xv) * g_ref[...]).astype(BF16)

    return _rows(name, body, S, tm, [('row', x, D, 0), ('full', g)], [(D, BF16)])[0]


def _postnorm_bwd(name, dxo, y, g, coef):
    S, D = y.shape
    tm = _tile(S, 512, 8)

    def body(d_ref, y_ref, g_ref, dy_ref, dg_ref):
        dx, dg = _rms_bwd(y_ref[...], g_ref[...], coef * d_ref[...])
        dy_ref[...] = dx.astype(BF16)
        _acc_add(dg_ref, jnp.sum(dg, axis=0, keepdims=True))

    return _rows(name, body, S, tm, [('row', dxo, D, 0), ('row', y, D, 0), ('full', g)], [(D, BF16)], [D])


def _prenorm_bwd(name, dh, x, g, dxo):
    S, D = x.shape
    tm = _tile(S, 512, 8)

    def body(dh_ref, x_ref, g_ref, d_ref, dx_ref, dg_ref):
        dx, dg = _rms_bwd(x_ref[...], g_ref[...], dh_ref[...])
        dx_ref[...] = d_ref[...] + dx
        _acc_add(dg_ref, jnp.sum(dg, axis=0, keepdims=True))

    return _rows(name, body, S, tm, [('row', dh, D, 0), ('row', x, D, 0), ('full', g), ('row', dxo, D, 0)],
                 [(D, F32)], [D])


def _loss_head(y, target):
    S, D = y.shape
    tm = _tile(S, 512, 8)

    def body(y_ref, t_ref, dy_ref, l_ref):
        e = y_ref[...] - t_ref[...]
        dy_ref[...] = e * (1.0 / D)
        row = 0.5 * jnp.mean(e * e, axis=-1, keepdims=True)
        _acc_add(l_ref, jnp.broadcast_to(jnp.sum(row, axis=0, keepdims=True), (1, LANE)))

    dy, l = _rows("loss_head", body, S, tm, [('row', y, D, 0), ('row', target, D, 0)], [(D, F32)], [LANE])
    return dy, l[0, 0]


def _ffn_up(h, wg, wu):
    S, D = h.shape
    C, _, Fc = wg.shape
    tm = _tile(S, 1024, 8)

    sub = _tile(tm, SUB_ROWS, 8)

    def body(h_ref, wg_ref, wu_ref, g_ref, u_ref, a_ref):
        for r in range(tm // sub):
            rows = slice(r * sub, (r + 1) * sub)
            hv = h_ref[rows, :]
            g = _dot(hv, wg_ref[...], NN)
            u = _dot(hv, wu_ref[...], NN)
            g_ref[rows, :] = g.astype(BF16)
            u_ref[rows, :] = u.astype(BF16)
            a_ref[rows, :] = (g * jax.nn.sigmoid(g) * u).astype(BF16)

    w_spec = pl.BlockSpec((None, D, Fc), lambda j, i: (j, 0, 0))
    o_spec = pl.BlockSpec((None, tm, Fc), lambda j, i: (j, i, 0))
    shp = jax.ShapeDtypeStruct((C, S, Fc), BF16)
    return pl.pallas_call(body, name="ffn_up", grid=(C, S // tm),
                          in_specs=[pl.BlockSpec((tm, D), lambda j, i: (i, 0)), w_spec, w_spec],
                          out_specs=[o_spec, o_spec, o_spec], out_shape=[shp, shp, shp],
                          compiler_params=_params(("arbitrary", "arbitrary")))(h, wg, wu)


def _chunk_post(name, a3, w3, x, g, coef):
    C, S, Kc = a3.shape
    D = w3.shape[2]
    tm = _tile(S, 512, 8)

    def epi(acc, ex, outs):
        x_ref, g_ref = ex
        outs[0][...] = x_ref[...] + coef * (acc * _rms_scale(acc) * g_ref[...])
        outs[1][...] = acc

    row = pl.BlockSpec((tm, D), lambda i, c: (i, 0))
    shp = jax.ShapeDtypeStruct((S, D), F32)
    per = 2 if C % 2 == 0 else 1
    specs = [(pl.BlockSpec((None, tm, Kc), functools.partial(lambda i, c, o: (per * c + o, i, 0), o=o)),
              pl.BlockSpec((None, Kc, D), functools.partial(lambda i, c, o: (per * c + o, 0, 0), o=o))) for o in range(per)]
    return _mm(name, [(a3, w3)] * per, specs, NN, (S // tm, C // per), 1, (tm, D), [shp, shp], [row, row], epi,
               extras=[x, g], extra_specs=[row, pl.BlockSpec((1, D), lambda i, c: (0, 0))])


def _ffn_da(dy, wd, gate, up):
    S, D = dy.shape
    C, Fc, _ = wd.shape
    tm = _tile(S, 1024, 8)

    sub = _tile(tm, SUB_ROWS, 8)

    def body(dy_ref, wd_ref, g_ref, u_ref, dg_ref, du_ref):
        for r in range(tm // sub):
            rows = slice(r * sub, (r + 1) * sub)
            da = _dot(dy_ref[rows, :], wd_ref[...], NT)
            g = g_ref[rows, :].astype(F32)
            u = u_ref[rows, :].astype(F32)
            sig = jax.nn.sigmoid(g)
            dg_ref[rows, :] = (da * u * (sig * (1.0 + g * (1.0 - sig)))).astype(BF16)
            du_ref[rows, :] = (da * (g * sig)).astype(BF16)

    blk = pl.BlockSpec((None, tm, Fc), lambda c, i: (c, i, 0))
    shp = jax.ShapeDtypeStruct((C, S, Fc), BF16)
    return pl.pallas_call(body, name="ffn_da", grid=(C, S // tm),
                          in_specs=[pl.BlockSpec((tm, D), lambda c, i: (i, 0)),
                                    pl.BlockSpec((None, Fc, D), lambda c, i: (c, 0, 0)), blk, blk],
                          out_specs=[blk, blk], out_shape=[shp, shp],
                          compiler_params=_params(("arbitrary", "arbitrary")))(dy, wd, gate, up)


def _ffn_dh(dg, du, wg, wu, rider=None):
    C, S, Fc = dg.shape
    D = wg.shape[1]
    tm = _tile(S, 1024, 8)
    a_spec = pl.BlockSpec((None, tm, Fc), lambda i, c: (c, i, 0))
    w_spec = pl.BlockSpec((None, D, Fc), lambda i, c: (c, 0, 0))
    res = _mm("ffn_dh", [(dg, wg), (du, wu)], [(a_spec, w_spec), (a_spec, w_spec)], NT, (S // tm, C), 1, (tm, D),
              [jax.ShapeDtypeStruct((S, D), F32)], [pl.BlockSpec((tm, D), lambda i, c: (i, 0))], _store(F32),
              rider=rider)
    return (res[0], None) if rider is None else (res[0][0], res[1])


def _ffn_fwd(x, p):
    h = _rms_cast("ffn_prenorm", x, p['pre_g'])
    gate, up, act = _ffn_up(h, p['w_gate'], p['w_up'])
    x_out, y = _chunk_post("ffn_down", act, p['w_down'], x, p['post_g'], 0.5)
    return x_out, (x, h, gate, up, act, y)


def _ffn_bwd(dxo, p, saved, rider=None):
    x, h, gate, up, act, y = saved
    dy, d_post = _postnorm_bwd("ffn_postnorm_bwd", dxo, y, p['post_g'], 0.5)
    dgate, dup = _ffn_da(dy, p['w_down'], gate, up)
    d_wd = _mm_tn_chunks_a("ffn_dwd", act, dy, BF16)
    dh, rode = _ffn_dh(dgate, dup, p['w_gate'], p['w_up'], rider)
    d_wg = _mm_tn_chunks_b("ffn_dwg", h, dgate, BF16)
    d_wu = _mm_tn_chunks_b("ffn_dwu", h, dup, BF16)
    dx, d_pre = _prenorm_bwd("ffn_prenorm_bwd", dh, x, p['pre_g'], dxo)
    return dx, dict(pre_g=d_pre, post_g=d_post, w_gate=d_wg, w_up=d_wu, w_down=d_wd), rode


MLA_FWD_TQ, MLA_FWD_TK = 512, 512
MLA_BWD_TQ, MLA_BWD_TK = 512, 512
MLA_HEADS_PER_STEP = 2


def _causal_mask(tq, tk, off):
    r = lax.broadcasted_iota(jnp.int32, (tq, tk), 0)
    c = lax.broadcasted_iota(jnp.int32, (tq, tk), 1)
    return r + off >= c


def _grid_ends(grid):
    def all_at(targets):
        hit = pl.program_id(0) == targets[0]
        for ax in range(1, len(grid)):
            hit = jnp.logical_and(hit, pl.program_id(ax) == targets[ax])
        return hit

    return (lambda: all_at([0] * len(grid))), (lambda: all_at([g - 1 for g in grid]))


def _with_rider(rider, name, grid, compute, n_in, in_specs, out_specs, out_shape, arrays, scratch=()):
    any_spec = pl.BlockSpec(memory_space=pl.ANY)
    r = 0 if rider is None else rider.n
    outs = pl.pallas_call(
        _ride(rider, n_in, len(out_shape), len(scratch), *_grid_ends(grid), compute), name=name, grid=grid,
        in_specs=list(in_specs) + [any_spec] * r, out_specs=list(out_specs) + [any_spec] * r,
        out_shape=list(out_shape) + ([] if rider is None else rider.out_shapes),
        scratch_shapes=list(scratch) + ([] if rider is None else rider.scratch),
        compiler_params=_params(("arbitrary",) * len(grid)))(*arrays, *([] if rider is None else rider.arrays))
    return outs[:len(out_shape)], outs[len(out_shape):]


def _mla_fwd(q, k, kv, rider=None):
    S = q.shape[0]
    tq, tk = _tile(S, MLA_FWD_TQ, LANE), _tile(S, MLA_FWD_TK, LANE)
    nq = S // tq
    n_edge = max(1, tq // tk)
    scale = 1.0 / math.sqrt(HEAD_DIM + ROPE_DIM)

    hp = MLA_HEADS_PER_STEP

    def compute(ins, outs, _):
        q_ref, k_ref, v_ref = ins
        o_ref, lse_ref = outs
        i = pl.program_id(1)

        def step(j, carry, masked):
            rows = pl.ds(pl.multiple_of(j * tk, tk), tk)
            out = []
            for hh, (m, l, acc) in enumerate(carry):
                qk = slice(hh * QK_PAD, (hh + 1) * QK_PAD)
                s = _dot(q_ref[:, qk], k_ref[rows, qk], NT) * scale
                if masked:
                    s = jnp.where(_causal_mask(tq, tk, i * tq - j * tk), s, NEG)
                m_new = jnp.maximum(m, jnp.max(s, axis=-1, keepdims=True))
                alpha = jnp.exp(m - m_new)
                pr = jnp.exp(s - m_new)
                l = alpha * l + jnp.sum(pr, axis=-1, keepdims=True)
                acc = alpha * acc + _dot(pr.astype(BF16), v_ref[rows, hh * HEAD_DIM:(hh + 1) * HEAD_DIM], NN)
                out.append((m_new, l, acc))
            return tuple(out)

        init = tuple((jnp.full((tq, 1), NEG, F32), jnp.zeros((tq, 1), F32), jnp.zeros((tq, HEAD_DIM), F32))
                     for _ in range(hp))
        n_full = (i * tq) // tk
        carry = lax.fori_loop(0, n_full, lambda j, c: step(j, c, False), init)
        for e in range(n_edge):
            carry = step(n_full + e, carry, True)
        for hh, (m, l, acc) in enumerate(carry):
            o_ref[:, hh * HEAD_DIM:(hh + 1) * HEAD_DIM] = (acc / l).astype(BF16)
            lse_ref[hh] = jnp.broadcast_to(m + jnp.log(l), (tq, LANE))

    return _with_rider(
        rider, "mla_fwd", (HEADS // hp, nq), compute, 3,
        [pl.BlockSpec((tq, hp * QK_PAD), lambda h, i: (i, h)), pl.BlockSpec((S, hp * QK_PAD), lambda h, i: (0, h)),
         pl.BlockSpec((S, hp * HEAD_DIM), lambda h, i: (0, HEADS // hp + h))],
        [pl.BlockSpec((tq, hp * HEAD_DIM), lambda h, i: (i, h)), pl.BlockSpec((hp, tq, LANE), lambda h, i: (h, i, 0))],
        [jax.ShapeDtypeStruct((S, HEADS * HEAD_DIM), BF16), jax.ShapeDtypeStruct((HEADS, S, LANE), F32)], (q, k, kv))


def _mla_stats(do, o, lse):
    S = o.shape[0]
    t = _tile(S, 512, 8)

    def body(do_ref, o_ref, lse_ref, st_ref):
        delta = jnp.sum(do_ref[...].astype(F32) * o_ref[...].astype(F32), axis=-1, keepdims=True)
        lane = lax.broadcasted_iota(jnp.int32, (t, LANE), 1)
        st_ref[...] = jnp.where(lane < LANE // 2, lse_ref[...], jnp.broadcast_to(delta, (t, LANE)))

    blk = pl.BlockSpec((t, HEAD_DIM), lambda h, i: (i, h))
    st = pl.BlockSpec((None, t, LANE), lambda h, i: (h, i, 0))
    return pl.pallas_call(body, name="mla_stats", grid=(HEADS, S // t), in_specs=[blk, blk, st], out_specs=st,
                          out_shape=jax.ShapeDtypeStruct((HEADS, S, LANE), F32),
                          compiler_params=_params(("arbitrary", "arbitrary")))(do, o, lse)


def _mla_bwd(q, k, kv, do, stats, rider=None):
    S = q.shape[0]
    tq, tk = _tile(S, MLA_BWD_TQ, LANE), _tile(S, MLA_BWD_TK, LANE)
    nq, nk = S // tq, S // tk
    n_edge = max(1, tk // tq)
    scale = 1.0 / math.sqrt(HEAD_DIM + ROPE_DIM)
    hp = MLA_HEADS_PER_STEP

    def compute(ins, outs, _):
        q_ref, do_ref, st_ref, k_ref, v_ref = ins
        dq_ref, dk_ref, dv_ref = outs
        j = pl.program_id(1)

        @pl.when(j == 0)
        def _():
            dq_ref[...] = jnp.zeros_like(dq_ref)

        def step(i, carry, masked):
            rows = pl.ds(pl.multiple_of(i * tq, tq), tq)
            out = []
            for hh, (dk, dv) in enumerate(carry):
                qk = slice(hh * QK_PAD, (hh + 1) * QK_PAD)
                vo = slice(hh * HEAD_DIM, (hh + 1) * HEAD_DIM)
                qb, kb, dob = q_ref[rows, qk], k_ref[:, qk], do_ref[rows, vo]
                pr = jnp.exp(_dot(qb, kb, NT) * scale - st_ref[hh, rows, 0:1])
                if masked:
                    pr = jnp.where(_causal_mask(tq, tk, i * tq - j * tk), pr, 0.0)
                dv = dv + _dot(pr.astype(BF16), dob, TN)
                dp = _dot(dob, v_ref[:, vo], NT)
                ds = (pr * (dp - st_ref[hh, rows, LANE // 2:LANE // 2 + 1]) * scale).astype(BF16)
                dk = dk + _dot(ds, qb, TN)
                dq_ref[rows, qk] += _dot(ds, kb, NN)
                out.append((dk, dv))
            return tuple(out)

        carry = tuple((jnp.zeros((tk, QK_PAD), F32), jnp.zeros((tk, HEAD_DIM), F32)) for _ in range(hp))
        i_edge = (j * tk) // tq
        for e in range(n_edge):
            carry = step(i_edge + e, carry, True)
        carry = lax.fori_loop(i_edge + n_edge, nq, lambda i, c: step(i, c, False), carry)
        for hh, (dk, dv) in enumerate(carry):
            dk_ref[:, hh * QK_PAD:(hh + 1) * QK_PAD] = dk
            dv_ref[:, hh * HEAD_DIM:(hh + 1) * HEAD_DIM] = dv.astype(BF16)

    once = pl.Buffered(1)
    return _with_rider(
        rider, "mla_bwd", (HEADS // hp, nk), compute, 5,
        [pl.BlockSpec((S, hp * QK_PAD), lambda h, j: (0, h), pipeline_mode=once),
         pl.BlockSpec((S, hp * HEAD_DIM), lambda h, j: (0, h), pipeline_mode=once),
         pl.BlockSpec((hp, S, LANE), lambda h, j: (h, 0, 0), pipeline_mode=once),
         pl.BlockSpec((tk, hp * QK_PAD), lambda h, j: (j, h)),
         pl.BlockSpec((tk, hp * HEAD_DIM), lambda h, j: (j, HEADS // hp + h))],
        [pl.BlockSpec((S, hp * QK_PAD), lambda h, j: (0, h), pipeline_mode=once),
         pl.BlockSpec((tk, hp * QK_PAD), lambda h, j: (j, h)),
         pl.BlockSpec((tk, hp * HEAD_DIM), lambda h, j: (j, h))],
        [jax.ShapeDtypeStruct((S, HEADS * QK_PAD), F32), jax.ShapeDtypeStruct((S, HEADS * QK_PAD), F32),
         jax.ShapeDtypeStruct((S, HEADS * HEAD_DIM), BF16)], (q, do, stats, k, kv))


DIL_BLK = 128
DIL_HB = 2


def _dil_tiles(S, d):
    L = S // d
    tq = _tile(L, 4 * DIL_BLK, DIL_BLK)
    return L, tq, tq // DIL_BLK


def _dil_masks(absent):
    qi = lax.broadcasted_iota(jnp.int32, (DIL_BLK, DIL_BLK), 0)
    kj = lax.broadcasted_iota(jnp.int32, (DIL_BLK, DIL_BLK), 1)
    edge = kj >= qi + jnp.where(absent, DIL_BLK, 0)
    return edge, kj >= qi, kj <= qi


def _dil_fwd(q, k, v, d):
    S = q.shape[0]
    L, tq, nb = _dil_tiles(S, d)
    W = DIL_HB * HEAD_DIM
    scale = 1.0 / math.sqrt(HEAD_DIM)
    view = lambda a: a.reshape(L, d * DIL_W)

    def body(q_ref, k_ref, v_ref, kp_ref, vp_ref, o_ref, lse_ref):
        m_edge, m_prev, m_cur = _dil_masks(pl.program_id(0) == 0)
        for hb in range(DIL_HB):
            cols = slice(hb * HEAD_DIM, (hb + 1) * HEAD_DIM)
            for b in range(nb):
                rows = slice(b * DIL_BLK, (b + 1) * DIL_BLK)
                qb = q_ref[rows, cols]
                if b == 0:
                    kp, vp = kp_ref[:, cols], vp_ref[:, cols]
                    ok_prev = m_edge
                else:
                    prev = slice((b - 1) * DIL_BLK, b * DIL_BLK)
                    kp, vp = k_ref[prev, cols], v_ref[prev, cols]
                    ok_prev = m_prev
                s_p = jnp.where(ok_prev, _dot(qb, kp, NT) * scale, NEG)
                s_c = jnp.where(m_cur, _dot(qb, k_ref[rows, cols], NT) * scale, NEG)
                m = jnp.maximum(jnp.max(s_p, axis=-1, keepdims=True), jnp.max(s_c, axis=-1, keepdims=True))
                e_p, e_c = jnp.exp(s_p - m), jnp.exp(s_c - m)
                l = jnp.sum(e_p, axis=-1, keepdims=True) + jnp.sum(e_c, axis=-1, keepdims=True)
                lse = m + jnp.log(l)
                p_p, p_c = jnp.exp(s_p - lse).astype(BF16), jnp.exp(s_c - lse).astype(BF16)
                o_ref[rows, cols] = _dot(p_p, vp, NN) + _dot(p_c, v_ref[rows, cols], NN)
                lse_ref[rows, cols] = jnp.broadcast_to(lse, (DIL_BLK, HEAD_DIM))

    cur = pl.BlockSpec((tq, W), lambda n, cb: (n, cb))
    prv = pl.BlockSpec((DIL_BLK, W), lambda n, cb: (jnp.maximum(n * nb - 1, 0), cb))
    shp = jax.ShapeDtypeStruct((L, d * DIL_W), F32)
    o, lse = pl.pallas_call(body, name=f"dil_fwd_{d}", grid=(L // tq, d * HEADS // DIL_HB),
                            in_specs=[cur, cur, cur, prv, prv], out_specs=[cur, cur], out_shape=[shp, shp],
                            compiler_params=_params(("arbitrary", "arbitrary")))(view(q), view(k), view(v), view(k), view(v))
    return o.reshape(S, DIL_W), lse.reshape(S, DIL_W)


def _dil_bwd(q, k, v, do, lse, dd, d):
    S = q.shape[0]
    L, tq, nb = _dil_tiles(S, d)
    nt = L // tq
    W = DIL_HB * HEAD_DIM
    scale = 1.0 / math.sqrt(HEAD_DIM)
    view = lambda a: a.reshape(L, d * DIL_W)

    def body(q_ref, k_ref, v_ref, do_ref, lse_ref, dd_ref, kp_ref, vp_ref, qn_ref, don_ref, lsen_ref, ddn_ref,
             dq_ref, dk_ref, dv_ref):
        n = pl.program_id(0)
        no_prev, m_prev, m_cur = _dil_masks(n == 0)
        has_next = _dil_masks(n == nt - 1)[0]

        def pair(qb, kb, vb, dob, lse_b, dd_b, mask):
            pr = jnp.where(mask, jnp.exp(_dot(qb, kb, NT) * scale - lse_b), 0.0)
            ds = (pr * (_dot(dob, vb, NT) - dd_b) * scale).astype(BF16)
            return _dot(ds, kb, NN), _dot(ds, qb, TN), _dot(pr.astype(BF16), dob, TN)

        for hb in range(DIL_HB):
            cols = slice(hb * HEAD_DIM, (hb + 1) * HEAD_DIM)
            stat = slice(hb * HEAD_DIM, hb * HEAD_DIM + 1)
            dks, dvs = [], []
            for b in range(nb):
                rows = slice(b * DIL_BLK, (b + 1) * DIL_BLK)
                qb, dob = q_ref[rows, cols], do_ref[rows, cols]
                lse_b, dd_b = lse_ref[rows, stat], dd_ref[rows, stat]
                dq, dk, dv = pair(qb, k_ref[rows, cols], v_ref[rows, cols], dob, lse_b, dd_b, m_cur)
                if b == 0:
                    dq_p, _, _ = pair(qb, kp_ref[:, cols], vp_ref[:, cols], dob, lse_b, dd_b, no_prev)
                else:
                    prev = slice((b - 1) * DIL_BLK, b * DIL_BLK)
                    dq_p, dk_p, dv_p = pair(qb, k_ref[prev, cols], v_ref[prev, cols], dob, lse_b, dd_b, m_prev)
                    dks[b - 1] = dks[b - 1] + dk_p
                    dvs[b - 1] = dvs[b - 1] + dv_p
                dq_ref[rows, cols] = dq + dq_p
                dks.append(dk)
                dvs.append(dv)
            last = slice((nb - 1) * DIL_BLK, nb * DIL_BLK)
            _, dk_n, dv_n = pair(qn_ref[:, cols], k_ref[last, cols], v_ref[last, cols], don_ref[:, cols],
                                 lsen_ref[:, stat], ddn_ref[:, stat], has_next)
            dks[nb - 1] = dks[nb - 1] + dk_n
            dvs[nb - 1] = dvs[nb - 1] + dv_n
            for b in range(nb):
                rows = slice(b * DIL_BLK, (b + 1) * DIL_BLK)
                dk_ref[rows, cols] = dks[b]
                dv_ref[rows, cols] = dvs[b]

    cur = pl.BlockSpec((tq, W), lambda n, cb: (n, cb))
    prv = pl.BlockSpec((DIL_BLK, W), lambda n, cb: (jnp.maximum(n * nb - 1, 0), cb))
    nxt = pl.BlockSpec((DIL_BLK, W), lambda n, cb: (jnp.minimum((n + 1) * nb, L // DIL_BLK - 1), cb))
    shp = jax.ShapeDtypeStruct((L, d * DIL_W), F32)
    qv, kv_, vv, dov, lv, ddv = (view(a) for a in (q, k, v, do, lse, dd))
    outs = pl.pallas_call(body, name=f"dil_bwd_{d}", grid=(nt, d * HEADS // DIL_HB),
                          in_specs=[cur] * 6 + [prv, prv] + [nxt] * 4, out_specs=[cur] * 3, out_shape=[shp] * 3,
                          compiler_params=_params(("arbitrary", "arbitrary")))(
                              qv, kv_, vv, dov, lv, ddv, kv_, vv, qv, dov, lv, ddv)
    return [a.reshape(S, DIL_W) for a in outs]


def _dil_merge(os_, lses):
    S = os_[0].shape[0]
    tm = _tile(S, 256, 8)

    def body(o0, o1, o2, l0, l1, l2, out_ref):
        ls = [l0[...], l1[...], l2[...]]
        m = jnp.maximum(jnp.maximum(ls[0], ls[1]), ls[2])
        es = [jnp.exp(l - m) for l in ls]
        den = es[0] + es[1] + es[2]
        out_ref[...] = ((es[0] * o0[...] + es[1] * o1[...] + es[2] * o2[...]) / den).astype(BF16)

    ins = [('row', a, DIL_W, 0) for a in list(os_) + list(lses)]
    return _rows("dil_merge", body, S, tm, ins, [(DIL_W, BF16)])[0]


def _dil_merge_bwd(dout, os_, lses):
    S = dout.shape[0]
    tm = _tile(S, 256, 8)

    def body(d_ref, o0, o1, o2, l0, l1, l2, do0, do1, do2, dd0, dd1, dd2):
        dv = d_ref[...]
        os3 = [o0[...], o1[...], o2[...]]
        ls = [l0[...], l1[...], l2[...]]
        m = jnp.maximum(jnp.maximum(ls[0], ls[1]), ls[2])
        es = [jnp.exp(l - m) for l in ls]
        den = es[0] + es[1] + es[2]
        ws = [e / den for e in es]
        for h in range(HEADS):
            cols = slice(h * HEAD_DIM, (h + 1) * HEAD_DIM)
            dw = [jnp.sum(dv[:, cols] * o[:, cols], axis=-1, keepdims=True) for o in os3]
            wh = [w[:, cols] for w in ws]
            mean = wh[0] * dw[0] + wh[1] * dw[1] + wh[2] * dw[2]
            for p_, (do_ref, dd_ref) in enumerate(((do0, dd0), (do1, dd1), (do2, dd2))):
                do_p = wh[p_] * dv[:, cols]
                dlse = wh[p_] * (dw[p_] - mean)
                do_ref[:, cols] = do_p.astype(BF16)
                dd_ref[:, cols] = wh[p_] * dw[p_] - dlse

    ins = [('row', a, DIL_W, 0) for a in [dout] + list(os_) + list(lses)]
    outs = _rows("dil_merge_bwd", body, S, tm, ins, [(DIL_W, BF16)] * 3 + [(DIL_W, F32)] * 3)
    return outs[:3], outs[3:]


def _qkv_prep(proj, gq, gkv, tabs):
    S = proj.shape[0]
    tm = _tile(S, 256, 8)
    cos_a, sin_a, cos_p, sin_p = tabs

    def body(cq, ckv, qd, kd, vd, kr, gq_ref, gkv_ref, ca, sa, cp, sp, o_cq, o_ckv, o_qd, o_kd, o_vd, o_kr):
        x = cq[...]
        o_cq[...] = (x * _rms_scale(x) * gq_ref[...]).astype(BF16)
        x = ckv[...]
        o_ckv[...] = (x * _rms_scale(x) * gkv_ref[...]).astype(BF16)
        c, s = cp[...], sp[...]
        for h in range(HEADS):
            cols = slice(h * HEAD_DIM, (h + 1) * HEAD_DIM)
            o_qd[:, cols] = _rope(qd[:, cols], c, s, PART_ROPE // 2).astype(BF16)
            o_kd[:, cols] = _rope(kd[:, cols], c, s, PART_ROPE // 2).astype(BF16)
        o_vd[...] = vd[...].astype(BF16)
        o_kr[...] = _rope(kr[...], ca[...], sa[...], ROPE_DIM // 2).astype(BF16)

    ins = [('row', proj, Q_RANK, 0), ('row', proj, KV_RANK, 1), ('row', proj, DIL_W, 1), ('row', proj, DIL_W, 2),
           ('row', proj, DIL_W, 3), ('row', proj, LANE, 4 * DIL_W // LANE), ('full', gq), ('full', gkv),
           ('row', cos_a, LANE, 0), ('row', sin_a, LANE, 0), ('row', cos_p, LANE, 0), ('row', sin_p, LANE, 0)]
    return _rows("qkv_prep", body, S, tm, ins,
                 [(Q_RANK, BF16), (KV_RANK, BF16), (DIL_W, BF16), (DIL_W, BF16), (DIL_W, BF16), (LANE, BF16)])


def _qk_finish(qa, kv, kr, tabs):
    S = qa.shape[0]
    tm = _tile(S, 256, 8)
    cos_a, sin_a = tabs[0], tabs[1]

    def body(qa_ref, kn_ref, kr_ref, ca, sa, q_ref, k_ref):
        c, s = ca[...], sa[...]
        krv = kr_ref[...]
        for h in range(HEADS):
            nope = slice(h * QK_PAD, h * QK_PAD + HEAD_DIM)
            rope = slice(h * QK_PAD + HEAD_DIM, (h + 1) * QK_PAD)
            q_ref[:, nope] = qa_ref[:, nope].astype(BF16)
            q_ref[:, rope] = _rope(qa_ref[:, rope], c, s, ROPE_DIM // 2).astype(BF16)
            k_ref[:, nope] = kn_ref[:, h * HEAD_DIM:(h + 1) * HEAD_DIM]
            k_ref[:, rope] = krv

    W = HEADS * QK_PAD
    ins = [('row', qa, W, 0), ('row', kv, DIL_W, 0), ('row', kr, LANE, 0), ('row', cos_a, LANE, 0), ('row', sin_a, LANE, 0)]
    return _rows("qk_finish", body, S, tm, ins, [(W, BF16), (W, BF16)])


def _qk_finish_bwd(dq, dk, dv, tabs):
    S = dq.shape[0]
    tm = _tile(S, 256, 8)
    cos_a, sin_a = tabs[0], tabs[1]

    def body(dq_ref, dk_ref, dv_ref, ca, sa, dqa_ref, dkv_ref, dkr_ref):
        c, s = ca[...], sa[...]
        krsum = jnp.zeros((tm, LANE), F32)
        for h in range(HEADS):
            nope = slice(h * QK_PAD, h * QK_PAD + HEAD_DIM)
            rope = slice(h * QK_PAD + HEAD_DIM, (h + 1) * QK_PAD)
            dqa_ref[:, nope] = dq_ref[:, nope].astype(BF16)
            dqa_ref[:, rope] = _rope_t(dq_ref[:, rope], c, s, ROPE_DIM // 2).astype(BF16)
            dkv_ref[:, h * HEAD_DIM:(h + 1) * HEAD_DIM] = dk_ref[:, nope].astype(BF16)
            krsum = krsum + dk_ref[:, rope]
        dkv_ref[:, DIL_W:] = dv_ref[...]
        dkr_ref[...] = _rope_t(krsum, c, s, ROPE_DIM // 2)

    W = HEADS * QK_PAD
    ins = [('row', dq, W, 0), ('row', dk, W, 0), ('row', dv, DIL_W, 0), ('row', cos_a, LANE, 0), ('row', sin_a, LANE, 0)]
    return _rows("qk_finish_bwd", body, S, tm, ins, [(W, BF16), (2 * DIL_W, BF16), (LANE, F32)])


def _qkv_prep_bwd(proj, gq, gkv, d_cqn, d_ckvn, dqs, dks, dvs, dkr, tabs):
    S = proj.shape[0]
    tm = _tile(S, 256, 8)
    cos_p, sin_p = tabs[2], tabs[3]

    def body(cq, ckv, gq_ref, gkv_ref, dcq, dckv, dq0, dq1, dq2, dk0, dk1, dk2, dv0, dv1, dv2, dkr_ref, cp, sp,
             out_ref, dgq_ref, dgkv_ref):
        dx, dg = _rms_bwd(cq[...], gq_ref[...], dcq[...])
        out_ref[:, 0:Q_RANK] = dx.astype(BF16)
        _acc_add(dgq_ref, jnp.sum(dg, axis=0, keepdims=True))
        dx, dg = _rms_bwd(ckv[...], gkv_ref[...], dckv[...])
        out_ref[:, Q_RANK:Q_RANK + KV_RANK] = dx.astype(BF16)
        _acc_add(dgkv_ref, jnp.sum(dg, axis=0, keepdims=True))
        c, s = cp[...], sp[...]
        base = Q_RANK + KV_RANK
        for h in range(HEADS):
            cols = slice(h * HEAD_DIM, (h + 1) * HEAD_DIM)
            dst = lambda part: slice(base + part * DIL_W + h * HEAD_DIM, base + part * DIL_W + (h + 1) * HEAD_DIM)
            out_ref[:, dst(0)] = _rope_t(dq0[:, cols] + dq1[:, cols] + dq2[:, cols], c, s, PART_ROPE // 2).astype(BF16)
            out_ref[:, dst(1)] = _rope_t(dk0[:, cols] + dk1[:, cols] + dk2[:, cols], c, s, PART_ROPE // 2).astype(BF16)
            out_ref[:, dst(2)] = (dv0[:, cols] + dv1[:, cols] + dv2[:, cols]).astype(BF16)
        out_ref[:, base + 3 * DIL_W:] = dkr_ref[...].astype(BF16)

    ins = [('row', proj, Q_RANK, 0), ('row', proj, KV_RANK, 1), ('full', gq), ('full', gkv),
           ('row', d_cqn, Q_RANK, 0), ('row', d_ckvn, KV_RANK, 0)]
    ins += [('row', a, DIL_W, 0) for a in list(dqs) + list(dks) + list(dvs)]
    ins += [('row', dkr, LANE, 0), ('row', cos_p, LANE, 0), ('row', sin_p, LANE, 0)]
    return _rows("qkv_prep_bwd", body, S, tm, ins, [(IN_PAD, BF16)], [Q_RANK, KV_RANK])


def _mix_fwd(x, p, tabs, rider=None):
    h = _rms_cast("mix_prenorm", x, p['pre_g'])
    proj = _mm_nn("mix_proj", h, p['w_in'], F32)
    cqn, ckvn, qd, kd, vd, kr = _qkv_prep(proj, p['q_norm_g'], p['kv_norm_g'], tabs)
    qa = _mm_nn("mla_q_up", cqn, p['w_uq'], F32, tn=1024)
    kv = _mm_nn("mla_kv_up", ckvn, p['w_ukv'], BF16, tn=1024)
    q_cat, k_cat = _qk_finish(qa, kv, kr, tabs)
    (o_a, lse_a), rode = _mla_fwd(q_cat, k_cat, kv, rider)
    o_ps, lse_ps = [], []
    for _, d in DIL_PATTERNS:
        o_p, lse_p = _dil_fwd(qd, kd, vd, d)
        o_ps.append(o_p)
        lse_ps.append(lse_p)
    o_b = _dil_merge(o_ps, lse_ps)
    o_cat = jnp.stack([o_a, o_b])
    x_out, y = _chunk_post("mix_out", o_cat, p['w_o'].reshape(2, DIL_W, -1), x, p['post_g'], 1.0)
    return x_out, (x, h, proj, cqn, ckvn, qd, kd, vd, q_cat, k_cat, kv, o_a, lse_a, o_ps, lse_ps, o_cat, y), rode


def _mix_bwd(dxo, p, tabs, saved, rider=None):
    x, h, proj, cqn, ckvn, qd, kd, vd, q_cat, k_cat, kv, o_a, lse_a, o_ps, lse_ps, o_cat, y = saved
    dy, d_post = _postnorm_bwd("mix_postnorm_bwd", dxo, y, p['post_g'], 1.0)
    w_o = p['w_o']
    d_oa = _mm_nt("mix_do_a", dy, w_o[:DIL_W], BF16, tk=1024)
    d_ob = _mm_nt("mix_do_b", dy, w_o[DIL_W:], F32, tk=1024)
    d_wo = _mm_tn_chunks_a("mix_dwo", o_cat, dy, BF16).reshape(w_o.shape)
    stats = _mla_stats(d_oa, o_a, lse_a)
    (dq, dk, dv), rode = _mla_bwd(q_cat, k_cat, kv, d_oa, stats, rider)
    dqa, dkv, dkr = _qk_finish_bwd(dq, dk, dv, tabs)
    d_cqn = _mm_nt("mla_dcq", dqa, p['w_uq'], F32, tk=1024)
    d_ckvn = _mm_nt("mla_dckv", dkv, p['w_ukv'], F32, tk=1024)
    d_wuq = _mm_tn("mla_dwuq", cqn, dqa, BF16, tn=1024)
    d_wukv = _mm_tn("mla_dwukv", ckvn, dkv, BF16, tn=1024)
    do_ps, dd_ps = _dil_merge_bwd(d_ob, o_ps, lse_ps)
    dqs, dks, dvs = [], [], []
    for (_, d), do_p, lse_p, dd_p in zip(DIL_PATTERNS, do_ps, lse_ps, dd_ps):
        dq_p, dk_p, dv_p = _dil_bwd(qd, kd, vd, do_p, lse_p, dd_p, d)
        dqs.append(dq_p)
        dks.append(dk_p)
        dvs.append(dv_p)
    d_proj, d_gq, d_gkv = _qkv_prep_bwd(proj, p['q_norm_g'], p['kv_norm_g'], d_cqn, d_ckvn, dqs, dks, dvs, dkr, tabs)
    dh = _mm_nt("mix_dh", d_proj, p['w_in'], F32)
    d_win = _mm_tn("mix_dwin", h, d_proj, BF16)
    dx, d_pre = _prenorm_bwd("mix_prenorm_bwd", dh, x, p['pre_g'], dxo)
    return dx, dict(pre_g=d_pre, post_g=d_post, w_in=d_win, q_norm_g=d_gq, w_uq=d_wuq, kv_norm_g=d_gkv,
                    w_ukv=d_wukv, w_o=d_wo), rode


def _cols_from_shards(g):
    return jnp.transpose(g, (1, 0, 2)).reshape(g.shape[1], -1)


def _shards_from_cols(w):
    K = w.shape[0]
    return jnp.transpose(w.reshape(K, N_DEV, -1), (1, 0, 2))


def _win_layout(g):
    w = _cols_from_shards(g)
    a = Q_RANK + KV_RANK
    return jnp.concatenate([w[:, :a], w[:, a + ROPE_DIM:], w[:, a:a + ROPE_DIM],
                            jnp.zeros((w.shape[0], IN_PAD - IN_COLS), w.dtype)], axis=1)


def _win_unlayout(dw):
    a = Q_RANK + KV_RANK
    w = jnp.concatenate([dw[:, :a], dw[:, a + 3 * DIL_W:a + 3 * DIL_W + ROPE_DIM], dw[:, a:a + 3 * DIL_W]], axis=1)
    return _shards_from_cols(w)


def _wuq_layout(g):
    return _cols_from_shards(jnp.pad(g, ((0, 0), (0, 0), (0, QK_PAD - HEAD_DIM - ROPE_DIM))))


def _wuq_unlayout(dw):
    return _shards_from_cols(dw)[:, :, :HEAD_DIM + ROPE_DIM]


def _wukv_layout(g):
    return jnp.concatenate([_cols_from_shards(g[:, :, :HEAD_DIM]), _cols_from_shards(g[:, :, HEAD_DIM:])], axis=1)


def _wukv_unlayout(dw):
    return jnp.concatenate([_shards_from_cols(dw[:, :DIL_W]), _shards_from_cols(dw[:, DIL_W:])], axis=2)


def _cast_bf16(x):
    shp = x.shape
    x2 = x.reshape(-1, shp[-1])
    R, C = x2.shape
    tr = _tile(R, 512, 8)

    def body(x_ref, o_ref):
        o_ref[...] = x_ref[...].astype(BF16)

    spec = pl.BlockSpec((tr, C), lambda i: (i, 0))
    out = pl.pallas_call(body, name="cast_bf16", grid=(R // tr,), in_specs=[spec], out_specs=spec,
                         out_shape=jax.ShapeDtypeStruct((R, C), BF16), compiler_params=_params(("arbitrary",)))(x2)
    return out.reshape(shp)


def _mesh_pos():
    x, y, c = lax.axis_index("x"), lax.axis_index("y"), lax.axis_index("c")
    return x, y, c


class _Rider:
    def __init__(self, name, arrays, out_shapes, scratch, start, finish):
        self.name, self.arrays, self.out_shapes, self.scratch = name, arrays, out_shapes, scratch
        self.start, self.finish = start, finish
        self.n = len(arrays)

    def split(self, refs):
        return refs[:self.n], refs[self.n:2 * self.n], refs[2 * self.n:]


def _run_rider(rider):
    def body(*refs):
        rider.start(*rider.split(refs))
        rider.finish(*rider.split(refs))

    any_spec = pl.BlockSpec(memory_space=pl.ANY)
    return pl.pallas_call(body, name=rider.name, in_specs=[any_spec] * rider.n, out_specs=[any_spec] * rider.n,
                          out_shape=rider.out_shapes, scratch_shapes=rider.scratch)(*rider.arrays)


def _ride(rider, n_in, n_out, n_scr, first, last, compute):
    def body(*refs):
        r = 0 if rider is None else rider.n
        o0 = n_in + r
        s0 = o0 + n_out + r
        own = (refs[:n_in], refs[o0:o0 + n_out], refs[s0:s0 + n_scr])
        if rider is None:
            compute(*own)
            return
        mine = (refs[n_in:o0], refs[o0 + n_out:s0], refs[s0 + n_scr:])

        @pl.when(first())
        def _():
            rider.start(*mine)

        compute(*own)

        @pl.when(last())
        def _():
            rider.finish(*mine)

    return body


def _gather_rider(xs):
    n = len(xs)

    def program(x_refs, o_refs, sems):
        send, recv, loc = sems
        x, y, c = _mesh_pos()
        me, sib = (x, y, c), (x, y, 1 - c)
        chips = [(1 - x, y), (x, 1 - y), (1 - x, 1 - y)]

        def slot(k, dev):
            return o_refs[k].at[4 * dev[0] + 2 * dev[1] + dev[2]]

        def copy(k, s, block, to, src=None):
            return pltpu.make_async_remote_copy(src_ref=slot(k, block) if src is None else src, dst_ref=slot(k, block),
                                                send_sem=send.at[k, s], recv_sem=recv.at[k, s],
                                                device_id=to, device_id_type=MESH)

        mine = [pltpu.make_async_copy(x_refs[k], slot(k, me), loc.at[k]) for k in range(n)]
        first = []
        for k in range(n):
            first.append(copy(k, 0, me, sib, src=x_refs[k]))
            first += [copy(k, 1 + j, me, (*chip, c), src=x_refs[k]) for j, chip in enumerate(chips)]

        def start():
            for cp in mine + first:
                cp.start()

        def finish():
            passed = []
            for k in range(n):
                for j, chip in enumerate(chips):
                    copy(k, 1 + j, (*chip, c), me).wait_recv()
                    fwd = copy(k, 4 + j, (*chip, c), sib)
                    fwd.start()
                    passed.append(fwd)
            for k in range(n):
                copy(k, 0, sib, me).wait_recv()
                for j, chip in enumerate(chips):
                    copy(k, 4 + j, (*chip, 1 - c), me).wait_recv()
            for cp in first + passed:
                cp.wait_send()
            for cp in mine:
                cp.wait()

        return start, finish

    return _Rider("all_gather", list(xs), [jax.ShapeDtypeStruct((N_DEV,) + a.shape, a.dtype) for a in xs],
                  [pltpu.SemaphoreType.DMA((n, 7)), pltpu.SemaphoreType.DMA((n, 7)), pltpu.SemaphoreType.DMA((n,))],
                  lambda *refs: program(*refs)[0](), lambda *refs: program(*refs)[1]())


def _exchange_rider(gs):
    n = len(gs)

    def program(g_refs, r_refs, sems):
        send, recv, loc = sems
        x, y, c = _mesh_pos()
        me = 4 * x + 2 * y + c
        mine = [pltpu.make_async_copy(g_refs[k].at[me], r_refs[k].at[me], loc.at[k]) for k in range(n)]
        out_cps, in_cps = [], []
        for k in range(n):
            for m in range(1, N_DEV):
                px = 1 - x if m & 4 else x
                py = 1 - y if m & 2 else y
                pc = 1 - c if m & 1 else c
                peer = 4 * px + 2 * py + pc
                kw = dict(src_ref=g_refs[k].at[peer], send_sem=send.at[k, m - 1], recv_sem=recv.at[k, m - 1],
                          device_id=(px, py, pc), device_id_type=MESH)
                out_cps.append(pltpu.make_async_remote_copy(dst_ref=r_refs[k].at[me], **kw))
                in_cps.append(pltpu.make_async_remote_copy(dst_ref=r_refs[k].at[peer], **kw))

        def start():
            for cp in mine + out_cps:
                cp.start()

        def finish():
            for cp in in_cps:
                cp.wait_recv()
            for cp in out_cps:
                cp.wait_send()
            for cp in mine:
                cp.wait()

        return start, finish

    return _Rider("grad_exchange", list(gs), [jax.ShapeDtypeStruct(a.shape, a.dtype) for a in gs],
                  [pltpu.SemaphoreType.DMA((n, 7)), pltpu.SemaphoreType.DMA((n, 7)), pltpu.SemaphoreType.DMA((n,))],
                  lambda *refs: program(*refs)[0](), lambda *refs: program(*refs)[1]())


def _all_reduce_small(v):
    R, C = v.shape

    def body(v_ref, o_ref, buf, send, recv):
        x, y, c = _mesh_pos()
        me = 4 * x + 2 * y + c
        buf[me] = v_ref[...]
        copies = []
        for m in range(1, N_DEV):
            px = 1 - x if m & 4 else x
            py = 1 - y if m & 2 else y
            pc = 1 - c if m & 1 else c
            peer = 4 * px + 2 * py + pc
            copies.append((pltpu.make_async_remote_copy(
                src_ref=v_ref, dst_ref=buf.at[me], send_sem=send.at[m - 1], recv_sem=recv.at[m - 1],
                device_id=(px, py, pc), device_id_type=MESH),
                pltpu.make_async_remote_copy(
                src_ref=v_ref, dst_ref=buf.at[peer], send_sem=send.at[m - 1], recv_sem=recv.at[m - 1],
                device_id=(px, py, pc), device_id_type=MESH)))
        for out_cp, _ in copies:
            out_cp.start()
        for _, in_cp in copies:
            in_cp.wait_recv()
        for out_cp, _ in copies:
            out_cp.wait_send()
        total = buf[0]
        for s in range(1, N_DEV):
            total = total + buf[s]
        o_ref[...] = total

    vmem = pl.BlockSpec(memory_space=pltpu.VMEM)
    return pl.pallas_call(
        body, name="all_reduce_small", in_specs=[vmem], out_specs=vmem, out_shape=jax.ShapeDtypeStruct((R, C), F32),
        scratch_shapes=[pltpu.VMEM((N_DEV, R, C), F32), pltpu.SemaphoreType.DMA((7,)), pltpu.SemaphoreType.DMA((7,))],
    )(v)


def _adamw(name, parts, w, m, v):
    L, R, C = w.shape
    P = parts[0].shape[0]
    tr = _tile(R, max(16, ADAMW_TILE_ELEMS // C), 16)
    nr = R // tr

    def body(*refs):
        p_refs, (w_ref, m_ref, v_ref), (g_out, d_out, m_out, v_out) = refs[:L], refs[L:L + 3], refs[L + 3:]

        def update(p_ref):
            g = p_ref[0].astype(F32)
            for s in range(1, P):
                g = g + p_ref[s].astype(F32)
            m_new = ADAM_B1 * m_ref[...] + (1.0 - ADAM_B1) * g
            v_new = ADAM_B2 * v_ref[...] + (1.0 - ADAM_B2) * (g * g)
            m_hat = m_new / (1.0 - ADAM_B1 ** ADAM_STEP)
            v_hat = v_new / (1.0 - ADAM_B2 ** ADAM_STEP)
            g_out[...] = g
            d_out[...] = -ADAM_LR * (m_hat / (jnp.sqrt(v_hat) + ADAM_EPS) + ADAM_WD * w_ref[...])
            m_out[...] = m_new
            v_out[...] = v_new

        for ll in range(L):
            pl.when(pl.program_id(0) == ll)(functools.partial(update, p_refs[ll]))

    def part_spec(ll):
        def index(l, i):
            return (0, jnp.where(l == ll, i, jnp.where(l > ll, nr - 1, 0)), 0)
        return pl.BlockSpec((P, tr, C), index)

    spec = pl.BlockSpec((None, tr, C), lambda l, i: (l, i, 0))
    shp = jax.ShapeDtypeStruct((L, R, C), F32)
    return pl.pallas_call(body, name=name, grid=(L, nr),
                          in_specs=[part_spec(ll) for ll in range(L)] + [spec, spec, spec],
                          out_specs=[spec] * 4, out_shape=[shp] * 4,
                          compiler_params=_params(("arbitrary", "arbitrary")))(*parts, w, m, v)


def _rope_tables(positions):
    pos = positions.reshape(-1).astype(F32)[:, None]
    S = pos.shape[0]

    def cs(dim):
        inv = ROPE_THETA ** (-jnp.arange(0, dim, 2, dtype=F32) / dim)
        ang = pos * inv
        return jnp.cos(ang), jnp.sin(ang)

    ca, sa = cs(ROPE_DIM)
    cp, sp = cs(PART_ROPE)
    z = lambda w: jnp.zeros((S, w), F32)
    return (jnp.concatenate([ca, ca, z(LANE - ROPE_DIM)], axis=1), jnp.concatenate([sa, sa, z(LANE - ROPE_DIM)], axis=1),
            jnp.concatenate([cp, cp, jnp.ones((S, LANE - PART_ROPE), F32)], axis=1),
            jnp.concatenate([sp, sp, z(LANE - PART_ROPE)], axis=1))


def _layer_params(gathered, gains, l):
    g = gathered
    row = lambda n: gains[n][l][None, :]
    ffn = lambda t: dict(pre_g=row(t + '_pre_g'), post_g=row(t + '_post_g'), w_gate=g[t + '_w_gate'],
                         w_up=g[t + '_w_up'], w_down=g[t + '_w_down'])
    mix = dict(pre_g=row('mix_pre_g'), post_g=row('mix_post_g'), q_norm_g=row('mla_q_norm_g'),
               kv_norm_g=row('mla_kv_norm_g'), w_in=_win_layout(g['w_in']), w_uq=_wuq_layout(g['mla_w_uq']),
               w_ukv=_wukv_layout(g['mla_w_ukv']), w_o=g['w_o'].reshape(-1, g['w_o'].shape[-1]))
    return ffn('ffn1'), mix, ffn('ffn2')


def _gain_grads(d1, dm, d2):
    return dict(ffn1_pre_g=d1['pre_g'], ffn1_post_g=d1['post_g'], mix_pre_g=dm['pre_g'], mix_post_g=dm['post_g'],
                mla_q_norm_g=dm['q_norm_g'], mla_kv_norm_g=dm['kv_norm_g'], ffn2_pre_g=d2['pre_g'], ffn2_post_g=d2['post_g'])


def _pack(vecs, width):
    flat = jnp.concatenate([v.reshape(-1) for v in vecs])
    per = 8 * width
    flat = jnp.pad(flat, (0, (-flat.shape[0]) % per))
    return flat.reshape(-1, width)


def _unpack(packed, shapes):
    flat = packed.reshape(-1)
    out, off = [], 0
    for shp in shapes:
        size = math.prod(shp)
        out.append(flat[off:off + size].reshape(shp))
        off += size
    return out


def kernel(x, positions, ffn1_pre_g, ffn1_post_g, ffn1_w_gate, ffn1_w_up, ffn1_w_down, mix_pre_g, mix_post_g, w_in, mla_q_norm_g, mla_w_uq, mla_kv_norm_g, mla_w_ukv, w_o, ffn2_pre_g, ffn2_post_g, ffn2_w_gate, ffn2_w_up, ffn2_w_down, loss_target, m_ffn1_pre_g, m_ffn1_post_g, m_ffn1_w_gate, m_ffn1_w_up, m_ffn1_w_down, m_mix_pre_g, m_mix_post_g, m_w_in, m_mla_q_norm_g, m_mla_w_uq, m_mla_kv_norm_g, m_mla_w_ukv, m_w_o, m_ffn2_pre_g, m_ffn2_post_g, m_ffn2_w_gate, m_ffn2_w_up, m_ffn2_w_down, v_ffn1_pre_g, v_ffn1_post_g, v_ffn1_w_gate, v_ffn1_w_up, v_ffn1_w_down, v_mix_pre_g, v_mix_post_g, v_w_in, v_mla_q_norm_g, v_mla_w_uq, v_mla_kv_norm_g, v_mla_w_ukv, v_w_o, v_ffn2_pre_g, v_ffn2_post_g, v_ffn2_w_gate, v_ffn2_w_up, v_ffn2_w_down):
    w = dict(zip(WNAMES, (ffn1_pre_g, ffn1_post_g, ffn1_w_gate, ffn1_w_up, ffn1_w_down, mix_pre_g, mix_post_g, w_in,
                          mla_q_norm_g, mla_w_uq, mla_kv_norm_g, mla_w_ukv, w_o, ffn2_pre_g, ffn2_post_g,
                          ffn2_w_gate, ffn2_w_up, ffn2_w_down)))
    mom = dict(zip(WNAMES, (m_ffn1_pre_g, m_ffn1_post_g, m_ffn1_w_gate, m_ffn1_w_up, m_ffn1_w_down, m_mix_pre_g,
                            m_mix_post_g, m_w_in, m_mla_q_norm_g, m_mla_w_uq, m_mla_kv_norm_g, m_mla_w_ukv, m_w_o,
                            m_ffn2_pre_g, m_ffn2_post_g, m_ffn2_w_gate, m_ffn2_w_up, m_ffn2_w_down)))
    var = dict(zip(WNAMES, (v_ffn1_pre_g, v_ffn1_post_g, v_ffn1_w_gate, v_ffn1_w_up, v_ffn1_w_down, v_mix_pre_g,
                            v_mix_post_g, v_w_in, v_mla_q_norm_g, v_mla_w_uq, v_mla_kv_norm_g, v_mla_w_ukv, v_w_o,
                            v_ffn2_pre_g, v_ffn2_post_g, v_ffn2_w_gate, v_ffn2_w_up, v_ffn2_w_down)))
    depth = w_in.shape[0]
    xs = x[0]
    D = xs.shape[1]
    tabs = _rope_tables(positions)

    shards = {n: _cast_bf16(w[n]) for n in BIG}
    gather = lambda l: _gather_rider([shards[n][l] for n in BIG])
    params = [None] * depth
    params[0] = _layer_params(dict(zip(BIG, _run_rider(gather(0)))), w, 0)

    saved = []
    act = xs
    for l in range(depth):
        p1, pm, p2 = params[l]
        act, s1 = _ffn_fwd(act, p1)
        act, sm, gathered = _mix_fwd(act, pm, tabs, gather(l + 1) if l + 1 < depth else None)
        if l + 1 < depth:
            params[l + 1] = _layer_params(dict(zip(BIG, gathered)), w, l + 1)
        act, s2 = _ffn_fwd(act, p2)
        saved.append((s1, sm, s2))
    dact, loss_part = _loss_head(act, loss_target[0])

    received = [[None] * len(BIG) for _ in range(depth)]
    gain_parts = [None] * depth

    def riding(items):
        return _exchange_rider([a for _, _, a in items]) if items else None

    def keep(items, got):
        for (ll, i, _), r in zip(items, got or ()):
            received[ll][i] = r

    on_ffn, on_attn = [], []
    for l in reversed(range(depth)):
        p1, pm, p2 = params[l]
        s1, sm, s2 = saved[l]
        dact, d2, got = _ffn_bwd(dact, p2, s2, riding(on_ffn))
        keep(on_ffn, got)
        on_attn = [(l, 7, d2['w_gate']), (l, 8, d2['w_up']), (l, 9, d2['w_down'])] + on_attn
        dact, dm, got = _mix_bwd(dact, pm, tabs, sm, riding(on_attn))
        keep(on_attn, got)
        on_ffn = [(l, 3, _win_unlayout(dm['w_in'])), (l, 4, _wuq_unlayout(dm['w_uq'])),
                  (l, 5, _wukv_unlayout(dm['w_ukv'])), (l, 6, dm['w_o'].reshape(N_DEV, -1, dm['w_o'].shape[-1]))]
        dact, d1, got = _ffn_bwd(dact, p1, s1, riding(on_ffn))
        keep(on_ffn, got)
        on_ffn = [(l, 0, d1['w_gate']), (l, 1, d1['w_up'])]
        on_attn = [(l, 2, d1['w_down'])]
        gain_parts[l] = _gain_grads(d1, dm, d2)
    last = on_ffn + on_attn
    keep(last, _run_rider(riding(last)))

    gain_local = [jnp.stack([gain_parts[l][n].reshape(-1) for l in range(depth)]) for n in GAINS]
    packed = _all_reduce_small(_pack(gain_local + [loss_part.reshape(1)], D))
    summed = _unpack(packed, [w[n].shape for n in GAINS] + [(1,)])
    loss = summed[-1][0]

    out = {}
    for i, n in enumerate(BIG):
        out[n] = _adamw("adamw_" + n, [received[l][i] for l in range(depth)], w[n], mom[n], var[n])
    pk = lambda d: _pack([d[n] for n in GAINS], D)[None]
    res = _adamw("adamw_gains", [_pack(summed[:-1], D)[None]], pk(w), pk(mom), pk(var))
    for t in range(4):
        for n, a in zip(GAINS, _unpack(res[t], [w[n].shape for n in GAINS])):
            out.setdefault(n, [None] * 4)[t] = a

    grads = [out[n][0] for n in WNAMES]
    deltas = [out[n][1] for n in WNAMES]
    new_m = [out[n][2] for n in WNAMES]
    new_v = [out[n][3] for n in WNAMES]
    return (loss, dact[None], *grads, *deltas, *new_m, *new_v)
```

```python
import functools
import math

import jax
import jax.numpy as jnp
from jax import lax
from jax.experimental import pallas as pl
from jax.experimental.pallas import tpu as pltpu

F32 = jnp.float32
BF16 = jnp.bfloat16
N_DEV = 8
MESH = pl.DeviceIdType.MESH

HEADS = 8
HEAD_DIM = 128
Q_RANK = 512
KV_RANK = 512
ROPE_DIM = 64
QK_PAD = 256
PART_ROPE = 32
DIL_PATTERNS = ((128, 1), (512, 4), (2048, 16))
ROPE_THETA = 500000.0
RMS_EPS = 1e-6
NEG = -1e30
LANE = 128
IN_COLS = 4160
IN_PAD = 4224
DIL_W = HEADS * HEAD_DIM

ADAM_LR, ADAM_B1, ADAM_B2, ADAM_EPS, ADAM_WD, ADAM_STEP = 0.001, 0.9, 0.999, 1e-08, 0.01, 10

VMEM_LIMIT = 56 * 1024 * 1024
SUB_ROWS = 256
ADAMW_TILE_ELEMS = 128 * 1024

WNAMES = ['ffn1_pre_g', 'ffn1_post_g', 'ffn1_w_gate', 'ffn1_w_up', 'ffn1_w_down', 'mix_pre_g', 'mix_post_g', 'w_in',
          'mla_q_norm_g', 'mla_w_uq', 'mla_kv_norm_g', 'mla_w_ukv', 'w_o', 'ffn2_pre_g', 'ffn2_post_g',
          'ffn2_w_gate', 'ffn2_w_up', 'ffn2_w_down']
BIG = ['ffn1_w_gate', 'ffn1_w_up', 'ffn1_w_down', 'w_in', 'mla_w_uq', 'mla_w_ukv', 'w_o',
       'ffn2_w_gate', 'ffn2_w_up', 'ffn2_w_down']
GAINS = [n for n in WNAMES if n not in BIG]

NT = (((1,), (1,)), ((), ()))
NN = (((1,), (0,)), ((), ()))
TN = (((0,), (0,)), ((), ()))


def _tile(n, target, mult):
    best = None
    t = mult
    while t <= min(n, target):
        if n % t == 0:
            best = t
        t += mult
    return n if best is None else best


def _params(sem=None):
    kw = dict(vmem_limit_bytes=VMEM_LIMIT)
    if sem is not None:
        kw['dimension_semantics'] = sem
    return pltpu.CompilerParams(**kw)


def _dot(a, b, dn):
    return lax.dot_general(a, b, dn, preferred_element_type=F32)


def _mm(name, pairs, pair_specs, dn, grid, k_axis, acc_shape, out_shapes, out_specs, epilogue,
        extras=(), extra_specs=(), rider=None):
    n_pair = len(pairs)
    nk = 1 if k_axis is None else grid[k_axis]

    def compute(ins, outs, scr):
        ab, ex = ins[:2 * n_pair], ins[2 * n_pair:]
        part = _dot(ab[0][...], ab[1][...], dn)
        for p in range(1, n_pair):
            part = part + _dot(ab[2 * p][...], ab[2 * p + 1][...], dn)
        if nk == 1:
            epilogue(part, ex, outs)
            return
        acc = scr[0]
        k = pl.program_id(k_axis)

        @pl.when(k == 0)
        def _():
            acc[...] = part

        @pl.when(k > 0)
        def _():
            acc[...] += part

        @pl.when(k == nk - 1)
        def _():
            epilogue(acc[...], ex, outs)

    flat, flat_specs = [], []
    for (a, b), (sa, sb) in zip(pairs, pair_specs):
        flat += [a, b]
        flat_specs += [sa, sb]
    outs, rode = _with_rider(rider, name, grid, compute, len(flat) + len(extras), flat_specs + list(extra_specs),
                             out_specs, out_shapes, flat + list(extras),
                             scratch=[pltpu.VMEM(acc_shape, F32)] if nk > 1 else [])
    return outs if rider is None else (outs, rode)


def _store(dtype):
    def epi(acc, ex, outs):
        outs[0][...] = acc.astype(dtype)
    return epi


def _mm_nn(name, a, b, out_dtype, tm=1024, tn=1408):
    M, K = a.shape
    N = b.shape[1]
    tm, tn = _tile(M, tm, 8), _tile(N, tn, LANE)
    return _mm(name, [(a, b)],
               [(pl.BlockSpec((tm, K), lambda j, i: (i, 0)), pl.BlockSpec((K, tn), lambda j, i: (0, j)))],
               NN, (N // tn, M // tm), None, None,
               [jax.ShapeDtypeStruct((M, N), out_dtype)], [pl.BlockSpec((tm, tn), lambda j, i: (i, j))],
               _store(out_dtype))[0]


def _mm_nt(name, a, b, out_dtype, tm=1024, tk=1408):
    M, K = a.shape
    N = b.shape[0]
    tm, tk = _tile(M, tm, 8), _tile(K, tk, LANE)
    return _mm(name, [(a, b)],
               [(pl.BlockSpec((tm, tk), lambda i, k: (i, k)), pl.BlockSpec((N, tk), lambda i, k: (0, k)))],
               NT, (M // tm, K // tk), 1, (tm, N),
               [jax.ShapeDtypeStruct((M, N), out_dtype)], [pl.BlockSpec((tm, N), lambda i, k: (i, 0))],
               _store(out_dtype))[0]


def _mm_tn(name, a, b, out_dtype, ts=1024, tn=1408):
    M, K = a.shape
    N = b.shape[1]
    ts, tn = _tile(M, ts, 16), _tile(N, tn, LANE)
    return _mm(name, [(a, b)],
               [(pl.BlockSpec((ts, K), lambda j, m: (m, 0)), pl.BlockSpec((ts, tn), lambda j, m: (m, j)))],
               TN, (N // tn, M // ts), 1, (K, tn),
               [jax.ShapeDtypeStruct((K, N), out_dtype)], [pl.BlockSpec((K, tn), lambda j, m: (0, j))],
               _store(out_dtype))[0]


def _mm_tn_chunks_a(name, a3, b, out_dtype, ts=2048):
    C, M, Kc = a3.shape
    N = b.shape[1]
    ts = _tile(M, ts, 16)
    return _mm(name, [(a3, b)],
               [(pl.BlockSpec((None, ts, Kc), lambda c, m: (c, m, 0)), pl.BlockSpec((ts, N), lambda c, m: (m, 0)))],
               TN, (C, M // ts), 1, (Kc, N),
               [jax.ShapeDtypeStruct((C, Kc, N), out_dtype)], [pl.BlockSpec((None, Kc, N), lambda c, m: (c, 0, 0))],
               _store(out_dtype))[0]


def _mm_tn_chunks_b(name, a, b3, out_dtype, ts=2048):
    M, K = a.shape
    C, _, Nc = b3.shape
    ts = _tile(M, ts, 16)
    return _mm(name, [(a, b3)],
               [(pl.BlockSpec((ts, K), lambda c, m: (m, 0)), pl.BlockSpec((None, ts, Nc), lambda c, m: (c, m, 0)))],
               TN, (C, M // ts), 1, (K, Nc),
               [jax.ShapeDtypeStruct((C, K, Nc), out_dtype)], [pl.BlockSpec((None, K, Nc), lambda c, m: (c, 0, 0))],
               _store(out_dtype))[0]


def _rows(name, body, n_rows, tm, ins, outs, accs=()):
    in_specs, arrays = [], []
    for spec in ins:
        if spec[0] == 'row':
            _, arr, width, cb = spec
            in_specs.append(pl.BlockSpec((tm, width), functools.partial(lambda i, cb: (i, cb), cb=cb)))
        else:
            arr = spec[1]
            in_specs.append(pl.BlockSpec(arr.shape, functools.partial(lambda i, nd: (0,) * nd, nd=arr.ndim)))
        arrays.append(arr)
    out_shapes = [jax.ShapeDtypeStruct((n_rows, w), dt) for w, dt in outs]
    out_specs = [pl.BlockSpec((tm, w), lambda i: (i, 0)) for w, _ in outs]
    out_shapes += [jax.ShapeDtypeStruct((1, w), F32) for w in accs]
    out_specs += [pl.BlockSpec((1, w), lambda i: (0, 0)) for w in accs]
    return pl.pallas_call(body, name=name, grid=(n_rows // tm,), in_specs=in_specs, out_specs=out_specs,
                          out_shape=out_shapes, compiler_params=_params(("arbitrary",)))(*arrays)


def _acc_add(ref, val):
    @pl.when(pl.program_id(0) == 0)
    def _():
        ref[...] = val

    @pl.when(pl.program_id(0) > 0)
    def _():
        ref[...] += val


def _rms_scale(x):
    return lax.rsqrt(jnp.mean(x * x, axis=-1, keepdims=True) + RMS_EPS)


def _rms_bwd(x, g, dy):
    r = _rms_scale(x)
    t = dy * g
    dx = r * t - x * (r * r * r) * jnp.mean(t * x, axis=-1, keepdims=True)
    return dx, dy * x * r


def _rot_half(x, hw):
    lane = lax.broadcasted_iota(jnp.int32, x.shape, 1)
    left = pltpu.roll(x, LANE - hw, 1)
    right = pltpu.roll(x, hw, 1)
    return jnp.where(lane < hw, -left, right)


def _rope(x, cos, sin, hw):
    return x * cos + _rot_half(x, hw) * sin


def _rope_t(dy, cos, sin, hw):
    return dy * cos - _rot_half(dy, hw) * sin


def _rms_cast(name, x, g):
    S, D = x.shape
    tm = _tile(S, 512, 8)

    def body(x_ref, g_ref, o_ref):
        xv = x_ref[...]
        o_ref[...] = (xv * _rms_scale(xv) * g_ref[...]).astype(BF16)

    return _rows(name, body, S, tm, [('row', x, D, 0), ('full', g)], [(D, BF16)])[0]


def _postnorm_bwd(name, dxo, y, g, coef):
    S, D = y.shape
    tm = _tile(S, 512, 8)

    def body(d_ref, y_ref, g_ref, dy_ref, dg_ref):
        dx, dg = _rms_bwd(y_ref[...], g_ref[...], coef * d_ref[...])
        dy_ref[...] = dx.astype(BF16)
        _acc_add(dg_ref, jnp.sum(dg, axis=0, keepdims=True))

    return _rows(name, body, S, tm, [('row', dxo, D, 0), ('row', y, D, 0), ('full', g)], [(D, BF16)], [D])


def _prenorm_bwd(name, dh, x, g, dxo):
    S, D = x.shape
    tm = _tile(S, 512, 8)

    def body(dh_ref, x_ref, g_ref, d_ref, dx_ref, dg_ref):
        dx, dg = _rms_bwd(x_ref[...], g_ref[...], dh_ref[...])
        dx_ref[...] = d_ref[...] + dx
        _acc_add(dg_ref, jnp.sum(dg, axis=0, keepdims=True))

    return _rows(name, body, S, tm, [('row', dh, D, 0), ('row', x, D, 0), ('full', g), ('row', dxo, D, 0)],
                 [(D, F32)], [D])


def _loss_head(y, target):
    S, D = y.shape
    tm = _tile(S, 512, 8)

    def body(y_ref, t_ref, dy_ref, l_ref):
        e = y_ref[...] - t_ref[...]
        dy_ref[...] = e * (1.0 / D)
        row = 0.5 * jnp.mean(e * e, axis=-1, keepdims=True)
        _acc_add(l_ref, jnp.broadcast_to(jnp.sum(row, axis=0, keepdims=True), (1, LANE)))

    dy, l = _rows("loss_head", body, S, tm, [('row', y, D, 0), ('row', target, D, 0)], [(D, F32)], [LANE])
    return dy, l[0, 0]


def _ffn_up(h, wg, wu, rider=None):
    S, D = h.shape
    C, _, Fc = wg.shape
    tm = _tile(S, 1024, 8)

    sub = _tile(tm, SUB_ROWS, 8)

    def compute(ins, outs, _):
        h_ref, wg_ref, wu_ref = ins
        g_ref, u_ref, a_ref = outs
        for r in range(tm // sub):
            rows = slice(r * sub, (r + 1) * sub)
            hv = h_ref[rows, :]
            g = _dot(hv, wg_ref[...], NN)
            u = _dot(hv, wu_ref[...], NN)
            g_ref[rows, :] = g.astype(BF16)
            u_ref[rows, :] = u.astype(BF16)
            a_ref[rows, :] = (g * jax.nn.sigmoid(g) * u).astype(BF16)

    w_spec = pl.BlockSpec((None, D, Fc), lambda j, i: (j, 0, 0))
    o_spec = pl.BlockSpec((None, tm, Fc), lambda j, i: (j, i, 0))
    shp = jax.ShapeDtypeStruct((C, S, Fc), BF16)
    return _with_rider(rider, "ffn_up", (C, S // tm), compute, 3,
                       [pl.BlockSpec((tm, D), lambda j, i: (i, 0)), w_spec, w_spec],
                       [o_spec, o_spec, o_spec], [shp, shp, shp], (h, wg, wu))


def _chunk_post(name, a3, w3, x, g, coef):
    C, S, Kc = a3.shape
    D = w3.shape[2]
    tm = _tile(S, 512, 8)

    def epi(acc, ex, outs):
        x_ref, g_ref = ex
        outs[0][...] = x_ref[...] + coef * (acc * _rms_scale(acc) * g_ref[...])
        outs[1][...] = acc

    row = pl.BlockSpec((tm, D), lambda i, c: (i, 0))
    shp = jax.ShapeDtypeStruct((S, D), F32)
    per = 2 if C % 2 == 0 else 1
    specs = [(pl.BlockSpec((None, tm, Kc), functools.partial(lambda i, c, o: (per * c + o, i, 0), o=o)),
              pl.BlockSpec((None, Kc, D), functools.partial(lambda i, c, o: (per * c + o, 0, 0), o=o))) for o in range(per)]
    return _mm(name, [(a3, w3)] * per, specs, NN, (S // tm, C // per), 1, (tm, D), [shp, shp], [row, row], epi,
               extras=[x, g], extra_specs=[row, pl.BlockSpec((1, D), lambda i, c: (0, 0))])


def _ffn_da(dy, wd, gate, up):
    S, D = dy.shape
    C, Fc, _ = wd.shape
    tm = _tile(S, 1024, 8)

    sub = _tile(tm, SUB_ROWS, 8)

    def body(dy_ref, wd_ref, g_ref, u_ref, dg_ref, du_ref):
        for r in range(tm // sub):
            rows = slice(r * sub, (r + 1) * sub)
            da = _dot(dy_ref[rows, :], wd_ref[...], NT)
            g = g_ref[rows, :].astype(F32)
            u = u_ref[rows, :].astype(F32)
            sig = jax.nn.sigmoid(g)
            dg_ref[rows, :] = (da * u * (sig * (1.0 + g * (1.0 - sig)))).astype(BF16)
            du_ref[rows, :] = (da * (g * sig)).astype(BF16)

    blk = pl.BlockSpec((None, tm, Fc), lambda c, i: (c, i, 0))
    shp = jax.ShapeDtypeStruct((C, S, Fc), BF16)
    return pl.pallas_call(body, name="ffn_da", grid=(C, S // tm),
                          in_specs=[pl.BlockSpec((tm, D), lambda c, i: (i, 0)),
                                    pl.BlockSpec((None, Fc, D), lambda c, i: (c, 0, 0)), blk, blk],
                          out_specs=[blk, blk], out_shape=[shp, shp],
                          compiler_params=_params(("arbitrary", "arbitrary")))(dy, wd, gate, up)


def _ffn_dh(dg, du, wg, wu, rider=None):
    C, S, Fc = dg.shape
    D = wg.shape[1]
    tm = _tile(S, 1024, 8)
    a_spec = pl.BlockSpec((None, tm, Fc), lambda i, c: (c, i, 0))
    w_spec = pl.BlockSpec((None, D, Fc), lambda i, c: (c, 0, 0))
    res = _mm("ffn_dh", [(dg, wg), (du, wu)], [(a_spec, w_spec), (a_spec, w_spec)], NT, (S // tm, C), 1, (tm, D),
              [jax.ShapeDtypeStruct((S, D), F32)], [pl.BlockSpec((tm, D), lambda i, c: (i, 0))], _store(F32),
              rider=rider)
    return (res[0], None) if rider is None else (res[0][0], res[1])


def _ffn_fwd(x, p, rider=None):
    h = _rms_cast("ffn_prenorm", x, p['pre_g'])
    (gate, up, act), rode = _ffn_up(h, p['w_gate'], p['w_up'], rider)
    x_out, y = _chunk_post("ffn_down", act, p['w_down'], x, p['post_g'], 0.5)
    return x_out, (x, h, gate, up, act, y), rode


def _ffn_bwd(dxo, p, saved, rider=None):
    x, h, gate, up, act, y = saved
    dy, d_post = _postnorm_bwd("ffn_postnorm_bwd", dxo, y, p['post_g'], 0.5)
    dgate, dup = _ffn_da(dy, p['w_down'], gate, up)
    d_wd = _mm_tn_chunks_a("ffn_dwd", act, dy, BF16)
    dh, rode = _ffn_dh(dgate, dup, p['w_gate'], p['w_up'], rider)
    d_wg = _mm_tn_chunks_b("ffn_dwg", h, dgate, BF16)
    d_wu = _mm_tn_chunks_b("ffn_dwu", h, dup, BF16)
    dx, d_pre = _prenorm_bwd("ffn_prenorm_bwd", dh, x, p['pre_g'], dxo)
    return dx, dict(pre_g=d_pre, post_g=d_post, w_gate=d_wg, w_up=d_wu, w_down=d_wd), rode


MLA_FWD_TQ, MLA_FWD_TK = 512, 512
MLA_BWD_TQ, MLA_BWD_TK = 512, 512
MLA_HEADS_PER_STEP = 2


def _causal_mask(tq, tk, off):
    r = lax.broadcasted_iota(jnp.int32, (tq, tk), 0)
    c = lax.broadcasted_iota(jnp.int32, (tq, tk), 1)
    return r + off >= c


def _grid_ends(grid):
    def all_at(targets):
        hit = pl.program_id(0) == targets[0]
        for ax in range(1, len(grid)):
            hit = jnp.logical_and(hit, pl.program_id(ax) == targets[ax])
        return hit

    return (lambda: all_at([0] * len(grid))), (lambda: all_at([g - 1 for g in grid]))


def _with_rider(rider, name, grid, compute, n_in, in_specs, out_specs, out_shape, arrays, scratch=()):
    any_spec = pl.BlockSpec(memory_space=pl.ANY)
    r = 0 if rider is None else rider.n
    outs = pl.pallas_call(
        _ride(rider, n_in, len(out_shape), len(scratch), *_grid_ends(grid), compute), name=name, grid=grid,
        in_specs=list(in_specs) + [any_spec] * r, out_specs=list(out_specs) + [any_spec] * r,
        out_shape=list(out_shape) + ([] if rider is None else rider.out_shapes),
        scratch_shapes=list(scratch) + ([] if rider is None else rider.scratch),
        compiler_params=_params(("arbitrary",) * len(grid)))(*arrays, *([] if rider is None else rider.arrays))
    return outs[:len(out_shape)], outs[len(out_shape):]


def _mla_fwd(q, k, kv, rider=None):
    S = q.shape[0]
    tq, tk = _tile(S, MLA_FWD_TQ, LANE), _tile(S, MLA_FWD_TK, LANE)
    nq = S // tq
    n_edge = max(1, tq // tk)
    scale = 1.0 / math.sqrt(HEAD_DIM + ROPE_DIM)

    hp = MLA_HEADS_PER_STEP

    def compute(ins, outs, _):
        q_ref, k_ref, v_ref = ins
        o_ref, lse_ref = outs
        i = pl.program_id(1)

        def step(j, carry, masked):
            rows = pl.ds(pl.multiple_of(j * tk, tk), tk)
            out = []
            for hh, (m, l, acc) in enumerate(carry):
                qk = slice(hh * QK_PAD, (hh + 1) * QK_PAD)
                s = _dot(q_ref[:, qk], k_ref[rows, qk], NT) * scale
                if masked:
                    s = jnp.where(_causal_mask(tq, tk, i * tq - j * tk), s, NEG)
                m_new = jnp.maximum(m, jnp.max(s, axis=-1, keepdims=True))
                alpha = jnp.exp(m - m_new)
                pr = jnp.exp(s - m_new)
                l = alpha * l + jnp.sum(pr, axis=-1, keepdims=True)
                acc = alpha * acc + _dot(pr.astype(BF16), v_ref[rows, hh * HEAD_DIM:(hh + 1) * HEAD_DIM], NN)
                out.append((m_new, l, acc))
            return tuple(out)

        init = tuple((jnp.full((tq, 1), NEG, F32), jnp.zeros((tq, 1), F32), jnp.zeros((tq, HEAD_DIM), F32))
                     for _ in range(hp))
        n_full = (i * tq) // tk
        carry = lax.fori_loop(0, n_full, lambda j, c: step(j, c, False), init)
        for e in range(n_edge):
            carry = step(n_full + e, carry, True)
        for hh, (m, l, acc) in enumerate(carry):
            o_ref[:, hh * HEAD_DIM:(hh + 1) * HEAD_DIM] = (acc / l).astype(BF16)
            lse_ref[hh] = jnp.broadcast_to(m + jnp.log(l), (tq, LANE))

    return _with_rider(
        rider, "mla_fwd", (HEADS // hp, nq), compute, 3,
        [pl.BlockSpec((tq, hp * QK_PAD), lambda h, i: (i, h)), pl.BlockSpec((S, hp * QK_PAD), lambda h, i: (0, h)),
         pl.BlockSpec((S, hp * HEAD_DIM), lambda h, i: (0, HEADS // hp + h))],
        [pl.BlockSpec((tq, hp * HEAD_DIM), lambda h, i: (i, h)), pl.BlockSpec((hp, tq, LANE), lambda h, i: (h, i, 0))],
        [jax.ShapeDtypeStruct((S, HEADS * HEAD_DIM), BF16), jax.ShapeDtypeStruct((HEADS, S, LANE), F32)], (q, k, kv))


def _mla_stats(do, o, lse):
    S = o.shape[0]
    t = _tile(S, 512, 8)

    def body(do_ref, o_ref, lse_ref, st_ref):
        delta = jnp.sum(do_ref[...].astype(F32) * o_ref[...].astype(F32), axis=-1, keepdims=True)
        lane = lax.broadcasted_iota(jnp.int32, (t, LANE), 1)
        st_ref[...] = jnp.where(lane < LANE // 2, lse_ref[...], jnp.broadcast_to(delta, (t, LANE)))

    blk = pl.BlockSpec((t, HEAD_DIM), lambda h, i: (i, h))
    st = pl.BlockSpec((None, t, LANE), lambda h, i: (h, i, 0))
    return pl.pallas_call(body, name="mla_stats", grid=(HEADS, S // t), in_specs=[blk, blk, st], out_specs=st,
                          out_shape=jax.ShapeDtypeStruct((HEADS, S, LANE), F32),
                          compiler_params=_params(("arbitrary", "arbitrary")))(do, o, lse)


def _mla_bwd(q, k, kv, do, stats, rider=None):
    S = q.shape[0]
    tq, tk = _tile(S, MLA_BWD_TQ, LANE), _tile(S, MLA_BWD_TK, LANE)
    nq, nk = S // tq, S // tk
    n_edge = max(1, tk // tq)
    scale = 1.0 / math.sqrt(HEAD_DIM + ROPE_DIM)
    hp = MLA_HEADS_PER_STEP

    def compute(ins, outs, _):
        q_ref, do_ref, st_ref, k_ref, v_ref = ins
        dq_ref, dk_ref, dv_ref = outs
        j = pl.program_id(1)

        @pl.when(j == 0)
        def _():
            dq_ref[...] = jnp.zeros_like(dq_ref)

        def step(i, carry, masked):
            rows = pl.ds(pl.multiple_of(i * tq, tq), tq)
            out = []
            for hh, (dk, dv) in enumerate(carry):
                qk = slice(hh * QK_PAD, (hh + 1) * QK_PAD)
                vo = slice(hh * HEAD_DIM, (hh + 1) * HEAD_DIM)
                qb, kb, dob = q_ref[rows, qk], k_ref[:, qk], do_ref[rows, vo]
                pr = jnp.exp(_dot(qb, kb, NT) * scale - st_ref[hh, rows, 0:1])
                if masked:
                    pr = jnp.where(_causal_mask(tq, tk, i * tq - j * tk), pr, 0.0)
                dv = dv + _dot(pr.astype(BF16), dob, TN)
                dp = _dot(dob, v_ref[:, vo], NT)
                ds = (pr * (dp - st_ref[hh, rows, LANE // 2:LANE // 2 + 1]) * scale).astype(BF16)
                dk = dk + _dot(ds, qb, TN)
                dq_ref[rows, qk] += _dot(ds, kb, NN)
                out.append((dk, dv))
            return tuple(out)

        carry = tuple((jnp.zeros((tk, QK_PAD), F32), jnp.zeros((tk, HEAD_DIM), F32)) for _ in range(hp))
        i_edge = (j * tk) // tq
        for e in range(n_edge):
            carry = step(i_edge + e, carry, True)
        carry = lax.fori_loop(i_edge + n_edge, nq, lambda i, c: step(i, c, False), carry)
        for hh, (dk, dv) in enumerate(carry):
            dk_ref[:, hh * QK_PAD:(hh + 1) * QK_PAD] = dk
            dv_ref[:, hh * HEAD_DIM:(hh + 1) * HEAD_DIM] = dv.astype(BF16)

    once = pl.Buffered(1)
    return _with_rider(
        rider, "mla_bwd", (HEADS // hp, nk), compute, 5,
        [pl.BlockSpec((S, hp * QK_PAD), lambda h, j: (0, h), pipeline_mode=once),
         pl.BlockSpec((S, hp * HEAD_DIM), lambda h, j: (0, h), pipeline_mode=once),
         pl.BlockSpec((hp, S, LANE), lambda h, j: (h, 0, 0), pipeline_mode=once),
         pl.BlockSpec((tk, hp * QK_PAD), lambda h, j: (j, h)),
         pl.BlockSpec((tk, hp * HEAD_DIM), lambda h, j: (j, HEADS // hp + h))],
        [pl.BlockSpec((S, hp * QK_PAD), lambda h, j: (0, h), pipeline_mode=once),
         pl.BlockSpec((tk, hp * QK_PAD), lambda h, j: (j, h)),
         pl.BlockSpec((tk, hp * HEAD_DIM), lambda h, j: (j, h))],
        [jax.ShapeDtypeStruct((S, HEADS * QK_PAD), F32), jax.ShapeDtypeStruct((S, HEADS * QK_PAD), F32),
         jax.ShapeDtypeStruct((S, HEADS * HEAD_DIM), BF16)], (q, do, stats, k, kv))


DIL_BLK = 128
DIL_HB = 1
DIL_HB_BWD = 1
DIL_ROWS = 2048

B_NT = (((2,), (2,)), ((0,), (0,)))
B_NN = (((2,), (1,)), ((0,), (0,)))
B_TN = (((1,), (1,)), ((0,), (0,)))


def _dil_geometry(S, d):
    seg = DIL_BLK * d
    tr = max(seg, min(S, DIL_ROWS))
    assert S % tr == 0 and tr % seg == 0
    return seg, tr, tr // seg


def _dil_blocks(ref, cols, d, seg, n_seg, dtype):
    return jnp.stack([ref[pl.ds(g * seg + r, DIL_BLK, stride=d), cols].astype(dtype)
                      for g in range(n_seg) for r in range(d)])


def _dil_unblocks(ref, cols, d, seg, n_seg, val):
    for g in range(n_seg):
        for r in range(d):
            ref[pl.ds(g * seg + r, DIL_BLK, stride=d), cols] = val[g * d + r]


def _dil_band(n_blocks, n_edge, absent):
    shape = (n_blocks, DIL_BLK, DIL_BLK)
    b = lax.broadcasted_iota(jnp.int32, shape, 0)
    qi = lax.broadcasted_iota(jnp.int32, shape, 1)
    kj = lax.broadcasted_iota(jnp.int32, shape, 2)
    gone = jnp.logical_and(b < n_edge, absent)
    return kj >= qi + jnp.where(gone, DIL_BLK, 0), kj <= qi


def _with_before(edge, cur, d):
    return edge if cur.shape[0] == d else jnp.concatenate([edge, cur[:cur.shape[0] - d]], axis=0)


def _dilated_fwd(q, k, v, d):
    S = q.shape[0]
    seg, tr, n_seg = _dil_geometry(S, d)
    nb = n_seg * d
    W = DIL_HB * HEAD_DIM
    scale = 1.0 / math.sqrt(HEAD_DIM)

    def body(q_ref, k_ref, v_ref, kp_ref, vp_ref, o_ref, lse_ref):
        m_prev, m_cur = _dil_band(nb, d, pl.program_id(0) == 0)
        for hb in range(DIL_HB):
            cols = slice(hb * HEAD_DIM, (hb + 1) * HEAD_DIM)
            qb = _dil_blocks(q_ref, cols, d, seg, n_seg, BF16)
            kb = _dil_blocks(k_ref, cols, d, seg, n_seg, BF16)
            vb = _dil_blocks(v_ref, cols, d, seg, n_seg, BF16)
            kp = _with_before(_dil_blocks(kp_ref, cols, d, seg, 1, BF16), kb, d)
            vp = _with_before(_dil_blocks(vp_ref, cols, d, seg, 1, BF16), vb, d)
            s_p = jnp.where(m_prev, _dot(qb, kp, B_NT) * scale, NEG)
            s_c = jnp.where(m_cur, _dot(qb, kb, B_NT) * scale, NEG)
            m = jnp.maximum(jnp.max(s_p, axis=-1, keepdims=True), jnp.max(s_c, axis=-1, keepdims=True))
            l = jnp.sum(jnp.exp(s_p - m), axis=-1, keepdims=True) + jnp.sum(jnp.exp(s_c - m), axis=-1, keepdims=True)
            lse = m + jnp.log(l)
            o = _dot(jnp.exp(s_p - lse).astype(BF16), vp, B_NN) + _dot(jnp.exp(s_c - lse).astype(BF16), vb, B_NN)
            _dil_unblocks(o_ref, cols, d, seg, n_seg, o)
            _dil_unblocks(lse_ref, cols, d, seg, n_seg, jnp.broadcast_to(lse, (nb, DIL_BLK, HEAD_DIM)))

    cur = pl.BlockSpec((tr, W), lambda n, cb: (n, cb))
    prv = pl.BlockSpec((seg, W), lambda n, cb: (jnp.maximum(n * n_seg - 1, 0), cb))
    shp = jax.ShapeDtypeStruct((S, DIL_W), F32)
    return pl.pallas_call(body, name=f"dil_fwd_{d}", grid=(S // tr, HEADS // DIL_HB),
                          in_specs=[cur, cur, cur, prv, prv], out_specs=[cur, cur], out_shape=[shp, shp],
                          compiler_params=_params(("arbitrary", "arbitrary")))(q, k, v, k, v)


def _dilated_bwd(q, k, v, do, lse, dd, d):
    S = q.shape[0]
    seg, tr, n_seg = _dil_geometry(S, d)
    nb = n_seg * d
    nt = S // tr
    W = DIL_HB_BWD * HEAD_DIM
    scale = 1.0 / math.sqrt(HEAD_DIM)

    def body(q_ref, k_ref, v_ref, do_ref, lse_ref, dd_ref, kp_ref, vp_ref, qn_ref, don_ref, lsen_ref, ddn_ref,
             dq_ref, dk_ref, dv_ref):
        n = pl.program_id(0)
        m_prev, m_cur = _dil_band(nb, d, n == 0)
        m_next = _dil_band(d, d, n == nt - 1)[0]

        def pair(qb, kb, vb, dob, lse_b, dd_b, mask):
            pr = jnp.where(mask, jnp.exp(_dot(qb, kb, B_NT) * scale - lse_b), 0.0)
            ds = (pr * (_dot(dob, vb, B_NT) - dd_b) * scale).astype(BF16)
            return pr.astype(BF16), ds

        for hb in range(DIL_HB_BWD):
            cols = slice(hb * HEAD_DIM, (hb + 1) * HEAD_DIM)
            stat = cols
            blocks = lambda ref, c, n_s, dt: _dil_blocks(ref, c, d, seg, n_s, dt)
            qb, kb, vb, dob = (blocks(r, cols, n_seg, BF16) for r in (q_ref, k_ref, v_ref, do_ref))
            lse_b, dd_b = blocks(lse_ref, stat, n_seg, F32), blocks(dd_ref, stat, n_seg, F32)
            kp = _with_before(blocks(kp_ref, cols, 1, BF16), kb, d)
            vp = _with_before(blocks(vp_ref, cols, 1, BF16), vb, d)
            p_c, ds_c = pair(qb, kb, vb, dob, lse_b, dd_b, m_cur)
            p_p, ds_p = pair(qb, kp, vp, dob, lse_b, dd_b, m_prev)
            qn, don = blocks(qn_ref, cols, 1, BF16), blocks(don_ref, cols, 1, BF16)
            p_n, ds_n = pair(qn, kb[nb - d:], vb[nb - d:], don, blocks(lsen_ref, stat, 1, F32),
                             blocks(ddn_ref, stat, 1, F32), m_next)
            dq = _dot(ds_c, kb, B_NN) + _dot(ds_p, kp, B_NN)
            shift = lambda own, nxt: nxt if nb == d else jnp.concatenate([own[d:], nxt], axis=0)
            dk = _dot(ds_c, qb, B_TN) + shift(_dot(ds_p, qb, B_TN), _dot(ds_n, qn, B_TN))
            dv = _dot(p_c, dob, B_TN) + shift(_dot(p_p, dob, B_TN), _dot(p_n, don, B_TN))
            _dil_unblocks(dq_ref, cols, d, seg, n_seg, dq)
            _dil_unblocks(dk_ref, cols, d, seg, n_seg, dk)
            _dil_unblocks(dv_ref, cols, d, seg, n_seg, dv)

    cur = pl.BlockSpec((tr, W), lambda n, cb: (n, cb))
    prv = pl.BlockSpec((seg, W), lambda n, cb: (jnp.maximum(n * n_seg - 1, 0), cb))
    nxt = pl.BlockSpec((seg, W), lambda n, cb: (jnp.minimum((n + 1) * n_seg, S // seg - 1), cb))
    shp = jax.ShapeDtypeStruct((S, DIL_W), F32)
    return pl.pallas_call(body, name=f"dil_bwd_{d}", grid=(nt, HEADS // DIL_HB_BWD),
                          in_specs=[cur] * 6 + [prv, prv] + [nxt] * 4, out_specs=[cur] * 3, out_shape=[shp] * 3,
                          compiler_params=_params(("arbitrary", "arbitrary")))(q, k, v, do, lse, dd, k, v, q, do, lse, dd)


def _dil_merge(os_, lses):
    S = os_[0].shape[0]
    tm = _tile(S, 256, 8)

    def body(o0, o1, o2, l0, l1, l2, out_ref):
        ls = [l0[...], l1[...], l2[...]]
        m = jnp.maximum(jnp.maximum(ls[0], ls[1]), ls[2])
        es = [jnp.exp(l - m) for l in ls]
        den = es[0] + es[1] + es[2]
        out_ref[...] = ((es[0] * o0[...] + es[1] * o1[...] + es[2] * o2[...]) / den).astype(BF16)

    ins = [('row', a, DIL_W, 0) for a in list(os_) + list(lses)]
    return _rows("dil_merge", body, S, tm, ins, [(DIL_W, BF16)])[0]


def _dil_merge_bwd(dout, os_, lses):
    S = dout.shape[0]
    tm = _tile(S, 256, 8)

    def body(d_ref, o0, o1, o2, l0, l1, l2, do0, do1, do2, dd0, dd1, dd2):
        dv = d_ref[...]
        os3 = [o0[...], o1[...], o2[...]]
        ls = [l0[...], l1[...], l2[...]]
        m = jnp.maximum(jnp.maximum(ls[0], ls[1]), ls[2])
        es = [jnp.exp(l - m) for l in ls]
        den = es[0] + es[1] + es[2]
        ws = [e / den for e in es]
        for h in range(HEADS):
            cols = slice(h * HEAD_DIM, (h + 1) * HEAD_DIM)
            dw = [jnp.sum(dv[:, cols] * o[:, cols], axis=-1, keepdims=True) for o in os3]
            wh = [w[:, cols] for w in ws]
            mean = wh[0] * dw[0] + wh[1] * dw[1] + wh[2] * dw[2]
            for p_, (do_ref, dd_ref) in enumerate(((do0, dd0), (do1, dd1), (do2, dd2))):
                do_p = wh[p_] * dv[:, cols]
                dlse = wh[p_] * (dw[p_] - mean)
                do_ref[:, cols] = do_p
                dd_ref[:, cols] = wh[p_] * dw[p_] - dlse

    ins = [('row', a, DIL_W, 0) for a in [dout] + list(os_) + list(lses)]
    outs = _rows("dil_merge_bwd", body, S, tm, ins, [(DIL_W, F32)] * 6)
    return outs[:3], outs[3:]


def _qkv_prep(proj, gq, gkv, tabs):
    S = proj.shape[0]
    tm = _tile(S, 256, 8)
    cos_a, sin_a, cos_p, sin_p = tabs

    def body(cq, ckv, qd, kd, vd, kr, gq_ref, gkv_ref, ca, sa, cp, sp, o_cq, o_ckv, o_qd, o_kd, o_vd, o_kr):
        x = cq[...]
        o_cq[...] = (x * _rms_scale(x) * gq_ref[...]).astype(BF16)
        x = ckv[...]
        o_ckv[...] = (x * _rms_scale(x) * gkv_ref[...]).astype(BF16)
        c, s = cp[...], sp[...]
        for h in range(HEADS):
            cols = slice(h * HEAD_DIM, (h + 1) * HEAD_DIM)
            o_qd[:, cols] = _rope(qd[:, cols], c, s, PART_ROPE // 2)
            o_kd[:, cols] = _rope(kd[:, cols], c, s, PART_ROPE // 2)
        o_vd[...] = vd[...]
        o_kr[...] = _rope(kr[...], ca[...], sa[...], ROPE_DIM // 2).astype(BF16)

    ins = [('row', proj, Q_RANK, 0), ('row', proj, KV_RANK, 1), ('row', proj, DIL_W, 1), ('row', proj, DIL_W, 2),
           ('row', proj, DIL_W, 3), ('row', proj, LANE, 4 * DIL_W // LANE), ('full', gq), ('full', gkv),
           ('row', cos_a, LANE, 0), ('row', sin_a, LANE, 0), ('row', cos_p, LANE, 0), ('row', sin_p, LANE, 0)]
    return _rows("qkv_prep", body, S, tm, ins,
                 [(Q_RANK, BF16), (KV_RANK, BF16), (DIL_W, F32), (DIL_W, F32), (DIL_W, F32), (LANE, BF16)])


def _qk_finish(qa, kv, kr, tabs):
    S = qa.shape[0]
    tm = _tile(S, 256, 8)
    cos_a, sin_a = tabs[0], tabs[1]

    def body(qa_ref, kn_ref, kr_ref, ca, sa, q_ref, k_ref):
        c, s = ca[...], sa[...]
        krv = kr_ref[...]
        for h in range(HEADS):
            nope = slice(h * QK_PAD, h * QK_PAD + HEAD_DIM)
            rope = slice(h * QK_PAD + HEAD_DIM, (h + 1) * QK_PAD)
            q_ref[:, nope] = qa_ref[:, nope].astype(BF16)
            q_ref[:, rope] = _rope(qa_ref[:, rope], c, s, ROPE_DIM // 2).astype(BF16)
            k_ref[:, nope] = kn_ref[:, h * HEAD_DIM:(h + 1) * HEAD_DIM]
            k_ref[:, rope] = krv

    W = HEADS * QK_PAD
    ins = [('row', qa, W, 0), ('row', kv, DIL_W, 0), ('row', kr, LANE, 0), ('row', cos_a, LANE, 0), ('row', sin_a, LANE, 0)]
    return _rows("qk_finish", body, S, tm, ins, [(W, BF16), (W, BF16)])


def _qk_finish_bwd(dq, dk, dv, tabs):
    S = dq.shape[0]
    tm = _tile(S, 256, 8)
    cos_a, sin_a = tabs[0], tabs[1]

    def body(dq_ref, dk_ref, dv_ref, ca, sa, dqa_ref, dkv_ref, dkr_ref):
        c, s = ca[...], sa[...]
        krsum = jnp.zeros((tm, LANE), F32)
        for h in range(HEADS):
            nope = slice(h * QK_PAD, h * QK_PAD + HEAD_DIM)
            rope = slice(h * QK_PAD + HEAD_DIM, (h + 1) * QK_PAD)
            dqa_ref[:, nope] = dq_ref[:, nope].astype(BF16)
            dqa_ref[:, rope] = _rope_t(dq_ref[:, rope], c, s, ROPE_DIM // 2).astype(BF16)
            dkv_ref[:, h * HEAD_DIM:(h + 1) * HEAD_DIM] = dk_ref[:, nope].astype(BF16)
            krsum = krsum + dk_ref[:, rope]
        dkv_ref[:, DIL_W:] = dv_ref[...]
        dkr_ref[...] = _rope_t(krsum, c, s, ROPE_DIM // 2)

    W = HEADS * QK_PAD
    ins = [('row', dq, W, 0), ('row', dk, W, 0), ('row', dv, DIL_W, 0), ('row', cos_a, LANE, 0), ('row', sin_a, LANE, 0)]
    return _rows("qk_finish_bwd", body, S, tm, ins, [(W, BF16), (2 * DIL_W, BF16), (LANE, F32)])


def _qkv_prep_bwd(proj, gq, gkv, d_cqn, d_ckvn, dqs, dks, dvs, dkr, tabs):
    S = proj.shape[0]
    tm = _tile(S, 256, 8)
    cos_p, sin_p = tabs[2], tabs[3]

    def body(cq, ckv, gq_ref, gkv_ref, dcq, dckv, dq0, dq1, dq2, dk0, dk1, dk2, dv0, dv1, dv2, dkr_ref, cp, sp,
             out_ref, dgq_ref, dgkv_ref):
        dx, dg = _rms_bwd(cq[...], gq_ref[...], dcq[...])
        out_ref[:, 0:Q_RANK] = dx.astype(BF16)
        _acc_add(dgq_ref, jnp.sum(dg, axis=0, keepdims=True))
        dx, dg = _rms_bwd(ckv[...], gkv_ref[...], dckv[...])
        out_ref[:, Q_RANK:Q_RANK + KV_RANK] = dx.astype(BF16)
        _acc_add(dgkv_ref, jnp.sum(dg, axis=0, keepdims=True))
        c, s = cp[...], sp[...]
        base = Q_RANK + KV_RANK
        for h in range(HEADS):
            cols = slice(h * HEAD_DIM, (h + 1) * HEAD_DIM)
            dst = lambda part: slice(base + part * DIL_W + h * HEAD_DIM, base + part * DIL_W + (h + 1) * HEAD_DIM)
            out_ref[:, dst(0)] = _rope_t(dq0[:, cols] + dq1[:, cols] + dq2[:, cols], c, s, PART_ROPE // 2).astype(BF16)
            out_ref[:, dst(1)] = _rope_t(dk0[:, cols] + dk1[:, cols] + dk2[:, cols], c, s, PART_ROPE // 2).astype(BF16)
            out_ref[:, dst(2)] = (dv0[:, cols] + dv1[:, cols] + dv2[:, cols]).astype(BF16)
        out_ref[:, base + 3 * DIL_W:] = dkr_ref[...].astype(BF16)

    ins = [('row', proj, Q_RANK, 0), ('row', proj, KV_RANK, 1), ('full', gq), ('full', gkv),
           ('row', d_cqn, Q_RANK, 0), ('row', d_ckvn, KV_RANK, 0)]
    ins += [('row', a, DIL_W, 0) for a in list(dqs) + list(dks) + list(dvs)]
    ins += [('row', dkr, LANE, 0), ('row', cos_p, LANE, 0), ('row', sin_p, LANE, 0)]
    return _rows("qkv_prep_bwd", body, S, tm, ins, [(IN_PAD, BF16)], [Q_RANK, KV_RANK])


def _mix_fwd(x, p, tabs, rider=None):
    h = _rms_cast("mix_prenorm", x, p['pre_g'])
    proj = _mm_nn("mix_proj", h, p['w_in'], F32)
    cqn, ckvn, qd, kd, vd, kr = _qkv_prep(proj, p['q_norm_g'], p['kv_norm_g'], tabs)
    qa = _mm_nn("mla_q_up", cqn, p['w_uq'], F32, tn=1024)
    kv = _mm_nn("mla_kv_up", ckvn, p['w_ukv'], BF16, tn=1024)
    q_cat, k_cat = _qk_finish(qa, kv, kr, tabs)
    (o_a, lse_a), rode = _mla_fwd(q_cat, k_cat, kv, rider)
    o_ps, lse_ps = [], []
    for _, d in DIL_PATTERNS:
        o_p, lse_p = _dilated_fwd(qd, kd, vd, d)
        o_ps.append(o_p)
        lse_ps.append(lse_p)
    o_b = _dil_merge(o_ps, lse_ps)
    o_cat = jnp.stack([o_a, o_b])
    x_out, y = _chunk_post("mix_out", o_cat, p['w_o'].reshape(2, DIL_W, -1), x, p['post_g'], 1.0)
    return x_out, (x, h, proj, cqn, ckvn, qd, kd, vd, q_cat, k_cat, kv, o_a, lse_a, o_ps, lse_ps, o_cat, y), rode


def _mix_bwd(dxo, p, tabs, saved, rider=None):
    x, h, proj, cqn, ckvn, qd, kd, vd, q_cat, k_cat, kv, o_a, lse_a, o_ps, lse_ps, o_cat, y = saved
    dy, d_post = _postnorm_bwd("mix_postnorm_bwd", dxo, y, p['post_g'], 1.0)
    w_o = p['w_o']
    d_oa = _mm_nt("mix_do_a", dy, w_o[:DIL_W], BF16, tk=1024)
    d_ob = _mm_nt("mix_do_b", dy, w_o[DIL_W:], F32, tk=1024)
    d_wo = _mm_tn_chunks_a("mix_dwo", o_cat, dy, BF16).reshape(w_o.shape)
    stats = _mla_stats(d_oa, o_a, lse_a)
    (dq, dk, dv), rode = _mla_bwd(q_cat, k_cat, kv, d_oa, stats, rider)
    dqa, dkv, dkr = _qk_finish_bwd(dq, dk, dv, tabs)
    d_cqn = _mm_nt("mla_dcq", dqa, p['w_uq'], F32, tk=1024)
    d_ckvn = _mm_nt("mla_dckv", dkv, p['w_ukv'], F32, tk=1024)
    d_wuq = _mm_tn("mla_dwuq", cqn, dqa, BF16, tn=1024)
    d_wukv = _mm_tn("mla_dwukv", ckvn, dkv, BF16, tn=1024)
    do_ps, dd_ps = _dil_merge_bwd(d_ob, o_ps, lse_ps)
    dqs, dks, dvs = [], [], []
    for (_, d), do_p, lse_p, dd_p in zip(DIL_PATTERNS, do_ps, lse_ps, dd_ps):
        dq_p, dk_p, dv_p = _dilated_bwd(qd, kd, vd, do_p, lse_p, dd_p, d)
        dqs.append(dq_p)
        dks.append(dk_p)
        dvs.append(dv_p)
    d_proj, d_gq, d_gkv = _qkv_prep_bwd(proj, p['q_norm_g'], p['kv_norm_g'], d_cqn, d_ckvn, dqs, dks, dvs, dkr, tabs)
    dh = _mm_nt("mix_dh", d_proj, p['w_in'], F32)
    d_win = _mm_tn("mix_dwin", h, d_proj, BF16)
    dx, d_pre = _prenorm_bwd("mix_prenorm_bwd", dh, x, p['pre_g'], dxo)
    return dx, dict(pre_g=d_pre, post_g=d_post, w_in=d_win, q_norm_g=d_gq, w_uq=d_wuq, kv_norm_g=d_gkv,
                    w_ukv=d_wukv, w_o=d_wo), rode


def _cols_from_shards(g):
    return jnp.transpose(g, (1, 0, 2)).reshape(g.shape[1], -1)


def _shards_from_cols(w):
    K = w.shape[0]
    return jnp.transpose(w.reshape(K, N_DEV, -1), (1, 0, 2))


def _win_layout(g):
    w = _cols_from_shards(g)
    a = Q_RANK + KV_RANK
    return jnp.concatenate([w[:, :a], w[:, a + ROPE_DIM:], w[:, a:a + ROPE_DIM],
                            jnp.zeros((w.shape[0], IN_PAD - IN_COLS), w.dtype)], axis=1)


def _win_unlayout(dw):
    a = Q_RANK + KV_RANK
    w = jnp.concatenate([dw[:, :a], dw[:, a + 3 * DIL_W:a + 3 * DIL_W + ROPE_DIM], dw[:, a:a + 3 * DIL_W]], axis=1)
    return _shards_from_cols(w)


def _wuq_layout(g):
    return _cols_from_shards(jnp.pad(g, ((0, 0), (0, 0), (0, QK_PAD - HEAD_DIM - ROPE_DIM))))


def _wuq_unlayout(dw):
    return _shards_from_cols(dw)[:, :, :HEAD_DIM + ROPE_DIM]


def _wukv_layout(g):
    return jnp.concatenate([_cols_from_shards(g[:, :, :HEAD_DIM]), _cols_from_shards(g[:, :, HEAD_DIM:])], axis=1)


def _wukv_unlayout(dw):
    return jnp.concatenate([_shards_from_cols(dw[:, :DIL_W]), _shards_from_cols(dw[:, DIL_W:])], axis=2)


def _cast_bf16(x):
    shp = x.shape
    x2 = x.reshape(-1, shp[-1])
    R, C = x2.shape
    tr = _tile(R, 512, 8)

    def body(x_ref, o_ref):
        o_ref[...] = x_ref[...].astype(BF16)

    spec = pl.BlockSpec((tr, C), lambda i: (i, 0))
    out = pl.pallas_call(body, name="cast_bf16", grid=(R // tr,), in_specs=[spec], out_specs=spec,
                         out_shape=jax.ShapeDtypeStruct((R, C), BF16), compiler_params=_params(("arbitrary",)))(x2)
    return out.reshape(shp)


def _mesh_pos():
    x, y, c = lax.axis_index("x"), lax.axis_index("y"), lax.axis_index("c")
    return x, y, c


class _Rider:
    def __init__(self, name, arrays, out_shapes, scratch, start, finish):
        self.name, self.arrays, self.out_shapes, self.scratch = name, arrays, out_shapes, scratch
        self.start, self.finish = start, finish
        self.n = len(arrays)

    def split(self, refs):
        return refs[:self.n], refs[self.n:2 * self.n], refs[2 * self.n:]


def _run_rider(rider):
    def body(*refs):
        rider.start(*rider.split(refs))
        rider.finish(*rider.split(refs))

    any_spec = pl.BlockSpec(memory_space=pl.ANY)
    return pl.pallas_call(body, name=rider.name, in_specs=[any_spec] * rider.n, out_specs=[any_spec] * rider.n,
                          out_shape=rider.out_shapes, scratch_shapes=rider.scratch)(*rider.arrays)


def _ride(rider, n_in, n_out, n_scr, first, last, compute):
    def body(*refs):
        r = 0 if rider is None else rider.n
        o0 = n_in + r
        s0 = o0 + n_out + r
        own = (refs[:n_in], refs[o0:o0 + n_out], refs[s0:s0 + n_scr])
        if rider is None:
            compute(*own)
            return
        mine = (refs[n_in:o0], refs[o0 + n_out:s0], refs[s0 + n_scr:])

        @pl.when(first())
        def _():
            rider.start(*mine)

        compute(*own)

        @pl.when(last())
        def _():
            rider.finish(*mine)

    return body


def _gather_rider(xs):
    n = len(xs)

    def program(x_refs, o_refs, sems):
        send, recv, loc = sems
        x, y, c = _mesh_pos()
        me, sib = (x, y, c), (x, y, 1 - c)
        chips = [(1 - x, y), (x, 1 - y), (1 - x, 1 - y)]

        def slot(k, dev):
            return o_refs[k].at[4 * dev[0] + 2 * dev[1] + dev[2]]

        def copy(k, s, block, to, src=None):
            return pltpu.make_async_remote_copy(src_ref=slot(k, block) if src is None else src, dst_ref=slot(k, block),
                                                send_sem=send.at[k, s], recv_sem=recv.at[k, s],
                                                device_id=to, device_id_type=MESH)

        mine = [pltpu.make_async_copy(x_refs[k], slot(k, me), loc.at[k]) for k in range(n)]
        first = []
        for k in range(n):
            first.append(copy(k, 0, me, sib, src=x_refs[k]))
            first += [copy(k, 1 + j, me, (*chip, c), src=x_refs[k]) for j, chip in enumerate(chips)]

        def start():
            for cp in mine + first:
                cp.start()

        def finish():
            passed = []
            for k in range(n):
                for j, chip in enumerate(chips):
                    copy(k, 1 + j, (*chip, c), me).wait_recv()
                    fwd = copy(k, 4 + j, (*chip, c), sib)
                    fwd.start()
                    passed.append(fwd)
            for k in range(n):
                copy(k, 0, sib, me).wait_recv()
                for j, chip in enumerate(chips):
                    copy(k, 4 + j, (*chip, 1 - c), me).wait_recv()
            for cp in first + passed:
                cp.wait_send()
            for cp in mine:
                cp.wait()

        return start, finish

    return _Rider("all_gather", list(xs), [jax.ShapeDtypeStruct((N_DEV,) + a.shape, a.dtype) for a in xs],
                  [pltpu.SemaphoreType.DMA((n, 7)), pltpu.SemaphoreType.DMA((n, 7)), pltpu.SemaphoreType.DMA((n,))],
                  lambda *refs: program(*refs)[0](), lambda *refs: program(*refs)[1]())


def _exchange_rider(gs):
    n = len(gs)

    def program(g_refs, r_refs, sems):
        send, recv, loc = sems
        x, y, c = _mesh_pos()
        me = 4 * x + 2 * y + c
        mine = [pltpu.make_async_copy(g_refs[k].at[me], r_refs[k].at[me], loc.at[k]) for k in range(n)]
        out_cps, in_cps = [], []
        for k in range(n):
            for m in range(1, N_DEV):
                px = 1 - x if m & 4 else x
                py = 1 - y if m & 2 else y
                pc = 1 - c if m & 1 else c
                peer = 4 * px + 2 * py + pc
                kw = dict(src_ref=g_refs[k].at[peer], send_sem=send.at[k, m - 1], recv_sem=recv.at[k, m - 1],
                          device_id=(px, py, pc), device_id_type=MESH)
                out_cps.append(pltpu.make_async_remote_copy(dst_ref=r_refs[k].at[me], **kw))
                in_cps.append(pltpu.make_async_remote_copy(dst_ref=r_refs[k].at[peer], **kw))

        def start():
            for cp in mine + out_cps:
                cp.start()

        def finish():
            for cp in in_cps:
                cp.wait_recv()
            for cp in out_cps:
                cp.wait_send()
            for cp in mine:
                cp.wait()

        return start, finish

    return _Rider("grad_exchange", list(gs), [jax.ShapeDtypeStruct(a.shape, a.dtype) for a in gs],
                  [pltpu.SemaphoreType.DMA((n, 7)), pltpu.SemaphoreType.DMA((n, 7)), pltpu.SemaphoreType.DMA((n,))],
                  lambda *refs: program(*refs)[0](), lambda *refs: program(*refs)[1]())


def _all_reduce_small(v):
    R, C = v.shape

    def body(v_ref, o_ref, buf, send, recv):
        x, y, c = _mesh_pos()
        me = 4 * x + 2 * y + c
        buf[me] = v_ref[...]
        copies = []
        for m in range(1, N_DEV):
            px = 1 - x if m & 4 else x
            py = 1 - y if m & 2 else y
            pc = 1 - c if m & 1 else c
            peer = 4 * px + 2 * py + pc
            copies.append((pltpu.make_async_remote_copy(
                src_ref=v_ref, dst_ref=buf.at[me], send_sem=send.at[m - 1], recv_sem=recv.at[m - 1],
                device_id=(px, py, pc), device_id_type=MESH),
                pltpu.make_async_remote_copy(
                src_ref=v_ref, dst_ref=buf.at[peer], send_sem=send.at[m - 1], recv_sem=recv.at[m - 1],
                device_id=(px, py, pc), device_id_type=MESH)))
        for out_cp, _ in copies:
            out_cp.start()
        for _, in_cp in copies:
            in_cp.wait_recv()
        for out_cp, _ in copies:
            out_cp.wait_send()
        total = buf[0]
        for s in range(1, N_DEV):
            total = total + buf[s]
        o_ref[...] = total

    vmem = pl.BlockSpec(memory_space=pltpu.VMEM)
    return pl.pallas_call(
        body, name="all_reduce_small", in_specs=[vmem], out_specs=vmem, out_shape=jax.ShapeDtypeStruct((R, C), F32),
        scratch_shapes=[pltpu.VMEM((N_DEV, R, C), F32), pltpu.SemaphoreType.DMA((7,)), pltpu.SemaphoreType.DMA((7,))],
    )(v)


def _adamw(name, parts, w, m, v):
    L, R, C = w.shape
    P = parts[0].shape[0]
    tr = _tile(R, max(16, ADAMW_TILE_ELEMS // C), 16)
    nr = R // tr

    def body(*refs):
        p_refs, (w_ref, m_ref, v_ref), (g_out, d_out, m_out, v_out) = refs[:L], refs[L:L + 3], refs[L + 3:]

        def update(p_ref):
            g = p_ref[0].astype(F32)
            for s in range(1, P):
                g = g + p_ref[s].astype(F32)
            m_new = ADAM_B1 * m_ref[...] + (1.0 - ADAM_B1) * g
            v_new = ADAM_B2 * v_ref[...] + (1.0 - ADAM_B2) * (g * g)
            m_hat = m_new / (1.0 - ADAM_B1 ** ADAM_STEP)
            v_hat = v_new / (1.0 - ADAM_B2 ** ADAM_STEP)
            g_out[...] = g
            d_out[...] = -ADAM_LR * (m_hat / (jnp.sqrt(v_hat) + ADAM_EPS) + ADAM_WD * w_ref[...])
            m_out[...] = m_new
            v_out[...] = v_new

        for ll in range(L):
            pl.when(pl.program_id(0) == ll)(functools.partial(update, p_refs[ll]))

    def part_spec(ll):
        def index(l, i):
            return (0, jnp.where(l == ll, i, jnp.where(l > ll, nr - 1, 0)), 0)
        return pl.BlockSpec((P, tr, C), index)

    spec = pl.BlockSpec((None, tr, C), lambda l, i: (l, i, 0))
    shp = jax.ShapeDtypeStruct((L, R, C), F32)
    return pl.pallas_call(body, name=name, grid=(L, nr),
                          in_specs=[part_spec(ll) for ll in range(L)] + [spec, spec, spec],
                          out_specs=[spec] * 4, out_shape=[shp] * 4,
                          compiler_params=_params(("arbitrary", "arbitrary")))(*parts, w, m, v)


def _rope_tables(positions):
    pos = positions.reshape(-1).astype(F32)[:, None]
    S = pos.shape[0]

    def cs(dim):
        inv = ROPE_THETA ** (-jnp.arange(0, dim, 2, dtype=F32) / dim)
        ang = pos * inv
        return jnp.cos(ang), jnp.sin(ang)

    ca, sa = cs(ROPE_DIM)
    cp, sp = cs(PART_ROPE)
    z = lambda w: jnp.zeros((S, w), F32)
    return (jnp.concatenate([ca, ca, z(LANE - ROPE_DIM)], axis=1), jnp.concatenate([sa, sa, z(LANE - ROPE_DIM)], axis=1),
            jnp.concatenate([cp, cp, jnp.ones((S, LANE - PART_ROPE), F32)], axis=1),
            jnp.concatenate([sp, sp, z(LANE - PART_ROPE)], axis=1))


def _ffn_params(tag, gathered, gains, l):
    row = lambda n: gains[n][l][None, :]
    return dict(pre_g=row(tag + '_pre_g'), post_g=row(tag + '_post_g'), w_gate=gathered[0], w_up=gathered[1],
                w_down=gathered[2])


def _mix_params(gathered, gains, l):
    row = lambda n: gains[n][l][None, :]
    w_in, w_uq, w_ukv, w_o = gathered
    return dict(pre_g=row('mix_pre_g'), post_g=row('mix_post_g'), q_norm_g=row('mla_q_norm_g'),
                kv_norm_g=row('mla_kv_norm_g'), w_in=_win_layout(w_in), w_uq=_wuq_layout(w_uq),
                w_ukv=_wukv_layout(w_ukv), w_o=w_o.reshape(-1, w_o.shape[-1]))


def _gain_grads(d1, dm, d2):
    return dict(ffn1_pre_g=d1['pre_g'], ffn1_post_g=d1['post_g'], mix_pre_g=dm['pre_g'], mix_post_g=dm['post_g'],
                mla_q_norm_g=dm['q_norm_g'], mla_kv_norm_g=dm['kv_norm_g'], ffn2_pre_g=d2['pre_g'], ffn2_post_g=d2['post_g'])


def _pack(vecs, width):
    flat = jnp.concatenate([v.reshape(-1) for v in vecs])
    per = 8 * width
    flat = jnp.pad(flat, (0, (-flat.shape[0]) % per))
    return flat.reshape(-1, width)


def _unpack(packed, shapes):
    flat = packed.reshape(-1)
    out, off = [], 0
    for shp in shapes:
        size = math.prod(shp)
        out.append(flat[off:off + size].reshape(shp))
        off += size
    return out


def kernel(x, positions, ffn1_pre_g, ffn1_post_g, ffn1_w_gate, ffn1_w_up, ffn1_w_down, mix_pre_g, mix_post_g, w_in, mla_q_norm_g, mla_w_uq, mla_kv_norm_g, mla_w_ukv, w_o, ffn2_pre_g, ffn2_post_g, ffn2_w_gate, ffn2_w_up, ffn2_w_down, loss_target, m_ffn1_pre_g, m_ffn1_post_g, m_ffn1_w_gate, m_ffn1_w_up, m_ffn1_w_down, m_mix_pre_g, m_mix_post_g, m_w_in, m_mla_q_norm_g, m_mla_w_uq, m_mla_kv_norm_g, m_mla_w_ukv, m_w_o, m_ffn2_pre_g, m_ffn2_post_g, m_ffn2_w_gate, m_ffn2_w_up, m_ffn2_w_down, v_ffn1_pre_g, v_ffn1_post_g, v_ffn1_w_gate, v_ffn1_w_up, v_ffn1_w_down, v_mix_pre_g, v_mix_post_g, v_w_in, v_mla_q_norm_g, v_mla_w_uq, v_mla_kv_norm_g, v_mla_w_ukv, v_w_o, v_ffn2_pre_g, v_ffn2_post_g, v_ffn2_w_gate, v_ffn2_w_up, v_ffn2_w_down):
    w = dict(zip(WNAMES, (ffn1_pre_g, ffn1_post_g, ffn1_w_gate, ffn1_w_up, ffn1_w_down, mix_pre_g, mix_post_g, w_in,
                          mla_q_norm_g, mla_w_uq, mla_kv_norm_g, mla_w_ukv, w_o, ffn2_pre_g, ffn2_post_g,
                          ffn2_w_gate, ffn2_w_up, ffn2_w_down)))
    mom = dict(zip(WNAMES, (m_ffn1_pre_g, m_ffn1_post_g, m_ffn1_w_gate, m_ffn1_w_up, m_ffn1_w_down, m_mix_pre_g,
                            m_mix_post_g, m_w_in, m_mla_q_norm_g, m_mla_w_uq, m_mla_kv_norm_g, m_mla_w_ukv, m_w_o,
                            m_ffn2_pre_g, m_ffn2_post_g, m_ffn2_w_gate, m_ffn2_w_up, m_ffn2_w_down)))
    var = dict(zip(WNAMES, (v_ffn1_pre_g, v_ffn1_post_g, v_ffn1_w_gate, v_ffn1_w_up, v_ffn1_w_down, v_mix_pre_g,
                            v_mix_post_g, v_w_in, v_mla_q_norm_g, v_mla_w_uq, v_mla_kv_norm_g, v_mla_w_ukv, v_w_o,
                            v_ffn2_pre_g, v_ffn2_post_g, v_ffn2_w_gate, v_ffn2_w_up, v_ffn2_w_down)))
    depth = w_in.shape[0]
    xs = x[0]
    D = xs.shape[1]
    tabs = _rope_tables(positions)

    shards = {n: _cast_bf16(w[n]) for n in BIG}
    ffn1_w, mix_w, ffn2_w = BIG[:3], BIG[3:7], BIG[7:]
    local = lambda names, l: [shards[n][l] for n in names]
    p1 = _ffn_params('ffn1', _run_rider(_gather_rider(local(ffn1_w, 0))), w, 0)
    pm = p2 = None

    params, saved = [], []
    act = xs
    for l in range(depth):
        more = l + 1 < depth
        act, s1, got = _ffn_fwd(act, p1, _gather_rider(local(mix_w, 0) if l == 0 else local(ffn2_w, l)))
        if l == 0:
            pm = _mix_params(got, w, 0)
        else:
            p2 = _ffn_params('ffn2', got, w, l)
        riding_w = (local(ffn1_w, l + 1) if more else []) + (local(ffn2_w, 0) if l == 0 else [])
        act, sm, got = _mix_fwd(act, pm, tabs, _gather_rider(riding_w) if riding_w else None)
        got = list(got)
        if more:
            p1_next, got = _ffn_params('ffn1', got[:3], w, l + 1), got[3:]
        if l == 0:
            p2 = _ffn_params('ffn2', got, w, 0)
        act, s2, got = _ffn_fwd(act, p2, _gather_rider(local(mix_w, l + 1)) if more else None)
        params.append((p1, pm, p2))
        saved.append((s1, sm, s2))
        if more:
            p1, pm = p1_next, _mix_params(got, w, l + 1)
    dact, loss_part = _loss_head(act, loss_target[0])

    received = [[None] * len(BIG) for _ in range(depth)]
    gain_parts = [None] * depth

    def riding(items):
        return _exchange_rider([a for _, _, a in items]) if items else None

    def keep(items, got):
        for (ll, i, _), r in zip(items, got or ()):
            received[ll][i] = r

    on_ffn, on_attn = [], []
    for l in reversed(range(depth)):
        p1, pm, p2 = params[l]
        s1, sm, s2 = saved[l]
        dact, d2, got = _ffn_bwd(dact, p2, s2, riding(on_ffn))
        keep(on_ffn, got)
        on_attn = [(l, 7, d2['w_gate']), (l, 8, d2['w_up']), (l, 9, d2['w_down'])] + on_attn
        dact, dm, got = _mix_bwd(dact, pm, tabs, sm, riding(on_attn))
        keep(on_attn, got)
        on_ffn = [(l, 3, _win_unlayout(dm['w_in'])), (l, 4, _wuq_unlayout(dm['w_uq'])),
                  (l, 5, _wukv_unlayout(dm['w_ukv'])), (l, 6, dm['w_o'].reshape(N_DEV, -1, dm['w_o'].shape[-1]))]
        dact, d1, got = _ffn_bwd(dact, p1, s1, riding(on_ffn))
        keep(on_ffn, got)
        on_ffn = [(l, 0, d1['w_gate']), (l, 1, d1['w_up'])]
        on_attn = [(l, 2, d1['w_down'])]
        gain_parts[l] = _gain_grads(d1, dm, d2)
    last = on_ffn + on_attn
    keep(last, _run_rider(riding(last)))

    gain_local = [jnp.stack([gain_parts[l][n].reshape(-1) for l in range(depth)]) for n in GAINS]
    packed = _all_reduce_small(_pack(gain_local + [loss_part.reshape(1)], D))
    summed = _unpack(packed, [w[n].shape for n in GAINS] + [(1,)])
    loss = summed[-1][0]

    out = {}
    for i, n in enumerate(BIG):
        out[n] = _adamw("adamw_" + n, [received[l][i] for l in range(depth)], w[n], mom[n], var[n])
    pk = lambda d: _pack([d[n] for n in GAINS], D)[None]
    res = _adamw("adamw_gains", [_pack(summed[:-1], D)[None]], pk(w), pk(mom), pk(var))
    for t in range(4):
        for n, a in zip(GAINS, _unpack(res[t], [w[n].shape for n in GAINS])):
            out.setdefault(n, [None] * 4)[t] = a

    grads = [out[n][0] for n in WNAMES]
    deltas = [out[n][1] for n in WNAMES]
    new_m = [out[n][2] for n in WNAMES]
    new_v = [out[n][3] for n in WNAMES]
    return (loss, dact[None], *grads, *deltas, *new_m, *new_v)
```

```python
import functools
import math

import jax
import jax.numpy as jnp
from jax import lax
from jax.experimental import pallas as pl
from jax.experimental.pallas import tpu as pltpu

F32 = jnp.float32
BF16 = jnp.bfloat16
N_DEV = 8
MESH = pl.DeviceIdType.MESH

HEADS = 8
HEAD_DIM = 128
Q_RANK = 512
KV_RANK = 512
ROPE_DIM = 64
QK_PAD = 256
PART_ROPE = 32
DIL_PATTERNS = ((128, 1), (512, 4), (2048, 16))
ROPE_THETA = 500000.0
RMS_EPS = 1e-6
NEG = -1e30
LANE = 128
IN_COLS = 4160
IN_PAD = 4224
DIL_W = HEADS * HEAD_DIM

ADAM_LR, ADAM_B1, ADAM_B2, ADAM_EPS, ADAM_WD, ADAM_STEP = 0.001, 0.9, 0.999, 1e-08, 0.01, 10

VMEM_LIMIT = 56 * 1024 * 1024
SUB_ROWS = 256
ADAMW_TILE_ELEMS = 128 * 1024

WNAMES = ['ffn1_pre_g', 'ffn1_post_g', 'ffn1_w_gate', 'ffn1_w_up', 'ffn1_w_down', 'mix_pre_g', 'mix_post_g', 'w_in',
          'mla_q_norm_g', 'mla_w_uq', 'mla_kv_norm_g', 'mla_w_ukv', 'w_o', 'ffn2_pre_g', 'ffn2_post_g',
          'ffn2_w_gate', 'ffn2_w_up', 'ffn2_w_down']
BIG = ['ffn1_w_gate', 'ffn1_w_up', 'ffn1_w_down', 'w_in', 'mla_w_uq', 'mla_w_ukv', 'w_o',
       'ffn2_w_gate', 'ffn2_w_up', 'ffn2_w_down']
GAINS = [n for n in WNAMES if n not in BIG]

NT = (((1,), (1,)), ((), ()))
NN = (((1,), (0,)), ((), ()))
TN = (((0,), (0,)), ((), ()))


def _tile(n, target, mult):
    best = None
    t = mult
    while t <= min(n, target):
        if n % t == 0:
            best = t
        t += mult
    return n if best is None else best


def _params(sem=None):
    kw = dict(vmem_limit_bytes=VMEM_LIMIT)
    if sem is not None:
        kw['dimension_semantics'] = sem
    return pltpu.CompilerParams(**kw)


def _dot(a, b, dn):
    return lax.dot_general(a, b, dn, preferred_element_type=F32)


def _mm(name, pairs, pair_specs, dn, grid, k_axis, acc_shape, out_shapes, out_specs, epilogue,
        extras=(), extra_specs=(), rider=None, acc_by_ref=False):
    n_pair = len(pairs)
    nk = 1 if k_axis is None else grid[k_axis]
    assert nk > 1 or not acc_by_ref

    def compute(ins, outs, scr):
        ab, ex = ins[:2 * n_pair], ins[2 * n_pair:]
        part = _dot(ab[0][...], ab[1][...], dn)
        for p in range(1, n_pair):
            part = part + _dot(ab[2 * p][...], ab[2 * p + 1][...], dn)
        if nk == 1:
            epilogue(part, ex, outs)
            return
        acc = scr[0]
        k = pl.program_id(k_axis)

        @pl.when(k == 0)
        def _():
            acc[...] = part

        @pl.when(k > 0)
        def _():
            acc[...] += part

        @pl.when(k == nk - 1)
        def _():
            epilogue(acc if acc_by_ref else acc[...], ex, outs)

    flat, flat_specs = [], []
    for (a, b), (sa, sb) in zip(pairs, pair_specs):
        flat += [a, b]
        flat_specs += [sa, sb]
    outs, rode = _with_rider(rider, name, grid, compute, len(flat) + len(extras), flat_specs + list(extra_specs),
                             out_specs, out_shapes, flat + list(extras),
                             scratch=[pltpu.VMEM(acc_shape, F32)] if nk > 1 else [])
    return outs if rider is None else (outs, rode)


def _store(dtype):
    def epi(acc, ex, outs):
        outs[0][...] = acc.astype(dtype)
    return epi


def _mm_nn(name, a, b, out_dtype, tm=1024, tn=1408):
    M, K = a.shape
    N = b.shape[1]
    tm, tn = _tile(M, tm, 8), _tile(N, tn, LANE)
    return _mm(name, [(a, b)],
               [(pl.BlockSpec((tm, K), lambda j, i: (i, 0)), pl.BlockSpec((K, tn), lambda j, i: (0, j)))],
               NN, (N // tn, M // tm), None, None,
               [jax.ShapeDtypeStruct((M, N), out_dtype)], [pl.BlockSpec((tm, tn), lambda j, i: (i, j))],
               _store(out_dtype))[0]


def _mm_nt(name, a, b, out_dtype, tm=1024, tk=1408):
    M, K = a.shape
    N = b.shape[0]
    tm, tk = _tile(M, tm, 8), _tile(K, tk, LANE)
    return _mm(name, [(a, b)],
               [(pl.BlockSpec((tm, tk), lambda i, k: (i, k)), pl.BlockSpec((N, tk), lambda i, k: (0, k)))],
               NT, (M // tm, K // tk), 1, (tm, N),
               [jax.ShapeDtypeStruct((M, N), out_dtype)], [pl.BlockSpec((tm, N), lambda i, k: (i, 0))],
               _store(out_dtype))[0]


def _mm_tn(name, a, b, out_dtype, ts=1024, tn=1408):
    M, K = a.shape
    N = b.shape[1]
    ts, tn = _tile(M, ts, 16), _tile(N, tn, LANE)
    return _mm(name, [(a, b)],
               [(pl.BlockSpec((ts, K), lambda j, m: (m, 0)), pl.BlockSpec((ts, tn), lambda j, m: (m, j)))],
               TN, (N // tn, M // ts), 1, (K, tn),
               [jax.ShapeDtypeStruct((K, N), out_dtype)], [pl.BlockSpec((K, tn), lambda j, m: (0, j))],
               _store(out_dtype))[0]


def _mm_tn_chunks_a(name, a3, b, out_dtype, ts=2048):
    C, M, Kc = a3.shape
    N = b.shape[1]
    ts = _tile(M, ts, 16)
    return _mm(name, [(a3, b)],
               [(pl.BlockSpec((None, ts, Kc), lambda c, m: (c, m, 0)), pl.BlockSpec((ts, N), lambda c, m: (m, 0)))],
               TN, (C, M // ts), 1, (Kc, N),
               [jax.ShapeDtypeStruct((C, Kc, N), out_dtype)], [pl.BlockSpec((None, Kc, N), lambda c, m: (c, 0, 0))],
               _store(out_dtype))[0]


def _mm_tn_chunks_b(name, a, b3, out_dtype, ts=2048, rider=None):
    M, K = a.shape
    C, _, Nc = b3.shape
    ts = _tile(M, ts, 16)
    res = _mm(name, [(a, b3)],
              [(pl.BlockSpec((ts, K), lambda c, m: (m, 0)), pl.BlockSpec((None, ts, Nc), lambda c, m: (c, m, 0)))],
              TN, (C, M // ts), 1, (K, Nc),
              [jax.ShapeDtypeStruct((C, K, Nc), out_dtype)], [pl.BlockSpec((None, K, Nc), lambda c, m: (c, 0, 0))],
              _store(out_dtype), rider=rider)
    return res[0] if rider is None else (res[0][0], res[1])


def _rows(name, body, n_rows, tm, ins, outs, accs=()):
    in_specs, arrays = [], []
    for spec in ins:
        if spec[0] == 'row':
            _, arr, width, cb = spec
            in_specs.append(pl.BlockSpec((tm, width), functools.partial(lambda i, cb: (i, cb), cb=cb)))
        else:
            arr = spec[1]
            in_specs.append(pl.BlockSpec(arr.shape, functools.partial(lambda i, nd: (0,) * nd, nd=arr.ndim)))
        arrays.append(arr)
    out_shapes = [jax.ShapeDtypeStruct((n_rows, w), dt) for w, dt in outs]
    out_specs = [pl.BlockSpec((tm, w), lambda i: (i, 0)) for w, _ in outs]
    out_shapes += [jax.ShapeDtypeStruct((1, w), F32) for w in accs]
    out_specs += [pl.BlockSpec((1, w), lambda i: (0, 0)) for w in accs]
    return pl.pallas_call(body, name=name, grid=(n_rows // tm,), in_specs=in_specs, out_specs=out_specs,
                          out_shape=out_shapes, compiler_params=_params(("arbitrary",)))(*arrays)


def _acc_add(ref, val):
    @pl.when(pl.program_id(0) == 0)
    def _():
        ref[...] = val

    @pl.when(pl.program_id(0) > 0)
    def _():
        ref[...] += val


def _rms_scale(x):
    return lax.rsqrt(jnp.mean(x * x, axis=-1, keepdims=True) + RMS_EPS)


def _rms_bwd(x, g, dy):
    r = _rms_scale(x)
    t = dy * g
    dx = r * t - x * (r * r * r) * jnp.mean(t * x, axis=-1, keepdims=True)
    return dx, dy * x * r


def _rot_half(x, hw):
    lane = lax.broadcasted_iota(jnp.int32, x.shape, 1)
    left = pltpu.roll(x, LANE - hw, 1)
    right = pltpu.roll(x, hw, 1)
    return jnp.where(lane < hw, -left, right)


def _rope(x, cos, sin, hw):
    return x * cos + _rot_half(x, hw) * sin


def _rope_t(dy, cos, sin, hw):
    return dy * cos - _rot_half(dy, hw) * sin


def _rms_cast(name, x, g):
    S, D = x.shape
    tm = _tile(S, 512, 8)

    def body(x_ref, g_ref, o_ref):
        xv = x_ref[...]
        o_ref[...] = (xv * _rms_scale(xv) * g_ref[...]).astype(BF16)

    return _rows(name, body, S, tm, [('row', x, D, 0), ('full', g)], [(D, BF16)])[0]


def _postnorm_bwd(name, dxo, y, g, coef):
    S, D = y.shape
    tm = _tile(S, 512, 8)

    def body(d_ref, y_ref, g_ref, dy_ref, dg_ref):
        dx, dg = _rms_bwd(y_ref[...], g_ref[...], coef * d_ref[...])
        dy_ref[...] = dx.astype(BF16)
        _acc_add(dg_ref, jnp.sum(dg, axis=0, keepdims=True))

    return _rows(name, body, S, tm, [('row', dxo, D, 0), ('row', y, D, 0), ('full', g)], [(D, BF16)], [D])


def _prenorm_bwd(name, dh, x, g, dxo):
    S, D = x.shape
    tm = _tile(S, 512, 8)

    def body(dh_ref, x_ref, g_ref, d_ref, dx_ref, dg_ref):
        dx, dg = _rms_bwd(x_ref[...], g_ref[...], dh_ref[...])
        dx_ref[...] = d_ref[...] + dx
        _acc_add(dg_ref, jnp.sum(dg, axis=0, keepdims=True))

    return _rows(name, body, S, tm, [('row', dh, D, 0), ('row', x, D, 0), ('full', g), ('row', dxo, D, 0)],
                 [(D, F32)], [D])


def _loss_head(y, target):
    S, D = y.shape
    tm = _tile(S, 512, 8)

    def body(y_ref, t_ref, dy_ref, l_ref):
        e = y_ref[...] - t_ref[...]
        dy_ref[...] = e * (1.0 / D)
        row = 0.5 * jnp.mean(e * e, axis=-1, keepdims=True)
        _acc_add(l_ref, jnp.broadcast_to(jnp.sum(row, axis=0, keepdims=True), (1, LANE)))

    dy, l = _rows("loss_head", body, S, tm, [('row', y, D, 0), ('row', target, D, 0)], [(D, F32)], [LANE])
    return dy, l[0, 0]


def _ffn_up(h, wg, wu, rider=None):
    S, D = h.shape
    C, _, Fc = wg.shape
    tm = _tile(S, 1024, 8)

    sub = _tile(tm, SUB_ROWS, 8)

    def compute(ins, outs, _):
        h_ref, wg_ref, wu_ref = ins
        g_ref, u_ref, a_ref = outs
        for r in range(tm // sub):
            rows = slice(r * sub, (r + 1) * sub)
            hv = h_ref[rows, :]
            g = _dot(hv, wg_ref[...], NN)
            u = _dot(hv, wu_ref[...], NN)
            g_ref[rows, :] = g.astype(BF16)
            u_ref[rows, :] = u.astype(BF16)
            a_ref[rows, :] = (g * jax.nn.sigmoid(g) * u).astype(BF16)

    w_spec = pl.BlockSpec((None, D, Fc), lambda j, i: (j, 0, 0))
    o_spec = pl.BlockSpec((None, tm, Fc), lambda j, i: (j, i, 0))
    shp = jax.ShapeDtypeStruct((C, S, Fc), BF16)
    return _with_rider(rider, "ffn_up", (C, S // tm), compute, 3,
                       [pl.BlockSpec((tm, D), lambda j, i: (i, 0)), w_spec, w_spec],
                       [o_spec, o_spec, o_spec], [shp, shp, shp], (h, wg, wu))


def _chunk_post(name, a3, w3, x, g, coef):
    C, S, Kc = a3.shape
    D = w3.shape[2]
    tm = _tile(S, 512, 8)

    def epi(acc, ex, outs):
        x_ref, g_ref = ex
        outs[0][...] = x_ref[...] + coef * (acc * _rms_scale(acc) * g_ref[...])
        outs[1][...] = acc

    row = pl.BlockSpec((tm, D), lambda i, c: (i, 0))
    shp = jax.ShapeDtypeStruct((S, D), F32)
    per = 2 if C % 2 == 0 else 1
    specs = [(pl.BlockSpec((None, tm, Kc), functools.partial(lambda i, c, o: (per * c + o, i, 0), o=o)),
              pl.BlockSpec((None, Kc, D), functools.partial(lambda i, c, o: (per * c + o, 0, 0), o=o))) for o in range(per)]
    return _mm(name, [(a3, w3)] * per, specs, NN, (S // tm, C // per), 1, (tm, D), [shp, shp], [row, row], epi,
               extras=[x, g], extra_specs=[row, pl.BlockSpec((1, D), lambda i, c: (0, 0))])


def _ffn_da(dy, wd, gate, up):
    S, D = dy.shape
    C, Fc, _ = wd.shape
    tm = _tile(S, 1024, 8)

    def body(dy_ref, wd_ref, g_ref, u_ref, dg_ref, du_ref):
        da = _dot(dy_ref[...], wd_ref[...], NT)
        g = g_ref[...].astype(F32)
        u = u_ref[...].astype(F32)
        sig = jax.nn.sigmoid(g)
        dg_ref[...] = (da * u * (sig * (1.0 + g * (1.0 - sig)))).astype(BF16)
        du_ref[...] = (da * (g * sig)).astype(BF16)

    blk = pl.BlockSpec((None, tm, Fc), lambda c, i: (c, i, 0))
    shp = jax.ShapeDtypeStruct((C, S, Fc), BF16)
    return pl.pallas_call(body, name="ffn_da", grid=(C, S // tm),
                          in_specs=[pl.BlockSpec((tm, D), lambda c, i: (i, 0)),
                                    pl.BlockSpec((None, Fc, D), lambda c, i: (c, 0, 0)), blk, blk],
                          out_specs=[blk, blk], out_shape=[shp, shp],
                          compiler_params=_params(("arbitrary", "arbitrary")))(dy, wd, gate, up)


def _ffn_dh(dg, du, wg, wu, x, g, dxo, rider=None):
    C, S, Fc = dg.shape
    D = wg.shape[1]
    tm = _tile(S, 512, 8)

    sub = _tile(tm, SUB_ROWS, 8)

    def epi(acc, ex, outs):
        x_ref, g_ref, d_ref = ex
        gain_sum = jnp.zeros((1, D), F32)
        for r in range(tm // sub):
            rows = slice(r * sub, (r + 1) * sub)
            dx, dgain = _rms_bwd(x_ref[rows, :], g_ref[...], acc[rows, :])
            outs[0][rows, :] = d_ref[rows, :] + dx
            gain_sum = gain_sum + jnp.sum(dgain, axis=0, keepdims=True)
        _acc_add(outs[1], gain_sum)

    a_spec = pl.BlockSpec((None, tm, Fc), lambda i, c: (c, i, 0))
    w_spec = pl.BlockSpec((None, D, Fc), lambda i, c: (c, 0, 0))
    row = pl.BlockSpec((tm, D), lambda i, c: (i, 0))
    vec = pl.BlockSpec((1, D), lambda i, c: (0, 0))
    res = _mm("ffn_dh", [(dg, wg), (du, wu)], [(a_spec, w_spec), (a_spec, w_spec)], NT, (S // tm, C), 1, (tm, D),
              [jax.ShapeDtypeStruct((S, D), F32), jax.ShapeDtypeStruct((1, D), F32)], [row, vec], epi,
              extras=[x, g, dxo], extra_specs=[row, vec, row], rider=rider, acc_by_ref=True)
    (dx, d_gain), rode = (res, None) if rider is None else res
    return dx, d_gain, rode


def _ffn_fwd(x, p, rider=None):
    h = _rms_cast("ffn_prenorm", x, p['pre_g'])
    (gate, up, act), rode = _ffn_up(h, p['w_gate'], p['w_up'], rider)
    x_out, y = _chunk_post("ffn_down", act, p['w_down'], x, p['post_g'], 0.5)
    return x_out, (x, h, gate, up, act, y), rode


def _ffn_bwd(dxo, p, saved, riding=(), drain=False):
    x, h, gate, up, act, y = saved
    dy, d_post = _postnorm_bwd("ffn_postnorm_bwd", dxo, y, p['post_g'], 0.5)
    dgate, dup = _ffn_da(dy, p['w_down'], gate, up)
    d_wd = _mm_tn_chunks_a("ffn_dwd", act, dy, BF16)
    own = None
    dh_args = (dgate, dup, p['w_gate'], p['w_up'], x, p['pre_g'], dxo)
    if drain:
        d_wg, got_wd = _mm_tn_chunks_b("ffn_dwg", h, dgate, BF16, rider=_exchange_rider([d_wd]))
        d_wu, got_wg = _mm_tn_chunks_b("ffn_dwu", h, dup, BF16, rider=_exchange_rider([d_wg]))
        dx, d_pre, got = _ffn_dh(*dh_args, _exchange_rider(list(riding) + [d_wu]))
        own, got = dict(w_down=got_wd[0], w_gate=got_wg[0], w_up=got[-1]), got[:-1]
    else:
        dx, d_pre, got = _ffn_dh(*dh_args, _exchange_rider(list(riding)) if riding else None)
        d_wg = _mm_tn_chunks_b("ffn_dwg", h, dgate, BF16)
        d_wu = _mm_tn_chunks_b("ffn_dwu", h, dup, BF16)
    return dx, dict(pre_g=d_pre, post_g=d_post, w_gate=d_wg, w_up=d_wu, w_down=d_wd), got, own


MLA_FWD_TQ, MLA_FWD_TK = 512, 512
MLA_BWD_TQ, MLA_BWD_TK = 512, 512
MLA_HEADS_PER_STEP = 2


def _causal_mask(tq, tk, off):
    r = lax.broadcasted_iota(jnp.int32, (tq, tk), 0)
    c = lax.broadcasted_iota(jnp.int32, (tq, tk), 1)
    return r + off >= c


def _grid_ends(grid):
    def all_at(targets):
        hit = pl.program_id(0) == targets[0]
        for ax in range(1, len(grid)):
            hit = jnp.logical_and(hit, pl.program_id(ax) == targets[ax])
        return hit

    return (lambda: all_at([0] * len(grid))), (lambda: all_at([g - 1 for g in grid]))


def _with_rider(rider, name, grid, compute, n_in, in_specs, out_specs, out_shape, arrays, scratch=()):
    any_spec = pl.BlockSpec(memory_space=pl.ANY)
    r = 0 if rider is None else rider.n
    outs = pl.pallas_call(
        _ride(rider, n_in, len(out_shape), len(scratch), *_grid_ends(grid), compute), name=name, grid=grid,
        in_specs=list(in_specs) + [any_spec] * r, out_specs=list(out_specs) + [any_spec] * r,
        out_shape=list(out_shape) + ([] if rider is None else rider.out_shapes),
        scratch_shapes=list(scratch) + ([] if rider is None else rider.scratch),
        compiler_params=_params(("arbitrary",) * len(grid)))(*arrays, *([] if rider is None else rider.arrays))
    return outs[:len(out_shape)], outs[len(out_shape):]


def _mla_fwd(q, k, kv, rider=None):
    S = q.shape[0]
    tq, tk = _tile(S, MLA_FWD_TQ, LANE), _tile(S, MLA_FWD_TK, LANE)
    nq = S // tq
    n_edge = max(1, tq // tk)
    scale = 1.0 / math.sqrt(HEAD_DIM + ROPE_DIM)

    hp = MLA_HEADS_PER_STEP

    def compute(ins, outs, _):
        q_ref, k_ref, v_ref = ins
        o_ref, lse_ref = outs
        i = pl.program_id(1)

        def step(j, carry, masked):
            rows = pl.ds(pl.multiple_of(j * tk, tk), tk)
            out = []
            for hh, (m, l, acc) in enumerate(carry):
                qk = slice(hh * QK_PAD, (hh + 1) * QK_PAD)
                s = _dot(q_ref[:, qk], k_ref[rows, qk], NT) * scale
                if masked:
                    s = jnp.where(_causal_mask(tq, tk, i * tq - j * tk), s, NEG)
                m_new = jnp.maximum(m, jnp.max(s, axis=-1, keepdims=True))
                alpha = jnp.exp(m - m_new)
                pr = jnp.exp(s - m_new)
                l = alpha * l + jnp.sum(pr, axis=-1, keepdims=True)
                acc = alpha * acc + _dot(pr.astype(BF16), v_ref[rows, hh * HEAD_DIM:(hh + 1) * HEAD_DIM], NN)
                out.append((m_new, l, acc))
            return tuple(out)

        init = tuple((jnp.full((tq, 1), NEG, F32), jnp.zeros((tq, 1), F32), jnp.zeros((tq, HEAD_DIM), F32))
                     for _ in range(hp))
        n_full = (i * tq) // tk
        carry = lax.fori_loop(0, n_full, lambda j, c: step(j, c, False), init)
        for e in range(n_edge):
            carry = step(n_full + e, carry, True)
        for hh, (m, l, acc) in enumerate(carry):
            o_ref[:, hh * HEAD_DIM:(hh + 1) * HEAD_DIM] = (acc / l).astype(BF16)
            lse_ref[hh] = jnp.broadcast_to(m + jnp.log(l), (tq, LANE))

    return _with_rider(
        rider, "mla_fwd", (HEADS // hp, nq), compute, 3,
        [pl.BlockSpec((tq, hp * QK_PAD), lambda h, i: (i, h)), pl.BlockSpec((S, hp * QK_PAD), lambda h, i: (0, h)),
         pl.BlockSpec((S, hp * HEAD_DIM), lambda h, i: (0, HEADS // hp + h))],
        [pl.BlockSpec((tq, hp * HEAD_DIM), lambda h, i: (i, h)), pl.BlockSpec((hp, tq, LANE), lambda h, i: (h, i, 0))],
        [jax.ShapeDtypeStruct((S, HEADS * HEAD_DIM), BF16), jax.ShapeDtypeStruct((HEADS, S, LANE), F32)], (q, k, kv))


def _mla_stats(do, o, lse):
    S = o.shape[0]
    t = _tile(S, 512, 8)

    def body(do_ref, o_ref, lse_ref, st_ref):
        delta = jnp.sum(do_ref[...].astype(F32) * o_ref[...].astype(F32), axis=-1, keepdims=True)
        lane = lax.broadcasted_iota(jnp.int32, (t, LANE), 1)
        st_ref[...] = jnp.where(lane < LANE // 2, lse_ref[...], jnp.broadcast_to(delta, (t, LANE)))

    blk = pl.BlockSpec((t, HEAD_DIM), lambda h, i: (i, h))
    st = pl.BlockSpec((None, t, LANE), lambda h, i: (h, i, 0))
    return pl.pallas_call(body, name="mla_stats", grid=(HEADS, S // t), in_specs=[blk, blk, st], out_specs=st,
                          out_shape=jax.ShapeDtypeStruct((HEADS, S, LANE), F32),
                          compiler_params=_params(("arbitrary", "arbitrary")))(do, o, lse)


def _mla_bwd(q, k, kv, do, stats, rider=None):
    S = q.shape[0]
    tq, tk = _tile(S, MLA_BWD_TQ, LANE), _tile(S, MLA_BWD_TK, LANE)
    nq, nk = S // tq, S // tk
    n_edge = max(1, tk // tq)
    scale = 1.0 / math.sqrt(HEAD_DIM + ROPE_DIM)
    hp = MLA_HEADS_PER_STEP

    def compute(ins, outs, _):
        q_ref, do_ref, st_ref, k_ref, v_ref = ins
        dq_ref, dk_ref, dv_ref = outs
        j = pl.program_id(1)

        @pl.when(j == 0)
        def _():
            dq_ref[...] = jnp.zeros_like(dq_ref)

        def step(i, carry, masked):
            rows = pl.ds(pl.multiple_of(i * tq, tq), tq)
            out = []
            for hh, (dk, dv) in enumerate(carry):
                qk = slice(hh * QK_PAD, (hh + 1) * QK_PAD)
                vo = slice(hh * HEAD_DIM, (hh + 1) * HEAD_DIM)
                qb, kb, dob = q_ref[rows, qk], k_ref[:, qk], do_ref[rows, vo]
                pr = jnp.exp(_dot(qb, kb, NT) * scale - st_ref[hh, rows, 0:1])
                if masked:
                    pr = jnp.where(_causal_mask(tq, tk, i * tq - j * tk), pr, 0.0)
                dv = dv + _dot(pr.astype(BF16), dob, TN)
                dp = _dot(dob, v_ref[:, vo], NT)
                ds = (pr * (dp - st_ref[hh, rows, LANE // 2:LANE // 2 + 1]) * scale).astype(BF16)
                dk = dk + _dot(ds, qb, TN)
                dq_ref[rows, qk] += _dot(ds, kb, NN)
                out.append((dk, dv))
            return tuple(out)

        carry = tuple((jnp.zeros((tk, QK_PAD), F32), jnp.zeros((tk, HEAD_DIM), F32)) for _ in range(hp))
        i_edge = (j * tk) // tq
        for e in range(n_edge):
            carry = step(i_edge + e, carry, True)
        carry = lax.fori_loop(i_edge + n_edge, nq, lambda i, c: step(i, c, False), carry)
        for hh, (dk, dv) in enumerate(carry):
            dk_ref[:, hh * QK_PAD:(hh + 1) * QK_PAD] = dk
            dv_ref[:, hh * HEAD_DIM:(hh + 1) * HEAD_DIM] = dv.astype(BF16)

    once = pl.Buffered(1)
    return _with_rider(
        rider, "mla_bwd", (HEADS // hp, nk), compute, 5,
        [pl.BlockSpec((S, hp * QK_PAD), lambda h, j: (0, h), pipeline_mode=once),
         pl.BlockSpec((S, hp * HEAD_DIM), lambda h, j: (0, h), pipeline_mode=once),
         pl.BlockSpec((hp, S, LANE), lambda h, j: (h, 0, 0), pipeline_mode=once),
         pl.BlockSpec((tk, hp * QK_PAD), lambda h, j: (j, h)),
         pl.BlockSpec((tk, hp * HEAD_DIM), lambda h, j: (j, HEADS // hp + h))],
        [pl.BlockSpec((S, hp * QK_PAD), lambda h, j: (0, h), pipeline_mode=once),
         pl.BlockSpec((tk, hp * QK_PAD), lambda h, j: (j, h)),
         pl.BlockSpec((tk, hp * HEAD_DIM), lambda h, j: (j, h))],
        [jax.ShapeDtypeStruct((S, HEADS * QK_PAD), F32), jax.ShapeDtypeStruct((S, HEADS * QK_PAD), F32),
         jax.ShapeDtypeStruct((S, HEADS * HEAD_DIM), BF16)], (q, do, stats, k, kv))


DIL_BLK = 128
DIL_HB = 1
DIL_HB_BWD = 1
DIL_ROWS = 2048

B_NT = (((2,), (2,)), ((0,), (0,)))
B_NN = (((2,), (1,)), ((0,), (0,)))
B_TN = (((1,), (1,)), ((0,), (0,)))


def _dil_geometry(S, d):
    seg = DIL_BLK * d
    tr = max(seg, min(S, DIL_ROWS))
    assert S % tr == 0 and tr % seg == 0
    return seg, tr, tr // seg


def _dil_blocks(ref, cols, d, seg, n_seg, dtype):
    return jnp.stack([ref[pl.ds(g * seg + r, DIL_BLK, stride=d), cols].astype(dtype)
                      for g in range(n_seg) for r in range(d)])


def _dil_unblocks(ref, cols, d, seg, n_seg, val):
    for g in range(n_seg):
        for r in range(d):
            ref[pl.ds(g * seg + r, DIL_BLK, stride=d), cols] = val[g * d + r]


def _dil_band(n_blocks, n_edge, absent):
    shape = (n_blocks, DIL_BLK, DIL_BLK)
    b = lax.broadcasted_iota(jnp.int32, shape, 0)
    qi = lax.broadcasted_iota(jnp.int32, shape, 1)
    kj = lax.broadcasted_iota(jnp.int32, shape, 2)
    gone = jnp.logical_and(b < n_edge, absent)
    return kj >= qi + jnp.where(gone, DIL_BLK, 0), kj <= qi


def _with_before(edge, cur, d):
    return edge if cur.shape[0] == d else jnp.concatenate([edge, cur[:cur.shape[0] - d]], axis=0)


def _dilated_fwd(q, k, v, d):
    S = q.shape[0]
    seg, tr, n_seg = _dil_geometry(S, d)
    nb = n_seg * d
    W = DIL_HB * HEAD_DIM
    scale = 1.0 / math.sqrt(HEAD_DIM)

    def body(q_ref, k_ref, v_ref, kp_ref, vp_ref, o_ref, lse_ref):
        m_prev, m_cur = _dil_band(nb, d, pl.program_id(0) == 0)
        for hb in range(DIL_HB):
            cols = slice(hb * HEAD_DIM, (hb + 1) * HEAD_DIM)
            qb = _dil_blocks(q_ref, cols, d, seg, n_seg, BF16)
            kb = _dil_blocks(k_ref, cols, d, seg, n_seg, BF16)
            vb = _dil_blocks(v_ref, cols, d, seg, n_seg, BF16)
            kp = _with_before(_dil_blocks(kp_ref, cols, d, seg, 1, BF16), kb, d)
            vp = _with_before(_dil_blocks(vp_ref, cols, d, seg, 1, BF16), vb, d)
            s_p = jnp.where(m_prev, _dot(qb, kp, B_NT) * scale, NEG)
            s_c = jnp.where(m_cur, _dot(qb, kb, B_NT) * scale, NEG)
            m = jnp.maximum(jnp.max(s_p, axis=-1, keepdims=True), jnp.max(s_c, axis=-1, keepdims=True))
            l = jnp.sum(jnp.exp(s_p - m), axis=-1, keepdims=True) + jnp.sum(jnp.exp(s_c - m), axis=-1, keepdims=True)
            lse = m + jnp.log(l)
            o = _dot(jnp.exp(s_p - lse).astype(BF16), vp, B_NN) + _dot(jnp.exp(s_c - lse).astype(BF16), vb, B_NN)
            _dil_unblocks(o_ref, cols, d, seg, n_seg, o)
            _dil_unblocks(lse_ref, cols, d, seg, n_seg, jnp.broadcast_to(lse, (nb, DIL_BLK, HEAD_DIM)))

    cur = pl.BlockSpec((tr, W), lambda n, cb: (n, cb))
    prv = pl.BlockSpec((seg, W), lambda n, cb: (jnp.maximum(n * n_seg - 1, 0), cb))
    shp = jax.ShapeDtypeStruct((S, DIL_W), F32)
    return pl.pallas_call(body, name=f"dil_fwd_{d}", grid=(S // tr, HEADS // DIL_HB),
                          in_specs=[cur, cur, cur, prv, prv], out_specs=[cur, cur], out_shape=[shp, shp],
                          compiler_params=_params(("arbitrary", "arbitrary")))(q, k, v, k, v)


def _dilated_bwd(q, k, v, do, lse, dd, d):
    S = q.shape[0]
    seg, tr, n_seg = _dil_geometry(S, d)
    nb = n_seg * d
    nt = S // tr
    W = DIL_HB_BWD * HEAD_DIM
    scale = 1.0 / math.sqrt(HEAD_DIM)

    def body(q_ref, k_ref, v_ref, do_ref, lse_ref, dd_ref, kp_ref, vp_ref, qn_ref, don_ref, lsen_ref, ddn_ref,
             dq_ref, dk_ref, dv_ref):
        n = pl.program_id(0)
        m_prev, m_cur = _dil_band(nb, d, n == 0)
        m_next = _dil_band(d, d, n == nt - 1)[0]

        def pair(qb, kb, vb, dob, lse_b, dd_b, mask):
            pr = jnp.where(mask, jnp.exp(_dot(qb, kb, B_NT) * scale - lse_b), 0.0)
            ds = (pr * (_dot(dob, vb, B_NT) - dd_b) * scale).astype(BF16)
            return pr.astype(BF16), ds

        for hb in range(DIL_HB_BWD):
            cols = slice(hb * HEAD_DIM, (hb + 1) * HEAD_DIM)
            stat = cols
            blocks = lambda ref, c, n_s, dt: _dil_blocks(ref, c, d, seg, n_s, dt)
            qb, kb, vb, dob = (blocks(r, cols, n_seg, BF16) for r in (q_ref, k_ref, v_ref, do_ref))
            lse_b, dd_b = blocks(lse_ref, stat, n_seg, F32), blocks(dd_ref, stat, n_seg, F32)
            kp = _with_before(blocks(kp_ref, cols, 1, BF16), kb, d)
            vp = _with_before(blocks(vp_ref, cols, 1, BF16), vb, d)
            p_c, ds_c = pair(qb, kb, vb, dob, lse_b, dd_b, m_cur)
            p_p, ds_p = pair(qb, kp, vp, dob, lse_b, dd_b, m_prev)
            qn, don = blocks(qn_ref, cols, 1, BF16), blocks(don_ref, cols, 1, BF16)
            p_n, ds_n = pair(qn, kb[nb - d:], vb[nb - d:], don, blocks(lsen_ref, stat, 1, F32),
                             blocks(ddn_ref, stat, 1, F32), m_next)
            dq = _dot(ds_c, kb, B_NN) + _dot(ds_p, kp, B_NN)
            shift = lambda own, nxt: nxt if nb == d else jnp.concatenate([own[d:], nxt], axis=0)
            dk = _dot(ds_c, qb, B_TN) + shift(_dot(ds_p, qb, B_TN), _dot(ds_n, qn, B_TN))
            dv = _dot(p_c, dob, B_TN) + shift(_dot(p_p, dob, B_TN), _dot(p_n, don, B_TN))
            _dil_unblocks(dq_ref, cols, d, seg, n_seg, dq)
            _dil_unblocks(dk_ref, cols, d, seg, n_seg, dk)
            _dil_unblocks(dv_ref, cols, d, seg, n_seg, dv)

    cur = pl.BlockSpec((tr, W), lambda n, cb: (n, cb))
    prv = pl.BlockSpec((seg, W), lambda n, cb: (jnp.maximum(n * n_seg - 1, 0), cb))
    nxt = pl.BlockSpec((seg, W), lambda n, cb: (jnp.minimum((n + 1) * n_seg, S // seg - 1), cb))
    shp = jax.ShapeDtypeStruct((S, DIL_W), F32)
    return pl.pallas_call(body, name=f"dil_bwd_{d}", grid=(nt, HEADS // DIL_HB_BWD),
                          in_specs=[cur] * 6 + [prv, prv] + [nxt] * 4, out_specs=[cur] * 3, out_shape=[shp] * 3,
                          compiler_params=_params(("arbitrary", "arbitrary")))(q, k, v, do, lse, dd, k, v, q, do, lse, dd)


def _dil_merge(os_, lses):
    S = os_[0].shape[0]
    tm = _tile(S, 256, 8)

    def body(o0, o1, o2, l0, l1, l2, out_ref):
        ls = [l0[...], l1[...], l2[...]]
        m = jnp.maximum(jnp.maximum(ls[0], ls[1]), ls[2])
        es = [jnp.exp(l - m) for l in ls]
        den = es[0] + es[1] + es[2]
        out_ref[...] = ((es[0] * o0[...] + es[1] * o1[...] + es[2] * o2[...]) / den).astype(BF16)

    ins = [('row', a, DIL_W, 0) for a in list(os_) + list(lses)]
    return _rows("dil_merge", body, S, tm, ins, [(DIL_W, BF16)])[0]


def _dil_merge_bwd(dout, os_, lses):
    S = dout.shape[0]
    tm = _tile(S, 256, 8)

    def body(d_ref, o0, o1, o2, l0, l1, l2, do0, do1, do2, dd0, dd1, dd2):
        dv = d_ref[...]
        os3 = [o0[...], o1[...], o2[...]]
        ls = [l0[...], l1[...], l2[...]]
        m = jnp.maximum(jnp.maximum(ls[0], ls[1]), ls[2])
        es = [jnp.exp(l - m) for l in ls]
        den = es[0] + es[1] + es[2]
        ws = [e / den for e in es]
        for h in range(HEADS):
            cols = slice(h * HEAD_DIM, (h + 1) * HEAD_DIM)
            dw = [jnp.sum(dv[:, cols] * o[:, cols], axis=-1, keepdims=True) for o in os3]
            wh = [w[:, cols] for w in ws]
            mean = wh[0] * dw[0] + wh[1] * dw[1] + wh[2] * dw[2]
            for p_, (do_ref, dd_ref) in enumerate(((do0, dd0), (do1, dd1), (do2, dd2))):
                do_p = wh[p_] * dv[:, cols]
                dlse = wh[p_] * (dw[p_] - mean)
                do_ref[:, cols] = do_p
                dd_ref[:, cols] = wh[p_] * dw[p_] - dlse

    ins = [('row', a, DIL_W, 0) for a in [dout] + list(os_) + list(lses)]
    outs = _rows("dil_merge_bwd", body, S, tm, ins, [(DIL_W, F32)] * 6)
    return outs[:3], outs[3:]


def _qkv_prep(proj, gq, gkv, tabs):
    S = proj.shape[0]
    tm = _tile(S, 256, 8)
    cos_a, sin_a, cos_p, sin_p = tabs

    def body(cq, ckv, qd, kd, vd, kr, gq_ref, gkv_ref, ca, sa, cp, sp, o_cq, o_ckv, o_qd, o_kd, o_vd, o_kr):
        x = cq[...]
        o_cq[...] = (x * _rms_scale(x) * gq_ref[...]).astype(BF16)
        x = ckv[...]
        o_ckv[...] = (x * _rms_scale(x) * gkv_ref[...]).astype(BF16)
        c, s = cp[...], sp[...]
        for h in range(HEADS):
            cols = slice(h * HEAD_DIM, (h + 1) * HEAD_DIM)
            o_qd[:, cols] = _rope(qd[:, cols], c, s, PART_ROPE // 2)
            o_kd[:, cols] = _rope(kd[:, cols], c, s, PART_ROPE // 2)
        o_vd[...] = vd[...]
        o_kr[...] = _rope(kr[...], ca[...], sa[...], ROPE_DIM // 2).astype(BF16)

    ins = [('row', proj, Q_RANK, 0), ('row', proj, KV_RANK, 1), ('row', proj, DIL_W, 1), ('row', proj, DIL_W, 2),
           ('row', proj, DIL_W, 3), ('row', proj, LANE, 4 * DIL_W // LANE), ('full', gq), ('full', gkv),
           ('row', cos_a, LANE, 0), ('row', sin_a, LANE, 0), ('row', cos_p, LANE, 0), ('row', sin_p, LANE, 0)]
    return _rows("qkv_prep", body, S, tm, ins,
                 [(Q_RANK, BF16), (KV_RANK, BF16), (DIL_W, F32), (DIL_W, F32), (DIL_W, F32), (LANE, BF16)])


def _qk_finish(qa, kv, kr, tabs):
    S = qa.shape[0]
    tm = _tile(S, 256, 8)
    cos_a, sin_a = tabs[0], tabs[1]

    def body(qa_ref, kn_ref, kr_ref, ca, sa, q_ref, k_ref):
        c, s = ca[...], sa[...]
        krv = kr_ref[...]
        for h in range(HEADS):
            nope = slice(h * QK_PAD, h * QK_PAD + HEAD_DIM)
            rope = slice(h * QK_PAD + HEAD_DIM, (h + 1) * QK_PAD)
            q_ref[:, nope] = qa_ref[:, nope].astype(BF16)
            q_ref[:, rope] = _rope(qa_ref[:, rope], c, s, ROPE_DIM // 2).astype(BF16)
            k_ref[:, nope] = kn_ref[:, h * HEAD_DIM:(h + 1) * HEAD_DIM]
            k_ref[:, rope] = krv

    W = HEADS * QK_PAD
    ins = [('row', qa, W, 0), ('row', kv, DIL_W, 0), ('row', kr, LANE, 0), ('row', cos_a, LANE, 0), ('row', sin_a, LANE, 0)]
    return _rows("qk_finish", body, S, tm, ins, [(W, BF16), (W, BF16)])


def _qk_finish_bwd(dq, dk, dv, tabs):
    S = dq.shape[0]
    tm = _tile(S, 256, 8)
    cos_a, sin_a = tabs[0], tabs[1]

    def body(dq_ref, dk_ref, dv_ref, ca, sa, dqa_ref, dkv_ref, dkr_ref):
        c, s = ca[...], sa[...]
        krsum = jnp.zeros((tm, LANE), F32)
        for h in range(HEADS):
            nope = slice(h * QK_PAD, h * QK_PAD + HEAD_DIM)
            rope = slice(h * QK_PAD + HEAD_DIM, (h + 1) * QK_PAD)
            dqa_ref[:, nope] = dq_ref[:, nope].astype(BF16)
            dqa_ref[:, rope] = _rope_t(dq_ref[:, rope], c, s, ROPE_DIM // 2).astype(BF16)
            dkv_ref[:, h * HEAD_DIM:(h + 1) * HEAD_DIM] = dk_ref[:, nope].astype(BF16)
            krsum = krsum + dk_ref[:, rope]
        dkv_ref[:, DIL_W:] = dv_ref[...]
        dkr_ref[...] = _rope_t(krsum, c, s, ROPE_DIM // 2)

    W = HEADS * QK_PAD
    ins = [('row', dq, W, 0), ('row', dk, W, 0), ('row', dv, DIL_W, 0), ('row', cos_a, LANE, 0), ('row', sin_a, LANE, 0)]
    return _rows("qk_finish_bwd", body, S, tm, ins, [(W, BF16), (2 * DIL_W, BF16), (LANE, F32)])


def _qkv_prep_bwd(proj, gq, gkv, d_cqn, d_ckvn, dqs, dks, dvs, dkr, tabs):
    S = proj.shape[0]
    tm = _tile(S, 256, 8)
    cos_p, sin_p = tabs[2], tabs[3]

    def body(cq, ckv, gq_ref, gkv_ref, dcq, dckv, dq0, dq1, dq2, dk0, dk1, dk2, dv0, dv1, dv2, dkr_ref, cp, sp,
             out_ref, dgq_ref, dgkv_ref):
        dx, dg = _rms_bwd(cq[...], gq_ref[...], dcq[...])
        out_ref[:, 0:Q_RANK] = dx.astype(BF16)
        _acc_add(dgq_ref, jnp.sum(dg, axis=0, keepdims=True))
        dx, dg = _rms_bwd(ckv[...], gkv_ref[...], dckv[...])
        out_ref[:, Q_RANK:Q_RANK + KV_RANK] = dx.astype(BF16)
        _acc_add(dgkv_ref, jnp.sum(dg, axis=0, keepdims=True))
        c, s = cp[...], sp[...]
        base = Q_RANK + KV_RANK
        for h in range(HEADS):
            cols = slice(h * HEAD_DIM, (h + 1) * HEAD_DIM)
            dst = lambda part: slice(base + part * DIL_W + h * HEAD_DIM, base + part * DIL_W + (h + 1) * HEAD_DIM)
            out_ref[:, dst(0)] = _rope_t(dq0[:, cols] + dq1[:, cols] + dq2[:, cols], c, s, PART_ROPE // 2).astype(BF16)
            out_ref[:, dst(1)] = _rope_t(dk0[:, cols] + dk1[:, cols] + dk2[:, cols], c, s, PART_ROPE // 2).astype(BF16)
            out_ref[:, dst(2)] = (dv0[:, cols] + dv1[:, cols] + dv2[:, cols]).astype(BF16)
        out_ref[:, base + 3 * DIL_W:] = dkr_ref[...].astype(BF16)

    ins = [('row', proj, Q_RANK, 0), ('row', proj, KV_RANK, 1), ('full', gq), ('full', gkv),
           ('row', d_cqn, Q_RANK, 0), ('row', d_ckvn, KV_RANK, 0)]
    ins += [('row', a, DIL_W, 0) for a in list(dqs) + list(dks) + list(dvs)]
    ins += [('row', dkr, LANE, 0), ('row', cos_p, LANE, 0), ('row', sin_p, LANE, 0)]
    return _rows("qkv_prep_bwd", body, S, tm, ins, [(IN_PAD, BF16)], [Q_RANK, KV_RANK])


def _mix_fwd(x, p, tabs, rider=None):
    h = _rms_cast("mix_prenorm", x, p['pre_g'])
    proj = _mm_nn("mix_proj", h, p['w_in'], F32)
    cqn, ckvn, qd, kd, vd, kr = _qkv_prep(proj, p['q_norm_g'], p['kv_norm_g'], tabs)
    qa = _mm_nn("mla_q_up", cqn, p['w_uq'], F32, tn=1024)
    kv = _mm_nn("mla_kv_up", ckvn, p['w_ukv'], BF16, tn=1024)
    q_cat, k_cat = _qk_finish(qa, kv, kr, tabs)
    (o_a, lse_a), rode = _mla_fwd(q_cat, k_cat, kv, rider)
    o_ps, lse_ps = [], []
    for _, d in DIL_PATTERNS:
        o_p, lse_p = _dilated_fwd(qd, kd, vd, d)
        o_ps.append(o_p)
        lse_ps.append(lse_p)
    o_b = _dil_merge(o_ps, lse_ps)
    o_cat = jnp.stack([o_a, o_b])
    x_out, y = _chunk_post("mix_out", o_cat, p['w_o'].reshape(2, DIL_W, -1), x, p['post_g'], 1.0)
    return x_out, (x, h, proj, cqn, ckvn, qd, kd, vd, q_cat, k_cat, kv, o_a, lse_a, o_ps, lse_ps, o_cat, y), rode


def _mix_bwd(dxo, p, tabs, saved, rider=None):
    x, h, proj, cqn, ckvn, qd, kd, vd, q_cat, k_cat, kv, o_a, lse_a, o_ps, lse_ps, o_cat, y = saved
    dy, d_post = _postnorm_bwd("mix_postnorm_bwd", dxo, y, p['post_g'], 1.0)
    w_o = p['w_o']
    d_oa = _mm_nt("mix_do_a", dy, w_o[:DIL_W], BF16, tk=1024)
    d_ob = _mm_nt("mix_do_b", dy, w_o[DIL_W:], F32, tk=1024)
    d_wo = _mm_tn_chunks_a("mix_dwo", o_cat, dy, BF16).reshape(w_o.shape)
    stats = _mla_stats(d_oa, o_a, lse_a)
    (dq, dk, dv), rode = _mla_bwd(q_cat, k_cat, kv, d_oa, stats, rider)
    dqa, dkv, dkr = _qk_finish_bwd(dq, dk, dv, tabs)
    d_cqn = _mm_nt("mla_dcq", dqa, p['w_uq'], F32, tk=1024)
    d_ckvn = _mm_nt("mla_dckv", dkv, p['w_ukv'], F32, tk=1024)
    d_wuq = _mm_tn("mla_dwuq", cqn, dqa, BF16, tn=1024)
    d_wukv = _mm_tn("mla_dwukv", ckvn, dkv, BF16, tn=1024)
    do_ps, dd_ps = _dil_merge_bwd(d_ob, o_ps, lse_ps)
    dqs, dks, dvs = [], [], []
    for (_, d), do_p, lse_p, dd_p in zip(DIL_PATTERNS, do_ps, lse_ps, dd_ps):
        dq_p, dk_p, dv_p = _dilated_bwd(qd, kd, vd, do_p, lse_p, dd_p, d)
        dqs.append(dq_p)
        dks.append(dk_p)
        dvs.append(dv_p)
    d_proj, d_gq, d_gkv = _qkv_prep_bwd(proj, p['q_norm_g'], p['kv_norm_g'], d_cqn, d_ckvn, dqs, dks, dvs, dkr, tabs)
    dh = _mm_nt("mix_dh", d_proj, p['w_in'], F32)
    d_win = _mm_tn("mix_dwin", h, d_proj, BF16)
    dx, d_pre = _prenorm_bwd("mix_prenorm_bwd", dh, x, p['pre_g'], dxo)
    return dx, dict(pre_g=d_pre, post_g=d_post, w_in=d_win, q_norm_g=d_gq, w_uq=d_wuq, kv_norm_g=d_gkv,
                    w_ukv=d_wukv, w_o=d_wo), rode


def _cols_from_shards(g):
    return jnp.transpose(g, (1, 0, 2)).reshape(g.shape[1], -1)


def _shards_from_cols(w):
    K = w.shape[0]
    return jnp.transpose(w.reshape(K, N_DEV, -1), (1, 0, 2))


def _win_layout(g):
    w = _cols_from_shards(g)
    a = Q_RANK + KV_RANK
    return jnp.concatenate([w[:, :a], w[:, a + ROPE_DIM:], w[:, a:a + ROPE_DIM],
                            jnp.zeros((w.shape[0], IN_PAD - IN_COLS), w.dtype)], axis=1)


def _win_unlayout(dw):
    a = Q_RANK + KV_RANK
    w = jnp.concatenate([dw[:, :a], dw[:, a + 3 * DIL_W:a + 3 * DIL_W + ROPE_DIM], dw[:, a:a + 3 * DIL_W]], axis=1)
    return _shards_from_cols(w)


def _wuq_layout(g):
    return _cols_from_shards(jnp.pad(g, ((0, 0), (0, 0), (0, QK_PAD - HEAD_DIM - ROPE_DIM))))


def _wuq_unlayout(dw):
    return _shards_from_cols(dw)[:, :, :HEAD_DIM + ROPE_DIM]


def _wukv_layout(g):
    return jnp.concatenate([_cols_from_shards(g[:, :, :HEAD_DIM]), _cols_from_shards(g[:, :, HEAD_DIM:])], axis=1)


def _wukv_unlayout(dw):
    return jnp.concatenate([_shards_from_cols(dw[:, :DIL_W]), _shards_from_cols(dw[:, DIL_W:])], axis=2)


def _cast_bf16(x):
    shp = x.shape
    x2 = x.reshape(-1, shp[-1])
    R, C = x2.shape
    tr = _tile(R, 512, 8)

    def body(x_ref, o_ref):
        o_ref[...] = x_ref[...].astype(BF16)

    spec = pl.BlockSpec((tr, C), lambda i: (i, 0))
    out = pl.pallas_call(body, name="cast_bf16", grid=(R // tr,), in_specs=[spec], out_specs=spec,
                         out_shape=jax.ShapeDtypeStruct((R, C), BF16), compiler_params=_params(("arbitrary",)))(x2)
    return out.reshape(shp)


def _mesh_pos():
    x, y, c = lax.axis_index("x"), lax.axis_index("y"), lax.axis_index("c")
    return x, y, c


class _Rider:
    def __init__(self, name, arrays, out_shapes, scratch, start, finish):
        self.name, self.arrays, self.out_shapes, self.scratch = name, arrays, out_shapes, scratch
        self.start, self.finish = start, finish
        self.n = len(arrays)

    def split(self, refs):
        return refs[:self.n], refs[self.n:2 * self.n], refs[2 * self.n:]


def _run_rider(rider):
    def body(*refs):
        rider.start(*rider.split(refs))
        rider.finish(*rider.split(refs))

    any_spec = pl.BlockSpec(memory_space=pl.ANY)
    return pl.pallas_call(body, name=rider.name, in_specs=[any_spec] * rider.n, out_specs=[any_spec] * rider.n,
                          out_shape=rider.out_shapes, scratch_shapes=rider.scratch)(*rider.arrays)


def _ride(rider, n_in, n_out, n_scr, first, last, compute):
    def body(*refs):
        r = 0 if rider is None else rider.n
        o0 = n_in + r
        s0 = o0 + n_out + r
        own = (refs[:n_in], refs[o0:o0 + n_out], refs[s0:s0 + n_scr])
        if rider is None:
            compute(*own)
            return
        mine = (refs[n_in:o0], refs[o0 + n_out:s0], refs[s0 + n_scr:])

        @pl.when(first())
        def _():
            rider.start(*mine)

        compute(*own)

        @pl.when(last())
        def _():
            rider.finish(*mine)

    return body


def _gather_rider(xs):
    n = len(xs)

    def program(x_refs, o_refs, sems):
        send, recv, loc = sems
        x, y, c = _mesh_pos()
        me, sib = (x, y, c), (x, y, 1 - c)
        chips = [(1 - x, y), (x, 1 - y), (1 - x, 1 - y)]

        def slot(k, dev):
            return o_refs[k].at[4 * dev[0] + 2 * dev[1] + dev[2]]

        def copy(k, s, block, to, src=None):
            return pltpu.make_async_remote_copy(src_ref=slot(k, block) if src is None else src, dst_ref=slot(k, block),
                                                send_sem=send.at[k, s], recv_sem=recv.at[k, s],
                                                device_id=to, device_id_type=MESH)

        mine = [pltpu.make_async_copy(x_refs[k], slot(k, me), loc.at[k]) for k in range(n)]
        first = []
        for k in range(n):
            first.append(copy(k, 0, me, sib, src=x_refs[k]))
            first += [copy(k, 1 + j, me, (*chip, c), src=x_refs[k]) for j, chip in enumerate(chips)]

        def start():
            for cp in mine + first:
                cp.start()

        def finish():
            passed = []
            for k in range(n):
                for j, chip in enumerate(chips):
                    copy(k, 1 + j, (*chip, c), me).wait_recv()
                    fwd = copy(k, 4 + j, (*chip, c), sib)
                    fwd.start()
                    passed.append(fwd)
            for k in range(n):
                copy(k, 0, sib, me).wait_recv()
                for j, chip in enumerate(chips):
                    copy(k, 4 + j, (*chip, 1 - c), me).wait_recv()
            for cp in first + passed:
                cp.wait_send()
            for cp in mine:
                cp.wait()

        return start, finish

    return _Rider("all_gather", list(xs), [jax.ShapeDtypeStruct((N_DEV,) + a.shape, a.dtype) for a in xs],
                  [pltpu.SemaphoreType.DMA((n, 7)), pltpu.SemaphoreType.DMA((n, 7)), pltpu.SemaphoreType.DMA((n,))],
                  lambda *refs: program(*refs)[0](), lambda *refs: program(*refs)[1]())


def _exchange_rider(gs):
    n = len(gs)

    def program(g_refs, r_refs, sems):
        send, recv, loc = sems
        x, y, c = _mesh_pos()
        me = 4 * x + 2 * y + c
        mine = [pltpu.make_async_copy(g_refs[k].at[me], r_refs[k].at[me], loc.at[k]) for k in range(n)]
        out_cps, in_cps = [], []
        for k in range(n):
            for m in range(1, N_DEV):
                px = 1 - x if m & 4 else x
                py = 1 - y if m & 2 else y
                pc = 1 - c if m & 1 else c
                peer = 4 * px + 2 * py + pc
                kw = dict(src_ref=g_refs[k].at[peer], send_sem=send.at[k, m - 1], recv_sem=recv.at[k, m - 1],
                          device_id=(px, py, pc), device_id_type=MESH)
                out_cps.append(pltpu.make_async_remote_copy(dst_ref=r_refs[k].at[me], **kw))
                in_cps.append(pltpu.make_async_remote_copy(dst_ref=r_refs[k].at[peer], **kw))

        def start():
            for cp in mine + out_cps:
                cp.start()

        def finish():
            for cp in in_cps:
                cp.wait_recv()
            for cp in out_cps:
                cp.wait_send()
            for cp in mine:
                cp.wait()

        return start, finish

    return _Rider("grad_exchange", list(gs), [jax.ShapeDtypeStruct(a.shape, a.dtype) for a in gs],
                  [pltpu.SemaphoreType.DMA((n, 7)), pltpu.SemaphoreType.DMA((n, 7)), pltpu.SemaphoreType.DMA((n,))],
                  lambda *refs: program(*refs)[0](), lambda *refs: program(*refs)[1]())


def _all_reduce_small(v):
    R, C = v.shape

    def body(v_ref, o_ref, buf, send, recv):
        x, y, c = _mesh_pos()
        me = 4 * x + 2 * y + c
        buf[me] = v_ref[...]
        copies = []
        for m in range(1, N_DEV):
            px = 1 - x if m & 4 else x
            py = 1 - y if m & 2 else y
            pc = 1 - c if m & 1 else c
            peer = 4 * px + 2 * py + pc
            copies.append((pltpu.make_async_remote_copy(
                src_ref=v_ref, dst_ref=buf.at[me], send_sem=send.at[m - 1], recv_sem=recv.at[m - 1],
                device_id=(px, py, pc), device_id_type=MESH),
                pltpu.make_async_remote_copy(
                src_ref=v_ref, dst_ref=buf.at[peer], send_sem=send.at[m - 1], recv_sem=recv.at[m - 1],
                device_id=(px, py, pc), device_id_type=MESH)))
        for out_cp, _ in copies:
            out_cp.start()
        for _, in_cp in copies:
            in_cp.wait_recv()
        for out_cp, _ in copies:
            out_cp.wait_send()
        total = buf[0]
        for s in range(1, N_DEV):
            total = total + buf[s]
        o_ref[...] = total

    vmem = pl.BlockSpec(memory_space=pltpu.VMEM)
    return pl.pallas_call(
        body, name="all_reduce_small", in_specs=[vmem], out_specs=vmem, out_shape=jax.ShapeDtypeStruct((R, C), F32),
        scratch_shapes=[pltpu.VMEM((N_DEV, R, C), F32), pltpu.SemaphoreType.DMA((7,)), pltpu.SemaphoreType.DMA((7,))],
    )(v)


def _adamw(name, parts, w, m, v):
    L, R, C = w.shape
    P = parts[0].shape[0]
    tr = _tile(R, max(16, ADAMW_TILE_ELEMS // C), 16)
    nr = R // tr

    def body(*refs):
        p_refs, (w_ref, m_ref, v_ref), (g_out, d_out, m_out, v_out) = refs[:L], refs[L:L + 3], refs[L + 3:]

        def update(p_ref):
            g = p_ref[0].astype(F32)
            for s in range(1, P):
                g = g + p_ref[s].astype(F32)
            m_new = ADAM_B1 * m_ref[...] + (1.0 - ADAM_B1) * g
            v_new = ADAM_B2 * v_ref[...] + (1.0 - ADAM_B2) * (g * g)
            m_hat = m_new / (1.0 - ADAM_B1 ** ADAM_STEP)
            v_hat = v_new / (1.0 - ADAM_B2 ** ADAM_STEP)
            g_out[...] = g
            d_out[...] = -ADAM_LR * (m_hat / (jnp.sqrt(v_hat) + ADAM_EPS) + ADAM_WD * w_ref[...])
            m_out[...] = m_new
            v_out[...] = v_new

        for ll in range(L):
            pl.when(pl.program_id(0) == ll)(functools.partial(update, p_refs[ll]))

    def part_spec(ll):
        def index(l, i):
            return (0, jnp.where(l == ll, i, jnp.where(l > ll, nr - 1, 0)), 0)
        return pl.BlockSpec((P, tr, C), index)

    spec = pl.BlockSpec((None, tr, C), lambda l, i: (l, i, 0))
    shp = jax.ShapeDtypeStruct((L, R, C), F32)
    return pl.pallas_call(body, name=name, grid=(L, nr),
                          in_specs=[part_spec(ll) for ll in range(L)] + [spec, spec, spec],
                          out_specs=[spec] * 4, out_shape=[shp] * 4,
                          compiler_params=_params(("arbitrary", "arbitrary")))(*parts, w, m, v)


def _rope_tables(positions):
    pos = positions.reshape(-1).astype(F32)[:, None]
    S = pos.shape[0]

    def cs(dim):
        inv = ROPE_THETA ** (-jnp.arange(0, dim, 2, dtype=F32) / dim)
        ang = pos * inv
        return jnp.cos(ang), jnp.sin(ang)

    ca, sa = cs(ROPE_DIM)
    cp, sp = cs(PART_ROPE)
    z = lambda w: jnp.zeros((S, w), F32)
    return (jnp.concatenate([ca, ca, z(LANE - ROPE_DIM)], axis=1), jnp.concatenate([sa, sa, z(LANE - ROPE_DIM)], axis=1),
            jnp.concatenate([cp, cp, jnp.ones((S, LANE - PART_ROPE), F32)], axis=1),
            jnp.concatenate([sp, sp, z(LANE - PART_ROPE)], axis=1))


def _ffn_params(tag, gathered, gains, l):
    row = lambda n: gains[n][l][None, :]
    return dict(pre_g=row(tag + '_pre_g'), post_g=row(tag + '_post_g'), w_gate=gathered[0], w_up=gathered[1],
                w_down=gathered[2])


def _mix_params(gathered, gains, l):
    row = lambda n: gains[n][l][None, :]
    w_in, w_uq, w_ukv, w_o = gathered
    return dict(pre_g=row('mix_pre_g'), post_g=row('mix_post_g'), q_norm_g=row('mla_q_norm_g'),
                kv_norm_g=row('mla_kv_norm_g'), w_in=_win_layout(w_in), w_uq=_wuq_layout(w_uq),
                w_ukv=_wukv_layout(w_ukv), w_o=w_o.reshape(-1, w_o.shape[-1]))


def _gain_grads(d1, dm, d2):
    return dict(ffn1_pre_g=d1['pre_g'], ffn1_post_g=d1['post_g'], mix_pre_g=dm['pre_g'], mix_post_g=dm['post_g'],
                mla_q_norm_g=dm['q_norm_g'], mla_kv_norm_g=dm['kv_norm_g'], ffn2_pre_g=d2['pre_g'], ffn2_post_g=d2['post_g'])


def _pack(vecs, width):
    flat = jnp.concatenate([v.reshape(-1) for v in vecs])
    per = 8 * width
    flat = jnp.pad(flat, (0, (-flat.shape[0]) % per))
    return flat.reshape(-1, width)


def _unpack(packed, shapes):
    flat = packed.reshape(-1)
    out, off = [], 0
    for shp in shapes:
        size = math.prod(shp)
        out.append(flat[off:off + size].reshape(shp))
        off += size
    return out


def kernel(x, positions, ffn1_pre_g, ffn1_post_g, ffn1_w_gate, ffn1_w_up, ffn1_w_down, mix_pre_g, mix_post_g, w_in, mla_q_norm_g, mla_w_uq, mla_kv_norm_g, mla_w_ukv, w_o, ffn2_pre_g, ffn2_post_g, ffn2_w_gate, ffn2_w_up, ffn2_w_down, loss_target, m_ffn1_pre_g, m_ffn1_post_g, m_ffn1_w_gate, m_ffn1_w_up, m_ffn1_w_down, m_mix_pre_g, m_mix_post_g, m_w_in, m_mla_q_norm_g, m_mla_w_uq, m_mla_kv_norm_g, m_mla_w_ukv, m_w_o, m_ffn2_pre_g, m_ffn2_post_g, m_ffn2_w_gate, m_ffn2_w_up, m_ffn2_w_down, v_ffn1_pre_g, v_ffn1_post_g, v_ffn1_w_gate, v_ffn1_w_up, v_ffn1_w_down, v_mix_pre_g, v_mix_post_g, v_w_in, v_mla_q_norm_g, v_mla_w_uq, v_mla_kv_norm_g, v_mla_w_ukv, v_w_o, v_ffn2_pre_g, v_ffn2_post_g, v_ffn2_w_gate, v_ffn2_w_up, v_ffn2_w_down):
    w = dict(zip(WNAMES, (ffn1_pre_g, ffn1_post_g, ffn1_w_gate, ffn1_w_up, ffn1_w_down, mix_pre_g, mix_post_g, w_in,
                          mla_q_norm_g, mla_w_uq, mla_kv_norm_g, mla_w_ukv, w_o, ffn2_pre_g, ffn2_post_g,
                          ffn2_w_gate, ffn2_w_up, ffn2_w_down)))
    mom = dict(zip(WNAMES, (m_ffn1_pre_g, m_ffn1_post_g, m_ffn1_w_gate, m_ffn1_w_up, m_ffn1_w_down, m_mix_pre_g,
                            m_mix_post_g, m_w_in, m_mla_q_norm_g, m_mla_w_uq, m_mla_kv_norm_g, m_mla_w_ukv, m_w_o,
                            m_ffn2_pre_g, m_ffn2_post_g, m_ffn2_w_gate, m_ffn2_w_up, m_ffn2_w_down)))
    var = dict(zip(WNAMES, (v_ffn1_pre_g, v_ffn1_post_g, v_ffn1_w_gate, v_ffn1_w_up, v_ffn1_w_down, v_mix_pre_g,
                            v_mix_post_g, v_w_in, v_mla_q_norm_g, v_mla_w_uq, v_mla_kv_norm_g, v_mla_w_ukv, v_w_o,
                            v_ffn2_pre_g, v_ffn2_post_g, v_ffn2_w_gate, v_ffn2_w_up, v_ffn2_w_down)))
    depth = w_in.shape[0]
    xs = x[0]
    D = xs.shape[1]
    tabs = _rope_tables(positions)

    shards = {n: _cast_bf16(w[n]) for n in BIG}
    ffn1_w, mix_w, ffn2_w = BIG[:3], BIG[3:7], BIG[7:]
    local = lambda names, l: [shards[n][l] for n in names]
    p1 = _ffn_params('ffn1', _run_rider(_gather_rider(local(ffn1_w, 0))), w, 0)
    pm = p2 = None

    params, saved = [], []
    act = xs
    for l in range(depth):
        more = l + 1 < depth
        act, s1, got = _ffn_fwd(act, p1, _gather_rider(local(mix_w, 0) if l == 0 else local(ffn2_w, l)))
        if l == 0:
            pm = _mix_params(got, w, 0)
        else:
            p2 = _ffn_params('ffn2', got, w, l)
        riding_w = (local(ffn1_w, l + 1) if more else []) + (local(ffn2_w, 0) if l == 0 else [])
        act, sm, got = _mix_fwd(act, pm, tabs, _gather_rider(riding_w) if riding_w else None)
        got = list(got)
        if more:
            p1_next, got = _ffn_params('ffn1', got[:3], w, l + 1), got[3:]
        if l == 0:
            p2 = _ffn_params('ffn2', got, w, 0)
        act, s2, got = _ffn_fwd(act, p2, _gather_rider(local(mix_w, l + 1)) if more else None)
        params.append((p1, pm, p2))
        saved.append((s1, sm, s2))
        if more:
            p1, pm = p1_next, _mix_params(got, w, l + 1)
    dact, loss_part = _loss_head(act, loss_target[0])

    received = [[None] * len(BIG) for _ in range(depth)]
    gain_parts = [None] * depth
    arrays = lambda items: [a for _, _, a in items]

    def keep(items, got):
        for (ll, i, _), r in zip(items, got or ()):
            received[ll][i] = r

    on_ffn, on_attn = [], []
    for l in reversed(range(depth)):
        p1, pm, p2 = params[l]
        s1, sm, s2 = saved[l]
        dact, d2, got, _ = _ffn_bwd(dact, p2, s2, arrays(on_ffn))
        keep(on_ffn, got)
        on_attn = [(l, 7, d2['w_gate']), (l, 8, d2['w_up']), (l, 9, d2['w_down'])] + on_attn
        dact, dm, got = _mix_bwd(dact, pm, tabs, sm, _exchange_rider(arrays(on_attn)))
        keep(on_attn, got)
        on_ffn = [(l, 3, _win_unlayout(dm['w_in'])), (l, 4, _wuq_unlayout(dm['w_uq'])),
                  (l, 5, _wukv_unlayout(dm['w_ukv'])), (l, 6, dm['w_o'].reshape(N_DEV, -1, dm['w_o'].shape[-1]))]
        dact, d1, got, own = _ffn_bwd(dact, p1, s1, arrays(on_ffn), drain=(l == 0))
        keep(on_ffn, got)
        if own is not None:
            received[l][0], received[l][1], received[l][2] = own['w_gate'], own['w_up'], own['w_down']
        on_ffn = [(l, 0, d1['w_gate']), (l, 1, d1['w_up'])]
        on_attn = [(l, 2, d1['w_down'])]
        gain_parts[l] = _gain_grads(d1, dm, d2)

    gain_local = [jnp.stack([gain_parts[l][n].reshape(-1) for l in range(depth)]) for n in GAINS]
    packed = _all_reduce_small(_pack(gain_local + [loss_part.reshape(1)], D))
    summed = _unpack(packed, [w[n].shape for n in GAINS] + [(1,)])
    loss = summed[-1][0]

    out = {}
    for i, n in enumerate(BIG):
        out[n] = _adamw("adamw_" + n, [received[l][i] for l in range(depth)], w[n], mom[n], var[n])
    pk = lambda d: _pack([d[n] for n in GAINS], D)[None]
    res = _adamw("adamw_gains", [_pack(summed[:-1], D)[None]], pk(w), pk(mom), pk(var))
    for t in range(4):
        for n, a in zip(GAINS, _unpack(res[t], [w[n].shape for n in GAINS])):
            out.setdefault(n, [None] * 4)[t] = a

    grads = [out[n][0] for n in WNAMES]
    deltas = [out[n][1] for n in WNAMES]
    new_m = [out[n][2] for n in WNAMES]
    new_v = [out[n][3] for n in WNAMES]
    return (loss, dact[None], *grads, *deltas, *new_m, *new_v)
```

```python
import functools
import math

import jax
import jax.numpy as jnp
from jax import lax
from jax.experimental import pallas as pl
from jax.experimental.pallas import tpu as pltpu

F32 = jnp.float32
BF16 = jnp.bfloat16
N_DEV = 8
MESH = pl.DeviceIdType.MESH

HEADS = 8
HEAD_DIM = 128
Q_RANK = 512
KV_RANK = 512
ROPE_DIM = 64
QK_PAD = 256
PART_ROPE = 32
DIL_PATTERNS = ((128, 1), (512, 4), (2048, 16))
ROPE_THETA = 500000.0
RMS_EPS = 1e-6
NEG = -1e30
LANE = 128
IN_COLS = 4160
IN_PAD = 4224
DIL_W = HEADS * HEAD_DIM

ADAM_LR, ADAM_B1, ADAM_B2, ADAM_EPS, ADAM_WD, ADAM_STEP = 0.001, 0.9, 0.999, 1e-08, 0.01, 10

VMEM_LIMIT = 56 * 1024 * 1024
SUB_ROWS = 256
ADAMW_TILE_ELEMS = 128 * 1024

WNAMES = ['ffn1_pre_g', 'ffn1_post_g', 'ffn1_w_gate', 'ffn1_w_up', 'ffn1_w_down', 'mix_pre_g', 'mix_post_g', 'w_in',
          'mla_q_norm_g', 'mla_w_uq', 'mla_kv_norm_g', 'mla_w_ukv', 'w_o', 'ffn2_pre_g', 'ffn2_post_g',
          'ffn2_w_gate', 'ffn2_w_up', 'ffn2_w_down']
BIG = ['ffn1_w_gate', 'ffn1_w_up', 'ffn1_w_down', 'w_in', 'mla_w_uq', 'mla_w_ukv', 'w_o',
       'ffn2_w_gate', 'ffn2_w_up', 'ffn2_w_down']
GAINS = [n for n in WNAMES if n not in BIG]

NT = (((1,), (1,)), ((), ()))
NN = (((1,), (0,)), ((), ()))
TN = (((0,), (0,)), ((), ()))


def _tile(n, target, mult):
    best = None
    t = mult
    while t <= min(n, target):
        if n % t == 0:
            best = t
        t += mult
    return n if best is None else best


def _params(sem=None):
    kw = dict(vmem_limit_bytes=VMEM_LIMIT)
    if sem is not None:
        kw['dimension_semantics'] = sem
    return pltpu.CompilerParams(**kw)


def _dot(a, b, dn):
    return lax.dot_general(a, b, dn, preferred_element_type=F32)


def _mm(name, pairs, pair_specs, dn, grid, k_axis, acc_shape, out_shapes, out_specs, epilogue,
        extras=(), extra_specs=(), rider=None, acc_by_ref=False):
    n_pair = len(pairs)
    nk = 1 if k_axis is None else grid[k_axis]
    assert nk > 1 or not acc_by_ref

    def compute(ins, outs, scr):
        ab, ex = ins[:2 * n_pair], ins[2 * n_pair:]
        part = _dot(ab[0][...], ab[1][...], dn)
        for p in range(1, n_pair):
            part = part + _dot(ab[2 * p][...], ab[2 * p + 1][...], dn)
        if nk == 1:
            epilogue(part, ex, outs)
            return
        acc = scr[0]
        k = pl.program_id(k_axis)

        @pl.when(k == 0)
        def _():
            acc[...] = part

        @pl.when(k > 0)
        def _():
            acc[...] += part

        @pl.when(k == nk - 1)
        def _():
            epilogue(acc if acc_by_ref else acc[...], ex, outs)

    flat, flat_specs = [], []
    for (a, b), (sa, sb) in zip(pairs, pair_specs):
        flat += [a, b]
        flat_specs += [sa, sb]
    outs, rode = _with_rider(rider, name, grid, compute, len(flat) + len(extras), flat_specs + list(extra_specs),
                             out_specs, out_shapes, flat + list(extras),
                             scratch=[pltpu.VMEM(acc_shape, F32)] if nk > 1 else [])
    return outs if rider is None else (outs, rode)


def _store(dtype):
    def epi(acc, ex, outs):
        outs[0][...] = acc.astype(dtype)
    return epi


def _mm_nn(name, a, b, out_dtype, tm=1024, tn=1408):
    M, K = a.shape
    N = b.shape[1]
    tm, tn = _tile(M, tm, 8), _tile(N, tn, LANE)
    return _mm(name, [(a, b)],
               [(pl.BlockSpec((tm, K), lambda j, i: (i, 0)), pl.BlockSpec((K, tn), lambda j, i: (0, j)))],
               NN, (N // tn, M // tm), None, None,
               [jax.ShapeDtypeStruct((M, N), out_dtype)], [pl.BlockSpec((tm, tn), lambda j, i: (i, j))],
               _store(out_dtype))[0]


def _mm_nt(name, a, b, out_dtype, tm=1024, tk=1408):
    M, K = a.shape
    N = b.shape[0]
    tm, tk = _tile(M, tm, 8), _tile(K, tk, LANE)
    return _mm(name, [(a, b)],
               [(pl.BlockSpec((tm, tk), lambda i, k: (i, k)), pl.BlockSpec((N, tk), lambda i, k: (0, k)))],
               NT, (M // tm, K // tk), 1, (tm, N),
               [jax.ShapeDtypeStruct((M, N), out_dtype)], [pl.BlockSpec((tm, N), lambda i, k: (i, 0))],
               _store(out_dtype))[0]


def _mm_tn(name, a, b, out_dtype, ts=1024, tn=1408):
    M, K = a.shape
    N = b.shape[1]
    ts, tn = _tile(M, ts, 16), _tile(N, tn, LANE)
    return _mm(name, [(a, b)],
               [(pl.BlockSpec((ts, K), lambda j, m: (m, 0)), pl.BlockSpec((ts, tn), lambda j, m: (m, j)))],
               TN, (N // tn, M // ts), 1, (K, tn),
               [jax.ShapeDtypeStruct((K, N), out_dtype)], [pl.BlockSpec((K, tn), lambda j, m: (0, j))],
               _store(out_dtype))[0]


def _mm_tn_chunks_a(name, a3, b, out_dtype, ts=2048):
    C, M, Kc = a3.shape
    N = b.shape[1]
    ts = _tile(M, ts, 16)
    return _mm(name, [(a3, b)],
               [(pl.BlockSpec((None, ts, Kc), lambda c, m: (c, m, 0)), pl.BlockSpec((ts, N), lambda c, m: (m, 0)))],
               TN, (C, M // ts), 1, (Kc, N),
               [jax.ShapeDtypeStruct((C, Kc, N), out_dtype)], [pl.BlockSpec((None, Kc, N), lambda c, m: (c, 0, 0))],
               _store(out_dtype))[0]


def _mm_tn_chunks_b(name, a, b3, out_dtype, ts=2048, rider=None):
    M, K = a.shape
    C, _, Nc = b3.shape
    ts = _tile(M, ts, 16)
    res = _mm(name, [(a, b3)],
              [(pl.BlockSpec((ts, K), lambda c, m: (m, 0)), pl.BlockSpec((None, ts, Nc), lambda c, m: (c, m, 0)))],
              TN, (C, M // ts), 1, (K, Nc),
              [jax.ShapeDtypeStruct((C, K, Nc), out_dtype)], [pl.BlockSpec((None, K, Nc), lambda c, m: (c, 0, 0))],
              _store(out_dtype), rider=rider)
    return res[0] if rider is None else (res[0][0], res[1])


def _rows(name, body, n_rows, tm, ins, outs, accs=()):
    in_specs, arrays = [], []
    for spec in ins:
        if spec[0] == 'row':
            _, arr, width, cb = spec
            in_specs.append(pl.BlockSpec((tm, width), functools.partial(lambda i, cb: (i, cb), cb=cb)))
        else:
            arr = spec[1]
            in_specs.append(pl.BlockSpec(arr.shape, functools.partial(lambda i, nd: (0,) * nd, nd=arr.ndim)))
        arrays.append(arr)
    out_shapes = [jax.ShapeDtypeStruct((n_rows, w), dt) for w, dt in outs]
    out_specs = [pl.BlockSpec((tm, w), lambda i: (i, 0)) for w, _ in outs]
    out_shapes += [jax.ShapeDtypeStruct((1, w), F32) for w in accs]
    out_specs += [pl.BlockSpec((1, w), lambda i: (0, 0)) for w in accs]
    return pl.pallas_call(body, name=name, grid=(n_rows // tm,), in_specs=in_specs, out_specs=out_specs,
                          out_shape=out_shapes, compiler_params=_params(("arbitrary",)))(*arrays)


def _acc_add(ref, val):
    @pl.when(pl.program_id(0) == 0)
    def _():
        ref[...] = val

    @pl.when(pl.program_id(0) > 0)
    def _():
        ref[...] += val


def _rms_scale(x):
    return lax.rsqrt(jnp.mean(x * x, axis=-1, keepdims=True) + RMS_EPS)


def _rms_bwd(x, g, dy):
    r = _rms_scale(x)
    t = dy * g
    dx = r * t - x * (r * r * r) * jnp.mean(t * x, axis=-1, keepdims=True)
    return dx, dy * x * r


def _rot_half(x, hw):
    lane = lax.broadcasted_iota(jnp.int32, x.shape, 1)
    left = pltpu.roll(x, LANE - hw, 1)
    right = pltpu.roll(x, hw, 1)
    return jnp.where(lane < hw, -left, right)


def _rope(x, cos, sin, hw):
    return x * cos + _rot_half(x, hw) * sin


def _rope_t(dy, cos, sin, hw):
    return dy * cos - _rot_half(dy, hw) * sin


def _rms_cast(name, x, g):
    S, D = x.shape
    tm = _tile(S, 512, 8)

    def body(x_ref, g_ref, o_ref):
        xv = x_ref[...]
        o_ref[...] = (xv * _rms_scale(xv) * g_ref[...]).astype(BF16)

    return _rows(name, body, S, tm, [('row', x, D, 0), ('full', g)], [(D, BF16)])[0]


def _postnorm_bwd(name, dxo, y, g, coef):
    S, D = y.shape
    tm = _tile(S, 512, 8)

    def body(d_ref, y_ref, g_ref, dy_ref, dg_ref):
        dx, dg = _rms_bwd(y_ref[...], g_ref[...], coef * d_ref[...])
        dy_ref[...] = dx.astype(BF16)
        _acc_add(dg_ref, jnp.sum(dg, axis=0, keepdims=True))

    return _rows(name, body, S, tm, [('row', dxo, D, 0), ('row', y, D, 0), ('full', g)], [(D, BF16)], [D])


def _prenorm_bwd(name, dh, x, g, dxo):
    S, D = x.shape
    tm = _tile(S, 512, 8)

    def body(dh_ref, x_ref, g_ref, d_ref, dx_ref, dg_ref):
        dx, dg = _rms_bwd(x_ref[...], g_ref[...], dh_ref[...])
        dx_ref[...] = d_ref[...] + dx
        _acc_add(dg_ref, jnp.sum(dg, axis=0, keepdims=True))

    return _rows(name, body, S, tm, [('row', dh, D, 0), ('row', x, D, 0), ('full', g), ('row', dxo, D, 0)],
                 [(D, F32)], [D])


def _loss_head(y, target):
    S, D = y.shape
    tm = _tile(S, 512, 8)

    def body(y_ref, t_ref, dy_ref, l_ref):
        e = y_ref[...] - t_ref[...]
        dy_ref[...] = e * (1.0 / D)
        row = 0.5 * jnp.mean(e * e, axis=-1, keepdims=True)
        _acc_add(l_ref, jnp.broadcast_to(jnp.sum(row, axis=0, keepdims=True), (1, LANE)))

    dy, l = _rows("loss_head", body, S, tm, [('row', y, D, 0), ('row', target, D, 0)], [(D, F32)], [LANE])
    return dy, l[0, 0]


def _ffn_up(h, wg, wu, rider=None):
    S, D = h.shape
    C, _, Fc = wg.shape
    tm = _tile(S, 1024, 8)

    sub = _tile(tm, SUB_ROWS, 8)

    def compute(ins, outs, _):
        h_ref, wg_ref, wu_ref = ins
        g_ref, u_ref, a_ref = outs
        for r in range(tm // sub):
            rows = slice(r * sub, (r + 1) * sub)
            hv = h_ref[rows, :]
            g = _dot(hv, wg_ref[...], NN)
            u = _dot(hv, wu_ref[...], NN)
            g_ref[rows, :] = g.astype(BF16)
            u_ref[rows, :] = u.astype(BF16)
            a_ref[rows, :] = (g * jax.nn.sigmoid(g) * u).astype(BF16)

    w_spec = pl.BlockSpec((None, D, Fc), lambda j, i: (j, 0, 0))
    o_spec = pl.BlockSpec((None, tm, Fc), lambda j, i: (j, i, 0))
    shp = jax.ShapeDtypeStruct((C, S, Fc), BF16)
    return _with_rider(rider, "ffn_up", (C, S // tm), compute, 3,
                       [pl.BlockSpec((tm, D), lambda j, i: (i, 0)), w_spec, w_spec],
                       [o_spec, o_spec, o_spec], [shp, shp, shp], (h, wg, wu))


def _chunk_post(name, a3, w3, x, g, coef):
    C, S, Kc = a3.shape
    D = w3.shape[2]
    tm = _tile(S, 512, 8)

    def epi(acc, ex, outs):
        x_ref, g_ref = ex
        outs[0][...] = x_ref[...] + coef * (acc * _rms_scale(acc) * g_ref[...])
        outs[1][...] = acc

    row = pl.BlockSpec((tm, D), lambda i, c: (i, 0))
    shp = jax.ShapeDtypeStruct((S, D), F32)
    per = 2 if C % 2 == 0 else 1
    specs = [(pl.BlockSpec((None, tm, Kc), functools.partial(lambda i, c, o: (per * c + o, i, 0), o=o)),
              pl.BlockSpec((None, Kc, D), functools.partial(lambda i, c, o: (per * c + o, 0, 0), o=o))) for o in range(per)]
    return _mm(name, [(a3, w3)] * per, specs, NN, (S // tm, C // per), 1, (tm, D), [shp, shp], [row, row], epi,
               extras=[x, g], extra_specs=[row, pl.BlockSpec((1, D), lambda i, c: (0, 0))])


def _ffn_da(dy, wd, gate, up):
    S, D = dy.shape
    C, Fc, _ = wd.shape
    tm = _tile(S, 1024, 8)

    def body(dy_ref, wd_ref, g_ref, u_ref, dg_ref, du_ref):
        da = _dot(dy_ref[...], wd_ref[...], NT)
        g = g_ref[...].astype(F32)
        u = u_ref[...].astype(F32)
        sig = jax.nn.sigmoid(g)
        dg_ref[...] = (da * u * (sig * (1.0 + g * (1.0 - sig)))).astype(BF16)
        du_ref[...] = (da * (g * sig)).astype(BF16)

    blk = pl.BlockSpec((None, tm, Fc), lambda c, i: (c, i, 0))
    shp = jax.ShapeDtypeStruct((C, S, Fc), BF16)
    return pl.pallas_call(body, name="ffn_da", grid=(C, S // tm),
                          in_specs=[pl.BlockSpec((tm, D), lambda c, i: (i, 0)),
                                    pl.BlockSpec((None, Fc, D), lambda c, i: (c, 0, 0)), blk, blk],
                          out_specs=[blk, blk], out_shape=[shp, shp],
                          compiler_params=_params(("arbitrary", "arbitrary")))(dy, wd, gate, up)


def _ffn_dh(dg, du, wg, wu, x, g, dxo, rider=None):
    C, S, Fc = dg.shape
    D = wg.shape[1]
    tm = _tile(S, 512, 8)

    sub = _tile(tm, SUB_ROWS, 8)

    def epi(acc, ex, outs):
        x_ref, g_ref, d_ref = ex
        gain_sum = jnp.zeros((1, D), F32)
        for r in range(tm // sub):
            rows = slice(r * sub, (r + 1) * sub)
            dx, dgain = _rms_bwd(x_ref[rows, :], g_ref[...], acc[rows, :])
            outs[0][rows, :] = d_ref[rows, :] + dx
            gain_sum = gain_sum + jnp.sum(dgain, axis=0, keepdims=True)
        _acc_add(outs[1], gain_sum)

    a_spec = pl.BlockSpec((None, tm, Fc), lambda i, c: (c, i, 0))
    w_spec = pl.BlockSpec((None, D, Fc), lambda i, c: (c, 0, 0))
    row = pl.BlockSpec((tm, D), lambda i, c: (i, 0))
    vec = pl.BlockSpec((1, D), lambda i, c: (0, 0))
    res = _mm("ffn_dh", [(dg, wg), (du, wu)], [(a_spec, w_spec), (a_spec, w_spec)], NT, (S // tm, C), 1, (tm, D),
              [jax.ShapeDtypeStruct((S, D), F32), jax.ShapeDtypeStruct((1, D), F32)], [row, vec], epi,
              extras=[x, g, dxo], extra_specs=[row, vec, row], rider=rider, acc_by_ref=True)
    (dx, d_gain), rode = (res, None) if rider is None else res
    return dx, d_gain, rode


def _ffn_fwd(x, p, rider=None):
    h = _rms_cast("ffn_prenorm", x, p['pre_g'])
    (gate, up, act), rode = _ffn_up(h, p['w_gate'], p['w_up'], rider)
    if p.get('w_down') is None:
        p['w_down'], rode = rode[0], rode[1:]
    x_out, y = _chunk_post("ffn_down", act, p['w_down'], x, p['post_g'], 0.5)
    return x_out, (x, h, gate, up, act, y), rode


def _ffn_bwd(dxo, p, saved, riding=(), drain=False):
    x, h, gate, up, act, y = saved
    dy, d_post = _postnorm_bwd("ffn_postnorm_bwd", dxo, y, p['post_g'], 0.5)
    dgate, dup = _ffn_da(dy, p['w_down'], gate, up)
    d_wd = _mm_tn_chunks_a("ffn_dwd", act, dy, BF16)
    own = None
    dh_args = (dgate, dup, p['w_gate'], p['w_up'], x, p['pre_g'], dxo)
    if drain:
        d_wg, got_wd = _mm_tn_chunks_b("ffn_dwg", h, dgate, BF16, rider=_exchange_rider([d_wd]))
        d_wu, got_wg = _mm_tn_chunks_b("ffn_dwu", h, dup, BF16, rider=_exchange_rider([d_wg]))
        dx, d_pre, got = _ffn_dh(*dh_args, _exchange_rider(list(riding) + [d_wu]))
        own, got = dict(w_down=got_wd[0], w_gate=got_wg[0], w_up=got[-1]), got[:-1]
    else:
        dx, d_pre, got = _ffn_dh(*dh_args, _exchange_rider(list(riding)) if riding else None)
        d_wg = _mm_tn_chunks_b("ffn_dwg", h, dgate, BF16)
        d_wu = _mm_tn_chunks_b("ffn_dwu", h, dup, BF16)
    return dx, dict(pre_g=d_pre, post_g=d_post, w_gate=d_wg, w_up=d_wu, w_down=d_wd), got, own


MLA_FWD_TQ, MLA_FWD_TK = 512, 512
MLA_BWD_TQ, MLA_BWD_TK = 512, 512
MLA_HEADS_PER_STEP = 2


def _causal_mask(tq, tk, off):
    r = lax.broadcasted_iota(jnp.int32, (tq, tk), 0)
    c = lax.broadcasted_iota(jnp.int32, (tq, tk), 1)
    return r + off >= c


def _grid_ends(grid):
    def all_at(targets):
        hit = pl.program_id(0) == targets[0]
        for ax in range(1, len(grid)):
            hit = jnp.logical_and(hit, pl.program_id(ax) == targets[ax])
        return hit

    return (lambda: all_at([0] * len(grid))), (lambda: all_at([g - 1 for g in grid]))


def _with_rider(rider, name, grid, compute, n_in, in_specs, out_specs, out_shape, arrays, scratch=()):
    any_spec = pl.BlockSpec(memory_space=pl.ANY)
    r = 0 if rider is None else rider.n
    outs = pl.pallas_call(
        _ride(rider, n_in, len(out_shape), len(scratch), *_grid_ends(grid), compute), name=name, grid=grid,
        in_specs=list(in_specs) + [any_spec] * r, out_specs=list(out_specs) + [any_spec] * r,
        out_shape=list(out_shape) + ([] if rider is None else rider.out_shapes),
        scratch_shapes=list(scratch) + ([] if rider is None else rider.scratch),
        compiler_params=_params(("arbitrary",) * len(grid)))(*arrays, *([] if rider is None else rider.arrays))
    return outs[:len(out_shape)], outs[len(out_shape):]


def _mla_fwd(q, k, kv, rider=None):
    S = q.shape[0]
    tq, tk = _tile(S, MLA_FWD_TQ, LANE), _tile(S, MLA_FWD_TK, LANE)
    nq = S // tq
    n_edge = max(1, tq // tk)
    scale = 1.0 / math.sqrt(HEAD_DIM + ROPE_DIM)

    hp = MLA_HEADS_PER_STEP

    def compute(ins, outs, _):
        q_ref, k_ref, v_ref = ins
        o_ref, lse_ref = outs
        i = pl.program_id(1)

        def step(j, carry, masked):
            rows = pl.ds(pl.multiple_of(j * tk, tk), tk)
            out = []
            for hh, (m, l, acc) in enumerate(carry):
                qk = slice(hh * QK_PAD, (hh + 1) * QK_PAD)
                s = _dot(q_ref[:, qk], k_ref[rows, qk], NT) * scale
                if masked:
                    s = jnp.where(_causal_mask(tq, tk, i * tq - j * tk), s, NEG)
                m_new = jnp.maximum(m, jnp.max(s, axis=-1, keepdims=True))
                alpha = jnp.exp(m - m_new)
                pr = jnp.exp(s - m_new)
                l = alpha * l + jnp.sum(pr, axis=-1, keepdims=True)
                acc = alpha * acc + _dot(pr.astype(BF16), v_ref[rows, hh * HEAD_DIM:(hh + 1) * HEAD_DIM], NN)
                out.append((m_new, l, acc))
            return tuple(out)

        init = tuple((jnp.full((tq, 1), NEG, F32), jnp.zeros((tq, 1), F32), jnp.zeros((tq, HEAD_DIM), F32))
                     for _ in range(hp))
        n_full = (i * tq) // tk
        carry = lax.fori_loop(0, n_full, lambda j, c: step(j, c, False), init)
        for e in range(n_edge):
            carry = step(n_full + e, carry, True)
        for hh, (m, l, acc) in enumerate(carry):
            o_ref[:, hh * HEAD_DIM:(hh + 1) * HEAD_DIM] = (acc / l).astype(BF16)
            lse_ref[hh] = jnp.broadcast_to(m + jnp.log(l), (tq, LANE))

    return _with_rider(
        rider, "mla_fwd", (HEADS // hp, nq), compute, 3,
        [pl.BlockSpec((tq, hp * QK_PAD), lambda h, i: (i, h)), pl.BlockSpec((S, hp * QK_PAD), lambda h, i: (0, h)),
         pl.BlockSpec((S, hp * HEAD_DIM), lambda h, i: (0, HEADS // hp + h))],
        [pl.BlockSpec((tq, hp * HEAD_DIM), lambda h, i: (i, h)), pl.BlockSpec((hp, tq, LANE), lambda h, i: (h, i, 0))],
        [jax.ShapeDtypeStruct((S, HEADS * HEAD_DIM), BF16), jax.ShapeDtypeStruct((HEADS, S, LANE), F32)], (q, k, kv))


def _mla_stats(do, o, lse):
    S = o.shape[0]
    t = _tile(S, 512, 8)

    def body(do_ref, o_ref, lse_ref, st_ref):
        delta = jnp.sum(do_ref[...].astype(F32) * o_ref[...].astype(F32), axis=-1, keepdims=True)
        lane = lax.broadcasted_iota(jnp.int32, (t, LANE), 1)
        st_ref[...] = jnp.where(lane < LANE // 2, lse_ref[...], jnp.broadcast_to(delta, (t, LANE)))

    blk = pl.BlockSpec((t, HEAD_DIM), lambda h, i: (i, h))
    st = pl.BlockSpec((None, t, LANE), lambda h, i: (h, i, 0))
    return pl.pallas_call(body, name="mla_stats", grid=(HEADS, S // t), in_specs=[blk, blk, st], out_specs=st,
                          out_shape=jax.ShapeDtypeStruct((HEADS, S, LANE), F32),
                          compiler_params=_params(("arbitrary", "arbitrary")))(do, o, lse)


def _mla_bwd(q, k, kv, do, stats, rider=None):
    S = q.shape[0]
    tq, tk = _tile(S, MLA_BWD_TQ, LANE), _tile(S, MLA_BWD_TK, LANE)
    nq, nk = S // tq, S // tk
    n_edge = max(1, tk // tq)
    scale = 1.0 / math.sqrt(HEAD_DIM + ROPE_DIM)
    hp = MLA_HEADS_PER_STEP

    def compute(ins, outs, _):
        q_ref, do_ref, st_ref, k_ref, v_ref = ins
        dq_ref, dk_ref, dv_ref = outs
        j = pl.program_id(1)

        @pl.when(j == 0)
        def _():
            dq_ref[...] = jnp.zeros_like(dq_ref)

        def step(i, carry, masked):
            rows = pl.ds(pl.multiple_of(i * tq, tq), tq)
            out = []
            for hh, (dk, dv) in enumerate(carry):
                qk = slice(hh * QK_PAD, (hh + 1) * QK_PAD)
                vo = slice(hh * HEAD_DIM, (hh + 1) * HEAD_DIM)
                qb, kb, dob = q_ref[rows, qk], k_ref[:, qk], do_ref[rows, vo]
                pr = jnp.exp(_dot(qb, kb, NT) * scale - st_ref[hh, rows, 0:1])
                if masked:
                    pr = jnp.where(_causal_mask(tq, tk, i * tq - j * tk), pr, 0.0)
                dv = dv + _dot(pr.astype(BF16), dob, TN)
                dp = _dot(dob, v_ref[:, vo], NT)
                ds = (pr * (dp - st_ref[hh, rows, LANE // 2:LANE // 2 + 1]) * scale).astype(BF16)
                dk = dk + _dot(ds, qb, TN)
                dq_ref[rows, qk] += _dot(ds, kb, NN)
                out.append((dk, dv))
            return tuple(out)

        carry = tuple((jnp.zeros((tk, QK_PAD), F32), jnp.zeros((tk, HEAD_DIM), F32)) for _ in range(hp))
        i_edge = (j * tk) // tq
        for e in range(n_edge):
            carry = step(i_edge + e, carry, True)
        carry = lax.fori_loop(i_edge + n_edge, nq, lambda i, c: step(i, c, False), carry)
        for hh, (dk, dv) in enumerate(carry):
            dk_ref[:, hh * QK_PAD:(hh + 1) * QK_PAD] = dk
            dv_ref[:, hh * HEAD_DIM:(hh + 1) * HEAD_DIM] = dv.astype(BF16)

    once = pl.Buffered(1)
    return _with_rider(
        rider, "mla_bwd", (HEADS // hp, nk), compute, 5,
        [pl.BlockSpec((S, hp * QK_PAD), lambda h, j: (0, h), pipeline_mode=once),
         pl.BlockSpec((S, hp * HEAD_DIM), lambda h, j: (0, h), pipeline_mode=once),
         pl.BlockSpec((hp, S, LANE), lambda h, j: (h, 0, 0), pipeline_mode=once),
         pl.BlockSpec((tk, hp * QK_PAD), lambda h, j: (j, h)),
         pl.BlockSpec((tk, hp * HEAD_DIM), lambda h, j: (j, HEADS // hp + h))],
        [pl.BlockSpec((S, hp * QK_PAD), lambda h, j: (0, h), pipeline_mode=once),
         pl.BlockSpec((tk, hp * QK_PAD), lambda h, j: (j, h)),
         pl.BlockSpec((tk, hp * HEAD_DIM), lambda h, j: (j, h))],
        [jax.ShapeDtypeStruct((S, HEADS * QK_PAD), F32), jax.ShapeDtypeStruct((S, HEADS * QK_PAD), F32),
         jax.ShapeDtypeStruct((S, HEADS * HEAD_DIM), BF16)], (q, do, stats, k, kv))


DIL_BLK = 128
DIL_HB = 1
DIL_HB_BWD = 1
DIL_ROWS = 2048

B_NT = (((2,), (2,)), ((0,), (0,)))
B_NN = (((2,), (1,)), ((0,), (0,)))
B_TN = (((1,), (1,)), ((0,), (0,)))


def _dil_geometry(S, d):
    seg = DIL_BLK * d
    tr = max(seg, min(S, DIL_ROWS))
    assert S % tr == 0 and tr % seg == 0
    return seg, tr, tr // seg


def _dil_blocks(ref, cols, d, seg, n_seg, dtype):
    return jnp.stack([ref[pl.ds(g * seg + r, DIL_BLK, stride=d), cols].astype(dtype)
                      for g in range(n_seg) for r in range(d)])


def _dil_unblocks(ref, cols, d, seg, n_seg, val):
    for g in range(n_seg):
        for r in range(d):
            ref[pl.ds(g * seg + r, DIL_BLK, stride=d), cols] = val[g * d + r]


def _dil_band(n_blocks, n_edge, absent):
    shape = (n_blocks, DIL_BLK, DIL_BLK)
    b = lax.broadcasted_iota(jnp.int32, shape, 0)
    qi = lax.broadcasted_iota(jnp.int32, shape, 1)
    kj = lax.broadcasted_iota(jnp.int32, shape, 2)
    gone = jnp.logical_and(b < n_edge, absent)
    return kj >= qi + jnp.where(gone, DIL_BLK, 0), kj <= qi


def _with_before(edge, cur, d):
    return edge if cur.shape[0] == d else jnp.concatenate([edge, cur[:cur.shape[0] - d]], axis=0)


def _dilated_fwd(q, k, v, d):
    S = q.shape[0]
    seg, tr, n_seg = _dil_geometry(S, d)
    nb = n_seg * d
    W = DIL_HB * HEAD_DIM
    scale = 1.0 / math.sqrt(HEAD_DIM)

    def body(q_ref, k_ref, v_ref, kp_ref, vp_ref, o_ref, lse_ref):
        m_prev, m_cur = _dil_band(nb, d, pl.program_id(0) == 0)
        for hb in range(DIL_HB):
            cols = slice(hb * HEAD_DIM, (hb + 1) * HEAD_DIM)
            qb = _dil_blocks(q_ref, cols, d, seg, n_seg, BF16)
            kb = _dil_blocks(k_ref, cols, d, seg, n_seg, BF16)
            vb = _dil_blocks(v_ref, cols, d, seg, n_seg, BF16)
            kp = _with_before(_dil_blocks(kp_ref, cols, d, seg, 1, BF16), kb, d)
            vp = _with_before(_dil_blocks(vp_ref, cols, d, seg, 1, BF16), vb, d)
            s_p = jnp.where(m_prev, _dot(qb, kp, B_NT) * scale, NEG)
            s_c = jnp.where(m_cur, _dot(qb, kb, B_NT) * scale, NEG)
            m = jnp.maximum(jnp.max(s_p, axis=-1, keepdims=True), jnp.max(s_c, axis=-1, keepdims=True))
            l = jnp.sum(jnp.exp(s_p - m), axis=-1, keepdims=True) + jnp.sum(jnp.exp(s_c - m), axis=-1, keepdims=True)
            lse = m + jnp.log(l)
            o = _dot(jnp.exp(s_p - lse).astype(BF16), vp, B_NN) + _dot(jnp.exp(s_c - lse).astype(BF16), vb, B_NN)
            _dil_unblocks(o_ref, cols, d, seg, n_seg, o)
            _dil_unblocks(lse_ref, cols, d, seg, n_seg, jnp.broadcast_to(lse, (nb, DIL_BLK, HEAD_DIM)))

    cur = pl.BlockSpec((tr, W), lambda n, cb: (n, cb))
    prv = pl.BlockSpec((seg, W), lambda n, cb: (jnp.maximum(n * n_seg - 1, 0), cb))
    shp = jax.ShapeDtypeStruct((S, DIL_W), F32)
    return pl.pallas_call(body, name=f"dil_fwd_{d}", grid=(S // tr, HEADS // DIL_HB),
                          in_specs=[cur, cur, cur, prv, prv], out_specs=[cur, cur], out_shape=[shp, shp],
                          compiler_params=_params(("arbitrary", "arbitrary")))(q, k, v, k, v)


def _dilated_bwd(q, k, v, do, lse, dd, d):
    S = q.shape[0]
    seg, tr, n_seg = _dil_geometry(S, d)
    nb = n_seg * d
    nt = S // tr
    W = DIL_HB_BWD * HEAD_DIM
    scale = 1.0 / math.sqrt(HEAD_DIM)

    def body(q_ref, k_ref, v_ref, do_ref, lse_ref, dd_ref, kp_ref, vp_ref, qn_ref, don_ref, lsen_ref, ddn_ref,
             dq_ref, dk_ref, dv_ref):
        n = pl.program_id(0)
        m_prev, m_cur = _dil_band(nb, d, n == 0)
        m_next = _dil_band(d, d, n == nt - 1)[0]

        def pair(qb, kb, vb, dob, lse_b, dd_b, mask):
            pr = jnp.where(mask, jnp.exp(_dot(qb, kb, B_NT) * scale - lse_b), 0.0)
            ds = (pr * (_dot(dob, vb, B_NT) - dd_b) * scale).astype(BF16)
            return pr.astype(BF16), ds

        for hb in range(DIL_HB_BWD):
            cols = slice(hb * HEAD_DIM, (hb + 1) * HEAD_DIM)
            stat = cols
            blocks = lambda ref, c, n_s, dt: _dil_blocks(ref, c, d, seg, n_s, dt)
            qb, kb, vb, dob = (blocks(r, cols, n_seg, BF16) for r in (q_ref, k_ref, v_ref, do_ref))
            lse_b, dd_b = blocks(lse_ref, stat, n_seg, F32), blocks(dd_ref, stat, n_seg, F32)
            kp = _with_before(blocks(kp_ref, cols, 1, BF16), kb, d)
            vp = _with_before(blocks(vp_ref, cols, 1, BF16), vb, d)
            p_c, ds_c = pair(qb, kb, vb, dob, lse_b, dd_b, m_cur)
            p_p, ds_p = pair(qb, kp, vp, dob, lse_b, dd_b, m_prev)
            qn, don = blocks(qn_ref, cols, 1, BF16), blocks(don_ref, cols, 1, BF16)
            p_n, ds_n = pair(qn, kb[nb - d:], vb[nb - d:], don, blocks(lsen_ref, stat, 1, F32),
                             blocks(ddn_ref, stat, 1, F32), m_next)
            dq = _dot(ds_c, kb, B_NN) + _dot(ds_p, kp, B_NN)
            shift = lambda own, nxt: nxt if nb == d else jnp.concatenate([own[d:], nxt], axis=0)
            dk = _dot(ds_c, qb, B_TN) + shift(_dot(ds_p, qb, B_TN), _dot(ds_n, qn, B_TN))
            dv = _dot(p_c, dob, B_TN) + shift(_dot(p_p, dob, B_TN), _dot(p_n, don, B_TN))
            _dil_unblocks(dq_ref, cols, d, seg, n_seg, dq)
            _dil_unblocks(dk_ref, cols, d, seg, n_seg, dk)
            _dil_unblocks(dv_ref, cols, d, seg, n_seg, dv)

    cur = pl.BlockSpec((tr, W), lambda n, cb: (n, cb))
    prv = pl.BlockSpec((seg, W), lambda n, cb: (jnp.maximum(n * n_seg - 1, 0), cb))
    nxt = pl.BlockSpec((seg, W), lambda n, cb: (jnp.minimum((n + 1) * n_seg, S // seg - 1), cb))
    shp = jax.ShapeDtypeStruct((S, DIL_W), F32)
    return pl.pallas_call(body, name=f"dil_bwd_{d}", grid=(nt, HEADS // DIL_HB_BWD),
                          in_specs=[cur] * 6 + [prv, prv] + [nxt] * 4, out_specs=[cur] * 3, out_shape=[shp] * 3,
                          compiler_params=_params(("arbitrary", "arbitrary")))(q, k, v, do, lse, dd, k, v, q, do, lse, dd)


def _dil_merge(os_, lses):
    S = os_[0].shape[0]
    tm = _tile(S, 256, 8)

    def body(o0, o1, o2, l0, l1, l2, out_ref):
        ls = [l0[...], l1[...], l2[...]]
        m = jnp.maximum(jnp.maximum(ls[0], ls[1]), ls[2])
        es = [jnp.exp(l - m) for l in ls]
        den = es[0] + es[1] + es[2]
        out_ref[...] = ((es[0] * o0[...] + es[1] * o1[...] + es[2] * o2[...]) / den).astype(BF16)

    ins = [('row', a, DIL_W, 0) for a in list(os_) + list(lses)]
    return _rows("dil_merge", body, S, tm, ins, [(DIL_W, BF16)])[0]


def _dil_merge_bwd(dout, os_, lses):
    S = dout.shape[0]
    tm = _tile(S, 256, 8)

    def body(d_ref, o0, o1, o2, l0, l1, l2, do0, do1, do2, dd0, dd1, dd2):
        dv = d_ref[...]
        os3 = [o0[...], o1[...], o2[...]]
        ls = [l0[...], l1[...], l2[...]]
        m = jnp.maximum(jnp.maximum(ls[0], ls[1]), ls[2])
        es = [jnp.exp(l - m) for l in ls]
        den = es[0] + es[1] + es[2]
        ws = [e / den for e in es]
        for h in range(HEADS):
            cols = slice(h * HEAD_DIM, (h + 1) * HEAD_DIM)
            dw = [jnp.sum(dv[:, cols] * o[:, cols], axis=-1, keepdims=True) for o in os3]
            wh = [w[:, cols] for w in ws]
            mean = wh[0] * dw[0] + wh[1] * dw[1] + wh[2] * dw[2]
            for p_, (do_ref, dd_ref) in enumerate(((do0, dd0), (do1, dd1), (do2, dd2))):
                do_p = wh[p_] * dv[:, cols]
                dlse = wh[p_] * (dw[p_] - mean)
                do_ref[:, cols] = do_p
                dd_ref[:, cols] = wh[p_] * dw[p_] - dlse

    ins = [('row', a, DIL_W, 0) for a in [dout] + list(os_) + list(lses)]
    outs = _rows("dil_merge_bwd", body, S, tm, ins, [(DIL_W, F32)] * 6)
    return outs[:3], outs[3:]


def _qkv_prep(proj, gq, gkv, tabs):
    S = proj.shape[0]
    tm = _tile(S, 256, 8)
    cos_a, sin_a, cos_p, sin_p = tabs

    def body(cq, ckv, qd, kd, vd, kr, gq_ref, gkv_ref, ca, sa, cp, sp, o_cq, o_ckv, o_qd, o_kd, o_vd, o_kr):
        x = cq[...]
        o_cq[...] = (x * _rms_scale(x) * gq_ref[...]).astype(BF16)
        x = ckv[...]
        o_ckv[...] = (x * _rms_scale(x) * gkv_ref[...]).astype(BF16)
        c, s = cp[...], sp[...]
        for h in range(HEADS):
            cols = slice(h * HEAD_DIM, (h + 1) * HEAD_DIM)
            o_qd[:, cols] = _rope(qd[:, cols], c, s, PART_ROPE // 2)
            o_kd[:, cols] = _rope(kd[:, cols], c, s, PART_ROPE // 2)
        o_vd[...] = vd[...]
        o_kr[...] = _rope(kr[...], ca[...], sa[...], ROPE_DIM // 2).astype(BF16)

    ins = [('row', proj, Q_RANK, 0), ('row', proj, KV_RANK, 1), ('row', proj, DIL_W, 1), ('row', proj, DIL_W, 2),
           ('row', proj, DIL_W, 3), ('row', proj, LANE, 4 * DIL_W // LANE), ('full', gq), ('full', gkv),
           ('row', cos_a, LANE, 0), ('row', sin_a, LANE, 0), ('row', cos_p, LANE, 0), ('row', sin_p, LANE, 0)]
    return _rows("qkv_prep", body, S, tm, ins,
                 [(Q_RANK, BF16), (KV_RANK, BF16), (DIL_W, F32), (DIL_W, F32), (DIL_W, F32), (LANE, BF16)])


def _qk_finish(qa, kv, kr, tabs):
    S = qa.shape[0]
    tm = _tile(S, 256, 8)
    cos_a, sin_a = tabs[0], tabs[1]

    def body(qa_ref, kn_ref, kr_ref, ca, sa, q_ref, k_ref):
        c, s = ca[...], sa[...]
        krv = kr_ref[...]
        for h in range(HEADS):
            nope = slice(h * QK_PAD, h * QK_PAD + HEAD_DIM)
            rope = slice(h * QK_PAD + HEAD_DIM, (h + 1) * QK_PAD)
            q_ref[:, nope] = qa_ref[:, nope].astype(BF16)
            q_ref[:, rope] = _rope(qa_ref[:, rope], c, s, ROPE_DIM // 2).astype(BF16)
            k_ref[:, nope] = kn_ref[:, h * HEAD_DIM:(h + 1) * HEAD_DIM]
            k_ref[:, rope] = krv

    W = HEADS * QK_PAD
    ins = [('row', qa, W, 0), ('row', kv, DIL_W, 0), ('row', kr, LANE, 0), ('row', cos_a, LANE, 0), ('row', sin_a, LANE, 0)]
    return _rows("qk_finish", body, S, tm, ins, [(W, BF16), (W, BF16)])


def _qk_finish_bwd(dq, dk, dv, tabs):
    S = dq.shape[0]
    tm = _tile(S, 256, 8)
    cos_a, sin_a = tabs[0], tabs[1]

    def body(dq_ref, dk_ref, dv_ref, ca, sa, dqa_ref, dkv_ref, dkr_ref):
        c, s = ca[...], sa[...]
        krsum = jnp.zeros((tm, LANE), F32)
        for h in range(HEADS):
            nope = slice(h * QK_PAD, h * QK_PAD + HEAD_DIM)
            rope = slice(h * QK_PAD + HEAD_DIM, (h + 1) * QK_PAD)
            dqa_ref[:, nope] = dq_ref[:, nope].astype(BF16)
            dqa_ref[:, rope] = _rope_t(dq_ref[:, rope], c, s, ROPE_DIM // 2).astype(BF16)
            dkv_ref[:, h * HEAD_DIM:(h + 1) * HEAD_DIM] = dk_ref[:, nope].astype(BF16)
            krsum = krsum + dk_ref[:, rope]
        dkv_ref[:, DIL_W:] = dv_ref[...]
        dkr_ref[...] = _rope_t(krsum, c, s, ROPE_DIM // 2)

    W = HEADS * QK_PAD
    ins = [('row', dq, W, 0), ('row', dk, W, 0), ('row', dv, DIL_W, 0), ('row', cos_a, LANE, 0), ('row', sin_a, LANE, 0)]
    return _rows("qk_finish_bwd", body, S, tm, ins, [(W, BF16), (2 * DIL_W, BF16), (LANE, F32)])


def _qkv_prep_bwd(proj, gq, gkv, d_cqn, d_ckvn, dqs, dks, dvs, dkr, tabs):
    S = proj.shape[0]
    tm = _tile(S, 256, 8)
    cos_p, sin_p = tabs[2], tabs[3]

    def body(cq, ckv, gq_ref, gkv_ref, dcq, dckv, dq0, dq1, dq2, dk0, dk1, dk2, dv0, dv1, dv2, dkr_ref, cp, sp,
             out_ref, dgq_ref, dgkv_ref):
        dx, dg = _rms_bwd(cq[...], gq_ref[...], dcq[...])
        out_ref[:, 0:Q_RANK] = dx.astype(BF16)
        _acc_add(dgq_ref, jnp.sum(dg, axis=0, keepdims=True))
        dx, dg = _rms_bwd(ckv[...], gkv_ref[...], dckv[...])
        out_ref[:, Q_RANK:Q_RANK + KV_RANK] = dx.astype(BF16)
        _acc_add(dgkv_ref, jnp.sum(dg, axis=0, keepdims=True))
        c, s = cp[...], sp[...]
        base = Q_RANK + KV_RANK
        for h in range(HEADS):
            cols = slice(h * HEAD_DIM, (h + 1) * HEAD_DIM)
            dst = lambda part: slice(base + part * DIL_W + h * HEAD_DIM, base + part * DIL_W + (h + 1) * HEAD_DIM)
            out_ref[:, dst(0)] = _rope_t(dq0[:, cols] + dq1[:, cols] + dq2[:, cols], c, s, PART_ROPE // 2).astype(BF16)
            out_ref[:, dst(1)] = _rope_t(dk0[:, cols] + dk1[:, cols] + dk2[:, cols], c, s, PART_ROPE // 2).astype(BF16)
            out_ref[:, dst(2)] = (dv0[:, cols] + dv1[:, cols] + dv2[:, cols]).astype(BF16)
        out_ref[:, base + 3 * DIL_W:] = dkr_ref[...].astype(BF16)

    ins = [('row', proj, Q_RANK, 0), ('row', proj, KV_RANK, 1), ('full', gq), ('full', gkv),
           ('row', d_cqn, Q_RANK, 0), ('row', d_ckvn, KV_RANK, 0)]
    ins += [('row', a, DIL_W, 0) for a in list(dqs) + list(dks) + list(dvs)]
    ins += [('row', dkr, LANE, 0), ('row', cos_p, LANE, 0), ('row', sin_p, LANE, 0)]
    return _rows("qkv_prep_bwd", body, S, tm, ins, [(IN_PAD, BF16)], [Q_RANK, KV_RANK])


def _mix_fwd(x, p, tabs, rider=None):
    h = _rms_cast("mix_prenorm", x, p['pre_g'])
    proj = _mm_nn("mix_proj", h, p['w_in'], F32)
    cqn, ckvn, qd, kd, vd, kr = _qkv_prep(proj, p['q_norm_g'], p['kv_norm_g'], tabs)
    qa = _mm_nn("mla_q_up", cqn, p['w_uq'], F32, tn=1024)
    kv = _mm_nn("mla_kv_up", ckvn, p['w_ukv'], BF16, tn=1024)
    q_cat, k_cat = _qk_finish(qa, kv, kr, tabs)
    (o_a, lse_a), rode = _mla_fwd(q_cat, k_cat, kv, rider)
    o_ps, lse_ps = [], []
    for _, d in DIL_PATTERNS:
        o_p, lse_p = _dilated_fwd(qd, kd, vd, d)
        o_ps.append(o_p)
        lse_ps.append(lse_p)
    o_b = _dil_merge(o_ps, lse_ps)
    o_cat = jnp.stack([o_a, o_b])
    x_out, y = _chunk_post("mix_out", o_cat, p['w_o'].reshape(2, DIL_W, -1), x, p['post_g'], 1.0)
    return x_out, (x, h, proj, cqn, ckvn, qd, kd, vd, q_cat, k_cat, kv, o_a, lse_a, o_ps, lse_ps, o_cat, y), rode


def _mix_bwd(dxo, p, tabs, saved, rider=None):
    x, h, proj, cqn, ckvn, qd, kd, vd, q_cat, k_cat, kv, o_a, lse_a, o_ps, lse_ps, o_cat, y = saved
    dy, d_post = _postnorm_bwd("mix_postnorm_bwd", dxo, y, p['post_g'], 1.0)
    w_o = p['w_o']
    d_oa = _mm_nt("mix_do_a", dy, w_o[:DIL_W], BF16, tk=1024)
    d_ob = _mm_nt("mix_do_b", dy, w_o[DIL_W:], F32, tk=1024)
    d_wo = _mm_tn_chunks_a("mix_dwo", o_cat, dy, BF16).reshape(w_o.shape)
    stats = _mla_stats(d_oa, o_a, lse_a)
    (dq, dk, dv), rode = _mla_bwd(q_cat, k_cat, kv, d_oa, stats, rider)
    dqa, dkv, dkr = _qk_finish_bwd(dq, dk, dv, tabs)
    d_cqn = _mm_nt("mla_dcq", dqa, p['w_uq'], F32, tk=1024)
    d_ckvn = _mm_nt("mla_dckv", dkv, p['w_ukv'], F32, tk=1024)
    d_wuq = _mm_tn("mla_dwuq", cqn, dqa, BF16, tn=1024)
    d_wukv = _mm_tn("mla_dwukv", ckvn, dkv, BF16, tn=1024)
    do_ps, dd_ps = _dil_merge_bwd(d_ob, o_ps, lse_ps)
    dqs, dks, dvs = [], [], []
    for (_, d), do_p, lse_p, dd_p in zip(DIL_PATTERNS, do_ps, lse_ps, dd_ps):
        dq_p, dk_p, dv_p = _dilated_bwd(qd, kd, vd, do_p, lse_p, dd_p, d)
        dqs.append(dq_p)
        dks.append(dk_p)
        dvs.append(dv_p)
    d_proj, d_gq, d_gkv = _qkv_prep_bwd(proj, p['q_norm_g'], p['kv_norm_g'], d_cqn, d_ckvn, dqs, dks, dvs, dkr, tabs)
    dh = _mm_nt("mix_dh", d_proj, p['w_in'], F32)
    d_win = _mm_tn("mix_dwin", h, d_proj, BF16)
    dx, d_pre = _prenorm_bwd("mix_prenorm_bwd", dh, x, p['pre_g'], dxo)
    return dx, dict(pre_g=d_pre, post_g=d_post, w_in=d_win, q_norm_g=d_gq, w_uq=d_wuq, kv_norm_g=d_gkv,
                    w_ukv=d_wukv, w_o=d_wo), rode


def _cols_from_shards(g):
    return jnp.transpose(g, (1, 0, 2)).reshape(g.shape[1], -1)


def _shards_from_cols(w):
    K = w.shape[0]
    return jnp.transpose(w.reshape(K, N_DEV, -1), (1, 0, 2))


def _win_layout(g):
    w = _cols_from_shards(g)
    a = Q_RANK + KV_RANK
    return jnp.concatenate([w[:, :a], w[:, a + ROPE_DIM:], w[:, a:a + ROPE_DIM],
                            jnp.zeros((w.shape[0], IN_PAD - IN_COLS), w.dtype)], axis=1)


def _win_unlayout(dw):
    a = Q_RANK + KV_RANK
    w = jnp.concatenate([dw[:, :a], dw[:, a + 3 * DIL_W:a + 3 * DIL_W + ROPE_DIM], dw[:, a:a + 3 * DIL_W]], axis=1)
    return _shards_from_cols(w)


def _wuq_layout(g):
    return _cols_from_shards(jnp.pad(g, ((0, 0), (0, 0), (0, QK_PAD - HEAD_DIM - ROPE_DIM))))


def _wuq_unlayout(dw):
    return _shards_from_cols(dw)[:, :, :HEAD_DIM + ROPE_DIM]


def _wukv_layout(g):
    return jnp.concatenate([_cols_from_shards(g[:, :, :HEAD_DIM]), _cols_from_shards(g[:, :, HEAD_DIM:])], axis=1)


def _wukv_unlayout(dw):
    return jnp.concatenate([_shards_from_cols(dw[:, :DIL_W]), _shards_from_cols(dw[:, DIL_W:])], axis=2)


def _cast_bf16(x):
    shp = x.shape
    x2 = x.reshape(-1, shp[-1])
    R, C = x2.shape
    tr = _tile(R, 512, 8)

    def body(x_ref, o_ref):
        o_ref[...] = x_ref[...].astype(BF16)

    spec = pl.BlockSpec((tr, C), lambda i: (i, 0))
    out = pl.pallas_call(body, name="cast_bf16", grid=(R // tr,), in_specs=[spec], out_specs=spec,
                         out_shape=jax.ShapeDtypeStruct((R, C), BF16), compiler_params=_params(("arbitrary",)))(x2)
    return out.reshape(shp)


def _mesh_pos():
    x, y, c = lax.axis_index("x"), lax.axis_index("y"), lax.axis_index("c")
    return x, y, c


class _Rider:
    def __init__(self, name, arrays, out_shapes, scratch, start, finish):
        self.name, self.arrays, self.out_shapes, self.scratch = name, arrays, out_shapes, scratch
        self.start, self.finish = start, finish
        self.n = len(arrays)

    def split(self, refs):
        return refs[:self.n], refs[self.n:2 * self.n], refs[2 * self.n:]


def _run_rider(rider):
    def body(*refs):
        rider.start(*rider.split(refs))
        rider.finish(*rider.split(refs))

    any_spec = pl.BlockSpec(memory_space=pl.ANY)
    return pl.pallas_call(body, name=rider.name, in_specs=[any_spec] * rider.n, out_specs=[any_spec] * rider.n,
                          out_shape=rider.out_shapes, scratch_shapes=rider.scratch)(*rider.arrays)


def _ride(rider, n_in, n_out, n_scr, first, last, compute):
    def body(*refs):
        r = 0 if rider is None else rider.n
        o0 = n_in + r
        s0 = o0 + n_out + r
        own = (refs[:n_in], refs[o0:o0 + n_out], refs[s0:s0 + n_scr])
        if rider is None:
            compute(*own)
            return
        mine = (refs[n_in:o0], refs[o0 + n_out:s0], refs[s0 + n_scr:])

        @pl.when(first())
        def _():
            rider.start(*mine)

        compute(*own)

        @pl.when(last())
        def _():
            rider.finish(*mine)

    return body


def _gather_rider(xs):
    n = len(xs)

    def program(x_refs, o_refs, sems):
        send, recv, loc = sems
        x, y, c = _mesh_pos()
        me, sib = (x, y, c), (x, y, 1 - c)
        chips = [(1 - x, y), (x, 1 - y), (1 - x, 1 - y)]

        def slot(k, dev):
            return o_refs[k].at[4 * dev[0] + 2 * dev[1] + dev[2]]

        def copy(k, s, block, to, src=None):
            return pltpu.make_async_remote_copy(src_ref=slot(k, block) if src is None else src, dst_ref=slot(k, block),
                                                send_sem=send.at[k, s], recv_sem=recv.at[k, s],
                                                device_id=to, device_id_type=MESH)

        mine = [pltpu.make_async_copy(x_refs[k], slot(k, me), loc.at[k]) for k in range(n)]
        first = []
        for k in range(n):
            first.append(copy(k, 0, me, sib, src=x_refs[k]))
            first += [copy(k, 1 + j, me, (*chip, c), src=x_refs[k]) for j, chip in enumerate(chips)]

        def start():
            for cp in mine + first:
                cp.start()

        def finish():
            passed = []
            for k in range(n):
                for j, chip in enumerate(chips):
                    copy(k, 1 + j, (*chip, c), me).wait_recv()
                    fwd = copy(k, 4 + j, (*chip, c), sib)
                    fwd.start()
                    passed.append(fwd)
            for k in range(n):
                copy(k, 0, sib, me).wait_recv()
                for j, chip in enumerate(chips):
                    copy(k, 4 + j, (*chip, 1 - c), me).wait_recv()
            for cp in first + passed:
                cp.wait_send()
            for cp in mine:
                cp.wait()

        return start, finish

    return _Rider("all_gather", list(xs), [jax.ShapeDtypeStruct((N_DEV,) + a.shape, a.dtype) for a in xs],
                  [pltpu.SemaphoreType.DMA((n, 7)), pltpu.SemaphoreType.DMA((n, 7)), pltpu.SemaphoreType.DMA((n,))],
                  lambda *refs: program(*refs)[0](), lambda *refs: program(*refs)[1]())


def _exchange_rider(gs):
    n = len(gs)

    def program(g_refs, r_refs, sems):
        send, recv, loc = sems
        x, y, c = _mesh_pos()
        me = 4 * x + 2 * y + c
        mine = [pltpu.make_async_copy(g_refs[k].at[me], r_refs[k].at[me], loc.at[k]) for k in range(n)]
        out_cps, in_cps = [], []
        for k in range(n):
            for m in range(1, N_DEV):
                px = 1 - x if m & 4 else x
                py = 1 - y if m & 2 else y
                pc = 1 - c if m & 1 else c
                peer = 4 * px + 2 * py + pc
                kw = dict(src_ref=g_refs[k].at[peer], send_sem=send.at[k, m - 1], recv_sem=recv.at[k, m - 1],
                          device_id=(px, py, pc), device_id_type=MESH)
                out_cps.append(pltpu.make_async_remote_copy(dst_ref=r_refs[k].at[me], **kw))
                in_cps.append(pltpu.make_async_remote_copy(dst_ref=r_refs[k].at[peer], **kw))

        def start():
            for cp in mine + out_cps:
                cp.start()

        def finish():
            for cp in in_cps:
                cp.wait_recv()
            for cp in out_cps:
                cp.wait_send()
            for cp in mine:
                cp.wait()

        return start, finish

    return _Rider("grad_exchange", list(gs), [jax.ShapeDtypeStruct(a.shape, a.dtype) for a in gs],
                  [pltpu.SemaphoreType.DMA((n, 7)), pltpu.SemaphoreType.DMA((n, 7)), pltpu.SemaphoreType.DMA((n,))],
                  lambda *refs: program(*refs)[0](), lambda *refs: program(*refs)[1]())


def _all_reduce_small(v):
    R, C = v.shape

    def body(v_ref, o_ref, buf, send, recv):
        x, y, c = _mesh_pos()
        me = 4 * x + 2 * y + c
        buf[me] = v_ref[...]
        copies = []
        for m in range(1, N_DEV):
            px = 1 - x if m & 4 else x
            py = 1 - y if m & 2 else y
            pc = 1 - c if m & 1 else c
            peer = 4 * px + 2 * py + pc
            copies.append((pltpu.make_async_remote_copy(
                src_ref=v_ref, dst_ref=buf.at[me], send_sem=send.at[m - 1], recv_sem=recv.at[m - 1],
                device_id=(px, py, pc), device_id_type=MESH),
                pltpu.make_async_remote_copy(
                src_ref=v_ref, dst_ref=buf.at[peer], send_sem=send.at[m - 1], recv_sem=recv.at[m - 1],
                device_id=(px, py, pc), device_id_type=MESH)))
        for out_cp, _ in copies:
            out_cp.start()
        for _, in_cp in copies:
            in_cp.wait_recv()
        for out_cp, _ in copies:
            out_cp.wait_send()
        total = buf[0]
        for s in range(1, N_DEV):
            total = total + buf[s]
        o_ref[...] = total

    vmem = pl.BlockSpec(memory_space=pltpu.VMEM)
    return pl.pallas_call(
        body, name="all_reduce_small", in_specs=[vmem], out_specs=vmem, out_shape=jax.ShapeDtypeStruct((R, C), F32),
        scratch_shapes=[pltpu.VMEM((N_DEV, R, C), F32), pltpu.SemaphoreType.DMA((7,)), pltpu.SemaphoreType.DMA((7,))],
    )(v)


def _adamw(name, parts, w, m, v):
    L, R, C = w.shape
    P = parts[0].shape[0]
    tr = _tile(R, max(16, ADAMW_TILE_ELEMS // C), 16)
    nr = R // tr

    def body(*refs):
        p_refs, (w_ref, m_ref, v_ref), (g_out, d_out, m_out, v_out) = refs[:L], refs[L:L + 3], refs[L + 3:]

        def update(p_ref):
            g = p_ref[0].astype(F32)
            for s in range(1, P):
                g = g + p_ref[s].astype(F32)
            m_new = ADAM_B1 * m_ref[...] + (1.0 - ADAM_B1) * g
            v_new = ADAM_B2 * v_ref[...] + (1.0 - ADAM_B2) * (g * g)
            m_hat = m_new / (1.0 - ADAM_B1 ** ADAM_STEP)
            v_hat = v_new / (1.0 - ADAM_B2 ** ADAM_STEP)
            g_out[...] = g
            d_out[...] = -ADAM_LR * (m_hat / (jnp.sqrt(v_hat) + ADAM_EPS) + ADAM_WD * w_ref[...])
            m_out[...] = m_new
            v_out[...] = v_new

        for ll in range(L):
            pl.when(pl.program_id(0) == ll)(functools.partial(update, p_refs[ll]))

    def part_spec(ll):
        def index(l, i):
            return (0, jnp.where(l == ll, i, jnp.where(l > ll, nr - 1, 0)), 0)
        return pl.BlockSpec((P, tr, C), index)

    spec = pl.BlockSpec((None, tr, C), lambda l, i: (l, i, 0))
    shp = jax.ShapeDtypeStruct((L, R, C), F32)
    return pl.pallas_call(body, name=name, grid=(L, nr),
                          in_specs=[part_spec(ll) for ll in range(L)] + [spec, spec, spec],
                          out_specs=[spec] * 4, out_shape=[shp] * 4,
                          compiler_params=_params(("arbitrary", "arbitrary")))(*parts, w, m, v)


def _rope_tables(positions):
    pos = positions.reshape(-1).astype(F32)[:, None]
    S = pos.shape[0]

    def cs(dim):
        inv = ROPE_THETA ** (-jnp.arange(0, dim, 2, dtype=F32) / dim)
        ang = pos * inv
        return jnp.cos(ang), jnp.sin(ang)

    ca, sa = cs(ROPE_DIM)
    cp, sp = cs(PART_ROPE)
    z = lambda w: jnp.zeros((S, w), F32)
    return (jnp.concatenate([ca, ca, z(LANE - ROPE_DIM)], axis=1), jnp.concatenate([sa, sa, z(LANE - ROPE_DIM)], axis=1),
            jnp.concatenate([cp, cp, jnp.ones((S, LANE - PART_ROPE), F32)], axis=1),
            jnp.concatenate([sp, sp, z(LANE - PART_ROPE)], axis=1))


def _ffn_params(tag, gathered, gains, l):
    row = lambda n: gains[n][l][None, :]
    return dict(pre_g=row(tag + '_pre_g'), post_g=row(tag + '_post_g'), w_gate=gathered[0], w_up=gathered[1],
                w_down=gathered[2])


def _mix_params(gathered, gains, l):
    row = lambda n: gains[n][l][None, :]
    w_in, w_uq, w_ukv, w_o = gathered
    return dict(pre_g=row('mix_pre_g'), post_g=row('mix_post_g'), q_norm_g=row('mla_q_norm_g'),
                kv_norm_g=row('mla_kv_norm_g'), w_in=_win_layout(w_in), w_uq=_wuq_layout(w_uq),
                w_ukv=_wukv_layout(w_ukv), w_o=w_o.reshape(-1, w_o.shape[-1]))


def _gain_grads(d1, dm, d2):
    return dict(ffn1_pre_g=d1['pre_g'], ffn1_post_g=d1['post_g'], mix_pre_g=dm['pre_g'], mix_post_g=dm['post_g'],
                mla_q_norm_g=dm['q_norm_g'], mla_kv_norm_g=dm['kv_norm_g'], ffn2_pre_g=d2['pre_g'], ffn2_post_g=d2['post_g'])


def _pack(vecs, width):
    flat = jnp.concatenate([v.reshape(-1) for v in vecs])
    per = 8 * width
    flat = jnp.pad(flat, (0, (-flat.shape[0]) % per))
    return flat.reshape(-1, width)


def _unpack(packed, shapes):
    flat = packed.reshape(-1)
    out, off = [], 0
    for shp in shapes:
        size = math.prod(shp)
        out.append(flat[off:off + size].reshape(shp))
        off += size
    return out


def kernel(x, positions, ffn1_pre_g, ffn1_post_g, ffn1_w_gate, ffn1_w_up, ffn1_w_down, mix_pre_g, mix_post_g, w_in, mla_q_norm_g, mla_w_uq, mla_kv_norm_g, mla_w_ukv, w_o, ffn2_pre_g, ffn2_post_g, ffn2_w_gate, ffn2_w_up, ffn2_w_down, loss_target, m_ffn1_pre_g, m_ffn1_post_g, m_ffn1_w_gate, m_ffn1_w_up, m_ffn1_w_down, m_mix_pre_g, m_mix_post_g, m_w_in, m_mla_q_norm_g, m_mla_w_uq, m_mla_kv_norm_g, m_mla_w_ukv, m_w_o, m_ffn2_pre_g, m_ffn2_post_g, m_ffn2_w_gate, m_ffn2_w_up, m_ffn2_w_down, v_ffn1_pre_g, v_ffn1_post_g, v_ffn1_w_gate, v_ffn1_w_up, v_ffn1_w_down, v_mix_pre_g, v_mix_post_g, v_w_in, v_mla_q_norm_g, v_mla_w_uq, v_mla_kv_norm_g, v_mla_w_ukv, v_w_o, v_ffn2_pre_g, v_ffn2_post_g, v_ffn2_w_gate, v_ffn2_w_up, v_ffn2_w_down):
    w = dict(zip(WNAMES, (ffn1_pre_g, ffn1_post_g, ffn1_w_gate, ffn1_w_up, ffn1_w_down, mix_pre_g, mix_post_g, w_in,
                          mla_q_norm_g, mla_w_uq, mla_kv_norm_g, mla_w_ukv, w_o, ffn2_pre_g, ffn2_post_g,
                          ffn2_w_gate, ffn2_w_up, ffn2_w_down)))
    mom = dict(zip(WNAMES, (m_ffn1_pre_g, m_ffn1_post_g, m_ffn1_w_gate, m_ffn1_w_up, m_ffn1_w_down, m_mix_pre_g,
                            m_mix_post_g, m_w_in, m_mla_q_norm_g, m_mla_w_uq, m_mla_kv_norm_g, m_mla_w_ukv, m_w_o,
                            m_ffn2_pre_g, m_ffn2_post_g, m_ffn2_w_gate, m_ffn2_w_up, m_ffn2_w_down)))
    var = dict(zip(WNAMES, (v_ffn1_pre_g, v_ffn1_post_g, v_ffn1_w_gate, v_ffn1_w_up, v_ffn1_w_down, v_mix_pre_g,
                            v_mix_post_g, v_w_in, v_mla_q_norm_g, v_mla_w_uq, v_mla_kv_norm_g, v_mla_w_ukv, v_w_o,
                            v_ffn2_pre_g, v_ffn2_post_g, v_ffn2_w_gate, v_ffn2_w_up, v_ffn2_w_down)))
    depth = w_in.shape[0]
    xs = x[0]
    D = xs.shape[1]
    tabs = _rope_tables(positions)

    shards = {n: _cast_bf16(w[n]) for n in BIG}
    ffn1_w, mix_w, ffn2_w = BIG[:3], BIG[3:7], BIG[7:]
    local = lambda names, l: [shards[n][l] for n in names]
    first = _run_rider(_gather_rider(local(ffn1_w[:2], 0)))
    p1 = _ffn_params('ffn1', list(first) + [None], w, 0)
    pm = p2 = None

    params, saved = [], []
    act = xs
    for l in range(depth):
        more = l + 1 < depth
        act, s1, got = _ffn_fwd(act, p1, _gather_rider(local(ffn1_w[2:] + mix_w, 0) if l == 0 else local(ffn2_w, l)))
        if l == 0:
            pm = _mix_params(got, w, 0)
        else:
            p2 = _ffn_params('ffn2', got, w, l)
        riding_w = (local(ffn1_w, l + 1) if more else []) + (local(ffn2_w, 0) if l == 0 else [])
        act, sm, got = _mix_fwd(act, pm, tabs, _gather_rider(riding_w) if riding_w else None)
        got = list(got)
        if more:
            p1_next, got = _ffn_params('ffn1', got[:3], w, l + 1), got[3:]
        if l == 0:
            p2 = _ffn_params('ffn2', got, w, 0)
        act, s2, got = _ffn_fwd(act, p2, _gather_rider(local(mix_w, l + 1)) if more else None)
        params.append((p1, pm, p2))
        saved.append((s1, sm, s2))
        if more:
            p1, pm = p1_next, _mix_params(got, w, l + 1)
    dact, loss_part = _loss_head(act, loss_target[0])

    received = [[None] * len(BIG) for _ in range(depth)]
    gain_parts = [None] * depth
    arrays = lambda items: [a for _, _, a in items]

    def keep(items, got):
        for (ll, i, _), r in zip(items, got or ()):
            received[ll][i] = r

    on_ffn, on_attn = [], []
    for l in reversed(range(depth)):
        p1, pm, p2 = params[l]
        s1, sm, s2 = saved[l]
        dact, d2, got, _ = _ffn_bwd(dact, p2, s2, arrays(on_ffn))
        keep(on_ffn, got)
        on_attn = [(l, 7, d2['w_gate']), (l, 8, d2['w_up']), (l, 9, d2['w_down'])] + on_attn
        dact, dm, got = _mix_bwd(dact, pm, tabs, sm, _exchange_rider(arrays(on_attn)))
        keep(on_attn, got)
        on_ffn = [(l, 3, _win_unlayout(dm['w_in'])), (l, 4, _wuq_unlayout(dm['w_uq'])),
                  (l, 5, _wukv_unlayout(dm['w_ukv'])), (l, 6, dm['w_o'].reshape(N_DEV, -1, dm['w_o'].shape[-1]))]
        dact, d1, got, own = _ffn_bwd(dact, p1, s1, arrays(on_ffn), drain=(l == 0))
        keep(on_ffn, got)
        if own is not None:
            received[l][0], received[l][1], received[l][2] = own['w_gate'], own['w_up'], own['w_down']
        on_ffn = [(l, 0, d1['w_gate']), (l, 1, d1['w_up'])]
        on_attn = [(l, 2, d1['w_down'])]
        gain_parts[l] = _gain_grads(d1, dm, d2)

    gain_local = [jnp.stack([gain_parts[l][n].reshape(-1) for l in range(depth)]) for n in GAINS]
    packed = _all_reduce_small(_pack(gain_local + [loss_part.reshape(1)], D))
    summed = _unpack(packed, [w[n].shape for n in GAINS] + [(1,)])
    loss = summed[-1][0]

    out = {}
    for i, n in enumerate(BIG):
        out[n] = _adamw("adamw_" + n, [received[l][i] for l in range(depth)], w[n], mom[n], var[n])
    pk = lambda d: _pack([d[n] for n in GAINS], D)[None]
    res = _adamw("adamw_gains", [_pack(summed[:-1], D)[None]], pk(w), pk(mom), pk(var))
    for t in range(4):
        for n, a in zip(GAINS, _unpack(res[t], [w[n].shape for n in GAINS])):
            out.setdefault(n, [None] * 4)[t] = a

    grads = [out[n][0] for n in WNAMES]
    deltas = [out[n][1] for n in WNAMES]
    new_m = [out[n][2] for n in WNAMES]
    new_v = [out[n][3] for n in WNAMES]
    return (loss, dact[None], *grads, *deltas, *new_m, *new_v)
```
